```python
import jax
import jax.numpy as jnp
from jax import lax
import numpy as np

D_MODEL = 2048
BATCH = 1
SEQ = 8192
DEPTH = 2
DEC_BATCH = 32
DEC_SEQ = 16
PAST_LEN = 4096

CHUNK = 64
N_EVEN = (DEPTH + 1) // 2
N_ODD = DEPTH // 2
EPS = 1e-6
ROPE_THETA = 10000.0
NEG_INF = -1e30
D_A = D_MODEL // 2
CONV_W = 3
H_B = 8
DK_B = D_MODEL // 2 // H_B
DV_B = 2 * DK_B
POOL_WINDOWS = (2, 4, 8, 16)
D_C = D_MODEL // 2
D_CG = D_C // len(POOL_WINDOWS)
POOL_HIST = max(POOL_WINDOWS) - 1
H_D = 16
KV_D = 2
G_D = H_D // KV_D
HD_D = 64
WINDOW = 128
NB = WINDOW // CHUNK
ATTN_SCALE = HD_D ** -0.5
D_FF = ((8 * D_MODEL // 3 + 255) // 256) * 256
FFN_CONV_W = 3
PLE_DIM = 256

EVEN_SPLITS = [D_A, 2 * D_A, 3 * D_A, 3 * D_A + H_B * DK_B, 3 * D_A + 2 * H_B * DK_B,
               3 * D_A + 2 * H_B * DK_B + H_B * DV_B]
EVEN_IN = 3 * D_A + 2 * H_B * DK_B + 2 * H_B * DV_B
EVEN_MIX = D_A + H_B * DV_B
ODD_SPLITS = [D_C, D_C + H_D * HD_D, D_C + (H_D + KV_D) * HD_D]
ODD_IN = D_C + (H_D + 2 * KV_D) * HD_D
ODD_MIX = D_C + H_D * HD_D

kernel_name = 'hybrid_streaming_encoder_step'


def rmsnorm(x, g):
    xf = x.astype(jnp.float32)
    y = xf * lax.rsqrt(jnp.mean(xf * xf, axis=-1, keepdims=True) + EPS)
    return (y * g.astype(jnp.float32)).astype(x.dtype)


def rope(x, pos):
    d = x.shape[-1]
    inv = 1.0 / (ROPE_THETA ** (jnp.arange(0, d, 2, dtype=jnp.float32) / d))
    ang = pos.astype(jnp.float32)[:, None] * inv[None, :]
    cos = jnp.cos(ang)[None, :, None, :]
    sin = jnp.sin(ang)[None, :, None, :]
    xf = x.astype(jnp.float32)
    x1, x2 = xf[..., : d // 2], xf[..., d // 2:]
    return jnp.concatenate([x1 * cos - x2 * sin, x1 * sin + x2 * cos], axis=-1).astype(x.dtype)


def causal_dwconv(ext, w):
    k = w.shape[0]
    t = ext.shape[1] - (k - 1)
    y = ext[:, 0:t] * w[0]
    for j in range(1, k):
        y = y + ext[:, j:j + t] * w[j]
    return y


def ret_chunk(state, qkv):
    q, k, v = qkv
    L = q.shape[1]
    lg = jnp.log1p(-jnp.exp2(-5.0 - jnp.arange(H_B, dtype=jnp.float32)))
    idx = jnp.arange(L, dtype=jnp.float32)
    rel = idx[:, None] - idx[None, :]
    dmat = jnp.where(rel >= 0, jnp.exp(jnp.maximum(rel, 0.0)[None] * lg[:, None, None]), 0.0)
    scores = jnp.einsum('bihd,bjhd->bhij', q, k) * dmat
    out = jnp.einsum('bhij,bjhe->bihe', scores, v)
    qdec = jnp.exp((idx + 1.0)[:, None] * lg[None, :])
    out = out + jnp.einsum('bihd,bhde->bihe', q * qdec[:, :, None], state)
    kdec = jnp.exp((L - 1.0 - idx)[:, None] * lg[None, :])
    new_state = jnp.exp(L * lg)[:, None, None] * state + jnp.einsum('bjhd,bjhe->bhde', k * kdec[:, :, None], v)
    return new_state, out


def retention(q, k, v, state):
    b, t = q.shape[:2]
    blk = min(t, CHUNK)
    n = t // blk

    def to_blocks(a):
        return jnp.moveaxis(a.reshape((b, n, blk) + a.shape[2:]), 1, 0)

    final, out = lax.scan(ret_chunk, state, (to_blocks(q), to_blocks(k), to_blocks(v)))
    return jnp.moveaxis(out, 0, 1).reshape(b, t, H_B, DV_B), final


def sink_softmax(scores, sinks):
    s = jnp.broadcast_to(sinks.astype(jnp.float32).reshape(KV_D, G_D, 1, 1), scores.shape[:-1] + (1,))
    return jax.nn.softmax(jnp.concatenate([scores, s], axis=-1), axis=-1)[..., :-1]


def swa_banded(q, k, v, sinks):
    b, s = q.shape[:2]
    nc = s // CHUNK
    pad = ((0, 0), (WINDOW, 0), (0, 0), (0, 0))
    kp = jnp.pad(k, pad).astype(jnp.float32).reshape(b, nc + NB, CHUNK, KV_D, HD_D)
    vp = jnp.pad(v, pad).astype(jnp.float32).reshape(b, nc + NB, CHUNK, KV_D, HD_D)
    kb = jnp.concatenate([kp[:, j:j + nc] for j in range(NB + 1)], axis=2)
    vb = jnp.concatenate([vp[:, j:j + nc] for j in range(NB + 1)], axis=2)
    qb = q.astype(jnp.float32).reshape(b, nc, CHUNK, KV_D, G_D, HD_D)
    sc = jnp.einsum('bnihgd,bnjhd->bnhgij', qb, kb) * ATTN_SCALE
    chunk_ok = (jnp.arange(nc)[:, None] + jnp.arange(NB + 1)[None, :]) >= NB
    ok = jnp.repeat(chunk_ok, CHUNK, axis=1)
    sc = jnp.where(ok[None, :, None, None, None, :], sc, NEG_INF)
    pr = sink_softmax(sc, sinks)
    o = jnp.einsum('bnhgij,bnjhd->bnihgd', pr, vb)
    return o.reshape(b, s, H_D, HD_D)


def swa_cached(q, k, v, ck, cv, sinks):
    b, t = q.shape[:2]
    kk = jnp.concatenate([ck.astype(k.dtype), k], axis=1).astype(jnp.float32)
    vv = jnp.concatenate([cv.astype(v.dtype), v], axis=1).astype(jnp.float32)
    qg = q.astype(jnp.float32).reshape(b, t, KV_D, G_D, HD_D)
    sc = jnp.einsum('bihgd,bjhd->bhgij', qg, kk) * ATTN_SCALE
    pr = sink_softmax(sc, sinks)
    o = jnp.einsum('bhgij,bjhd->bihgd', pr, vv)
    return o.reshape(b, t, H_D, HD_D)


def multiscale_pool(u, hist, pos, w_pool, scale):
    b, t, _ = u.shape
    ext = jnp.concatenate([hist.astype(u.dtype), u], axis=1)
    ef = ext.astype(jnp.float32)
    cs = jnp.concatenate([jnp.zeros((b, 1, D_C), jnp.float32), jnp.cumsum(ef, axis=1)], axis=1)
    hi = cs[:, POOL_HIST + 1:]
    cur = ef[:, POOL_HIST:]
    outs = []
    for gi, win in enumerate(POOL_WINDOWS):
        c = slice(gi * D_CG, (gi + 1) * D_CG)
        lo = cs[:, POOL_HIST + 1 - win: POOL_HIST + 1 - win + t, c]
        cnt = jnp.minimum(pos + 1, win).astype(jnp.float32)[None, :, None]
        d = (hi[..., c] - lo) / cnt - cur[..., c]
        outs.append(d.astype(u.dtype) @ w_pool[gi])
    y = jnp.concatenate(outs, axis=-1) * scale
    return y, ext[:, -POOL_HIST:]


def even_mixer(h, pos, conv_hist, ret_state, w_in, w_conv, gn, w_out):
    b, t, _ = h.shape
    h_a, b_a, c_a, q, k, v, g = jnp.split(h @ w_in, EVEN_SPLITS, axis=-1)
    u = c_a * h_a
    ext = jnp.concatenate([conv_hist.astype(u.dtype), u], axis=1)
    y_a = b_a * causal_dwconv(ext, w_conv)
    q = rope(q.reshape(b, t, H_B, DK_B), pos).astype(jnp.float32)
    k = rope(k.reshape(b, t, H_B, DK_B), pos).astype(jnp.float32) * (DK_B ** -0.5)
    v = v.reshape(b, t, H_B, DV_B).astype(jnp.float32)
    o, new_state = retention(q, k, v, ret_state.astype(jnp.float32))
    o = o * lax.rsqrt(jnp.mean(o * o, axis=-1, keepdims=True) + EPS) * gn.astype(jnp.float32)
    y_b = o.reshape(b, t, H_B * DV_B).astype(h.dtype) * jax.nn.silu(g)
    y = jnp.concatenate([y_a, y_b], axis=-1) @ w_out
    return y, ext[:, -(CONV_W - 1):], new_state


def odd_mixer(h, pos, pool_hist, ck, cv, kv_rows, w_in, w_pool, pool_scale, sinks, w_out):
    b, t, _ = h.shape
    u, q, k, v = jnp.split(h @ w_in, ODD_SPLITS, axis=-1)
    y_c, pool_state = multiscale_pool(u, pool_hist, pos, w_pool, pool_scale)
    q = rope(q.reshape(b, t, H_D, HD_D), pos)
    k = rope(k.reshape(b, t, KV_D, HD_D), pos)
    v = v.reshape(b, t, KV_D, HD_D)
    if ck is None:
        y_d = swa_banded(q, k, v, sinks)
        k_state, v_state = k[:, -kv_rows:], v[:, -kv_rows:]
    else:
        y_d = swa_cached(q, k, v, ck, cv, sinks)
        k_state, v_state = k, v
    y = jnp.concatenate([y_c, y_d.reshape(b, t, H_D * HD_D).astype(h.dtype)], axis=-1) @ w_out
    return y, pool_state, k_state, v_state


def conv_ffn(h, hist, w_up, w_conv, w_down):
    a, gval = jnp.split(h @ w_up, 2, axis=-1)
    ext = jnp.concatenate([hist.astype(a.dtype), a], axis=1)
    z = jax.nn.gelu(causal_dwconv(ext, w_conv), approximate=True) * gval
    return z @ w_down, ext[:, -(FFN_CONV_W - 1):]


def per_layer_embedding(x, p, w_gate, w_proj, g_pre, g_post):
    gate = jax.nn.sigmoid(rmsnorm(x, g_pre) @ w_gate)
    return rmsnorm(gate * (p.astype(x.dtype) @ w_proj), g_post)


def run_trunk(x, p, pos, conv_st, ret_st, pool_st, ck, cv, ffn_st, kv_rows, w):
    conv_new, ret_new, pool_new, k_new, v_new, ffn_new = [], [], [], [], [], []
    for i in range(DEPTH):
        j = i // 2
        h = rmsnorm(x, w['norm_mix_pre'][i])
        if i % 2 == 0:
            m, c_s, r_s = even_mixer(h, pos, conv_st[j], ret_st[j], w['w_in_even'][j], w['w_conv_a'][j],
                                     w['ret_gn'][j], w['w_out_even'][j])
            conv_new.append(c_s)
            ret_new.append(r_s.astype(ret_st.dtype))
        else:
            m, p_s, k_s, v_s = odd_mixer(h, pos, pool_st[j], None if ck is None else ck[j],
                                         None if cv is None else cv[j], kv_rows, w['w_in_odd'][j],
                                         w['w_pool'][j], w['pool_scale'][j], w['sinks'][j], w['w_out_odd'][j])
            pool_new.append(p_s)
            k_new.append(k_s)
            v_new.append(v_s)
        x = x + rmsnorm(m, w['norm_mix_post'][i])
        f, f_s = conv_ffn(rmsnorm(x, w['norm_ffn_pre'][i]), ffn_st[i], w['w_up'][i], w['w_conv_ffn'][i], w['w_down'][i])
        ffn_new.append(f_s)
        x = x + rmsnorm(f, w['norm_ffn_post'][i])
        x = x + per_layer_embedding(x, p[i], w['w_ple_gate'][i], w['w_ple_proj'][i],
                                    w['norm_ple_pre'][i], w['norm_ple_post'][i])
    return x, (jnp.stack(conv_new), jnp.stack(ret_new), jnp.stack(pool_new),
               jnp.stack(k_new), jnp.stack(v_new), jnp.stack(ffn_new))


def setup_inputs(seed: int = 0) -> dict:
    key = jax.random.key(seed)
    ks = jax.random.split(key, 30)
    kv_rows = min(WINDOW, PAST_LEN)

    def nrm(i, shape, scale=1.0):
        return jax.random.normal(ks[i], shape, jnp.float32) * scale

    def gain(i, shape):
        return 1.0 + 0.05 * nrm(i, shape)

    return {
        'x_prompt': nrm(0, (BATCH, SEQ, D_MODEL)),
        'x_sample': nrm(1, (DEC_BATCH, DEC_SEQ, D_MODEL)),
        'state_conv': nrm(2, (N_EVEN, DEC_BATCH, CONV_W - 1, D_A)),
        'state_ret': nrm(3, (N_EVEN, DEC_BATCH, H_B, DK_B, DV_B)),
        'state_pool': nrm(4, (N_ODD, DEC_BATCH, POOL_HIST, D_C)),
        'cache_k': nrm(5, (N_ODD, DEC_BATCH, kv_rows, KV_D, HD_D)),
        'cache_v': nrm(6, (N_ODD, DEC_BATCH, kv_rows, KV_D, HD_D)),
        'state_ffn': nrm(7, (DEPTH, DEC_BATCH, FFN_CONV_W - 1, D_FF)),
        'p_prompt': nrm(8, (DEPTH, BATCH, SEQ, PLE_DIM)),
        'p_sample': nrm(9, (DEPTH, DEC_BATCH, DEC_SEQ, PLE_DIM)),
        'norm_mix_pre': gain(10, (DEPTH, D_MODEL)),
        'norm_mix_post': gain(11, (DEPTH, D_MODEL)),
        'norm_ffn_pre': gain(12, (DEPTH, D_MODEL)),
        'norm_ffn_post': gain(13, (DEPTH, D_MODEL)),
        'norm_ple_pre': gain(14, (DEPTH, D_MODEL)),
        'norm_ple_post': gain(15, (DEPTH, D_MODEL)),
        'w_in_even': nrm(16, (N_EVEN, D_MODEL, EVEN_IN), D_MODEL ** -0.5),
        'w_conv_a': nrm(17, (N_EVEN, CONV_W, D_A), CONV_W ** -0.5),
        'ret_gn': gain(18, (N_EVEN, H_B, DV_B)),
        'w_out_even': nrm(19, (N_EVEN, EVEN_MIX, D_MODEL), EVEN_MIX ** -0.5),
        'w_in_odd': nrm(20, (N_ODD, D_MODEL, ODD_IN), D_MODEL ** -0.5),
        'w_pool': nrm(21, (N_ODD, len(POOL_WINDOWS), D_CG, D_CG), D_CG ** -0.5),
        'pool_scale': 1.0 + 0.1 * nrm(22, (N_ODD, D_C)),
        'sinks': nrm(23, (N_ODD, H_D)),
        'w_out_odd': nrm(24, (N_ODD, ODD_MIX, D_MODEL), ODD_MIX ** -0.5),
        'w_up': nrm(25, (DEPTH, D_MODEL, 2 * D_FF), D_MODEL ** -0.5),
        'w_conv_ffn': nrm(26, (DEPTH, FFN_CONV_W, D_FF), FFN_CONV_W ** -0.5),
        'w_down': nrm(27, (DEPTH, D_FF, D_MODEL), D_FF ** -0.5),
        'w_ple_gate': nrm(28, (DEPTH, D_MODEL, D_MODEL), D_MODEL ** -0.5),
        'w_ple_proj': nrm(29, (DEPTH, PLE_DIM, D_MODEL), PLE_DIM ** -0.5),
    }


def reference(x_prompt, x_sample, state_conv, state_ret, state_pool, cache_k, cache_v, state_ffn,
              p_prompt, p_sample,
              norm_mix_pre, norm_mix_post, norm_ffn_pre, norm_ffn_post, norm_ple_pre, norm_ple_post,
              w_in_even, w_conv_a, ret_gn, w_out_even,
              w_in_odd, w_pool, pool_scale, sinks, w_out_odd,
              w_up, w_conv_ffn, w_down, w_ple_gate, w_ple_proj):
    w = dict(norm_mix_pre=norm_mix_pre, norm_mix_post=norm_mix_post, norm_ffn_pre=norm_ffn_pre,
             norm_ffn_post=norm_ffn_post, norm_ple_pre=norm_ple_pre, norm_ple_post=norm_ple_post,
             w_in_even=w_in_even, w_conv_a=w_conv_a, ret_gn=ret_gn, w_out_even=w_out_even,
             w_in_odd=w_in_odd, w_pool=w_pool, pool_scale=pool_scale, sinks=sinks, w_out_odd=w_out_odd,
             w_up=w_up, w_conv_ffn=w_conv_ffn, w_down=w_down, w_ple_gate=w_ple_gate, w_ple_proj=w_ple_proj)
    b, s = x_prompt.shape[:2]
    t = x_sample.shape[1]
    kv_rows = cache_k.shape[2]
    dt = x_prompt.dtype
    y_prompt, (conv_p, ret_p, pool_p, k_p, v_p, ffn_p) = run_trunk(
        x_prompt, p_prompt, jnp.arange(s, dtype=jnp.int32),
        jnp.zeros((N_EVEN, b, CONV_W - 1, D_A), dt), jnp.zeros((N_EVEN, b, H_B, DK_B, DV_B), dt),
        jnp.zeros((N_ODD, b, POOL_HIST, D_C), dt), None, None,
        jnp.zeros((DEPTH, b, FFN_CONV_W - 1, D_FF), dt), kv_rows, w)
    y_sample, (conv_s, ret_s, pool_s, k_s, v_s, ffn_s) = run_trunk(
        x_sample, p_sample, PAST_LEN + jnp.arange(t, dtype=jnp.int32),
        state_conv, state_ret, state_pool, cache_k, cache_v, state_ffn, kv_rows, w)
    return (y_prompt, y_sample, conv_p, conv_s, ret_p, ret_s, pool_p, pool_s, k_p, k_s, v_p, v_s, ffn_p, ffn_s)
```

```python
import functools
import math

import jax
import jax.numpy as jnp
from jax import lax
from jax.experimental import pallas as pl
from jax.experimental.pallas import tpu as pltpu

CHUNK = 64
WINDOW = 128
PAST_LEN = 4096
EPS = 1e-6
ROPE_THETA = 10000.0
NEG_INF = -1e30
POOL_WINDOWS = (2, 4, 8, 16)
POOL_HIST = max(POOL_WINDOWS) - 1

V7X_VMEM_BYTES = 64 * 1024 * 1024
VMEM_LIMIT_BYTES = 56 * 1024 * 1024

F32 = jnp.float32
BF16 = jnp.bfloat16


def _params():
    return pltpu.CompilerParams(vmem_limit_bytes=VMEM_LIMIT_BYTES)


def _rms(x, g):
    return x * lax.rsqrt(jnp.mean(x * x, axis=-1, keepdims=True) + EPS) * g


def _row_tile(rows, want):
    t = min(rows, want)
    assert rows % t == 0, (rows, t)
    return t


def _norm_matmul_kernel(x_ref, g_ref, w_ref, o_ref, h_ref):
    @pl.when(pl.program_id(1) == 0)
    def _():
        h_ref[...] = _rms(x_ref[...], g_ref[...]).astype(BF16)

    o_ref[...] = jnp.dot(h_ref[...], w_ref[...], preferred_element_type=F32).astype(o_ref.dtype)


def norm_matmul(x, g, w, *, tm, tn, out_dtype=F32):
    rows, d = x.shape
    n = w.shape[1]
    tm = _row_tile(rows, tm)
    assert n % tn == 0
    return pl.pallas_call(
        _norm_matmul_kernel,
        grid=(rows // tm, n // tn),
        in_specs=[
            pl.BlockSpec((tm, d), lambda i, j: (i, 0)),
            pl.BlockSpec((1, d), lambda i, j: (0, 0)),
            pl.BlockSpec((d, tn), lambda i, j: (0, j)),
        ],
        out_specs=pl.BlockSpec((tm, tn), lambda i, j: (i, j)),
        out_shape=jax.ShapeDtypeStruct((rows, n), out_dtype),
        scratch_shapes=[pltpu.VMEM((tm, d), BF16)],
        compiler_params=_params(),
        name="norm_matmul",
    )(x, g.reshape(1, d), w)


def _even_mixer_kernel(v_ref, g_ref, ha_ref, ba_ref, ca_ref, q_ref, k_ref, cos_ref, sin_ref,
                       chist_ref, rstate_ref, wconv_ref, gn_ref,
                       mix_ref, cstate_ref, rout_ref,
                       carry_ref, s_ref, *, n_heads, dk, dv, d_a):
    c = pl.program_id(1)
    last = pl.num_programs(1) - 1
    blk = q_ref.shape[0]

    @pl.when(c == 0)
    def _():
        carry_ref[...] = chist_ref[...]
        s_ref[...] = rstate_ref[...]

    u = ca_ref[...] * ha_ref[...]
    row = lax.broadcasted_iota(jnp.int32, u.shape, 0)
    h0 = carry_ref[0:1, :]
    h1 = carry_ref[1:2, :]
    prev1 = jnp.where(row == 0, h1, pltpu.roll(u, 1, axis=0))
    prev2 = jnp.where(row == 0, h0, jnp.where(row == 1, h1, pltpu.roll(u, 2, axis=0)))
    w = wconv_ref[...]
    conv = prev2 * w[0:1] + prev1 * w[1:2] + u * w[2:3]
    mix_ref[:, 0:d_a] = (ba_ref[...] * conv).astype(BF16)
    tail = u[blk - 2:blk, :]
    carry_ref[...] = tail

    @pl.when(c == last)
    def _():
        cstate_ref[...] = tail

    cosb = cos_ref[...]
    sinb = sin_ref[...]
    ii = lax.broadcasted_iota(jnp.int32, (blk, 1), 0).astype(F32)
    jj = lax.broadcasted_iota(jnp.int32, (1, blk), 1).astype(F32)
    rel = ii - jj
    for h in range(n_heads):
        lg = math.log1p(-(2.0 ** (-5.0 - h)))
        q = q_ref[:, h * dk:(h + 1) * dk]
        k = k_ref[:, h * dk:(h + 1) * dk]
        qr = q * cosb + pltpu.roll(q, dk // 2, axis=1) * sinb
        kr = (k * cosb + pltpu.roll(k, dk // 2, axis=1) * sinb) * (dk ** -0.5)
        vb = v_ref[:, h * dv:(h + 1) * dv].astype(BF16)
        dmat = jnp.where(rel >= 0, jnp.exp(jnp.maximum(rel, 0.0) * lg), 0.0)
        sc = lax.dot_general(qr.astype(BF16), kr.astype(BF16), (((1,), (1,)), ((), ())),
                             preferred_element_type=F32) * dmat
        o = jnp.dot(sc.astype(BF16), vb, preferred_element_type=F32)
        st = s_ref[h]
        qdec = jnp.exp((ii + 1.0) * lg)
        o = o + jnp.dot((qr * qdec).astype(BF16), st.astype(BF16), preferred_element_type=F32)
        kdec = jnp.exp((blk - 1.0 - ii) * lg)
        kv = lax.dot_general((kr * kdec).astype(BF16), vb, (((0,), (0,)), ((), ())),
                             preferred_element_type=F32)
        s_ref[h] = math.exp(blk * lg) * st + kv
        on = o * lax.rsqrt(jnp.mean(o * o, axis=-1, keepdims=True) + EPS) * gn_ref[h:h + 1, :]
        gate = g_ref[:, h * dv:(h + 1) * dv]
        mix_ref[:, d_a + h * dv:d_a + (h + 1) * dv] = (on * (gate * jax.nn.sigmoid(gate))).astype(BF16)

    @pl.when(c == last)
    def _():
        rout_ref[...] = s_ref[...]


def even_mixer(p, cosb, sinb, conv_hist, ret_state, w_conv, gn, *, blk):
    b, t, _ = p.shape
    n_heads, dk, dv = ret_state.shape[1:]
    d_a = w_conv.shape[1]
    hv = n_heads * dv
    assert hv % d_a == 0 and n_heads * dk == d_a
    base = (2 * hv) // d_a
    blk = _row_tile(t, blk)
    nc = t // blk
    wide = lambda j: pl.BlockSpec((None, blk, hv), lambda bi, ci: (bi, ci, j))
    narrow = lambda j: pl.BlockSpec((None, blk, d_a), lambda bi, ci: (bi, ci, j))
    kern = functools.partial(_even_mixer_kernel, n_heads=n_heads, dk=dk, dv=dv, d_a=d_a)
    return pl.pallas_call(
        kern,
        grid=(b, nc),
        in_specs=[
            wide(0), wide(1), narrow(base), narrow(base + 1), narrow(base + 2), narrow(base + 3),
            narrow(base + 4),
            pl.BlockSpec((blk, dk), lambda bi, ci: (ci, 0)),
            pl.BlockSpec((blk, dk), lambda bi, ci: (ci, 0)),
            pl.BlockSpec((None, 2, d_a), lambda bi, ci: (bi, 0, 0)),
            pl.BlockSpec((None, n_heads, dk, dv), lambda bi, ci: (bi, 0, 0, 0)),
            pl.BlockSpec((3, d_a), lambda bi, ci: (0, 0)),
            pl.BlockSpec((n_heads, dv), lambda bi, ci: (0, 0)),
        ],
        out_specs=[
            pl.BlockSpec((None, blk, d_a + hv), lambda bi, ci: (bi, ci, 0)),
            pl.BlockSpec((None, 2, d_a), lambda bi, ci: (bi, 0, 0)),
            pl.BlockSpec((None, n_heads, dk, dv), lambda bi, ci: (bi, 0, 0, 0)),
        ],
        out_shape=[
            jax.ShapeDtypeStruct((b, t, d_a + hv), BF16),
            jax.ShapeDtypeStruct((b, 2, d_a), F32),
            jax.ShapeDtypeStruct((b, n_heads, dk, dv), F32),
        ],
        scratch_shapes=[pltpu.VMEM((2, d_a), F32), pltpu.VMEM((n_heads, dk, dv), F32)],
        compiler_params=_params(),
        name="even_mixer",
    )(p, p, p, p, p, p, p, cosb, sinb, conv_hist, ret_state, w_conv, gn)


def _odd_mixer_kernel(sinks_ref, u_ref, q_ref, k_ref, v_ref, cos_ref, sin_ref,
                      phist_ref, ck_ref, cv_ref, wpool_ref, pscale_ref,
                      mix_ref, pstate_ref, kstate_ref, vstate_ref,
                      ext_ref, kbuf_ref, vbuf_ref,
                      *, d_c, n_q_heads, n_kv_heads, hd, pos0, has_cache, n_state_rows):
    c = pl.program_id(1)
    last = pl.num_programs(1) - 1
    blk = u_ref.shape[0]
    hist_rows = POOL_HIST + 1
    d_cg = d_c // len(POOL_WINDOWS)

    @pl.when(c == 0)
    def _():
        ext_ref[0:1, :] = jnp.zeros((1, d_c), F32)
        ext_ref[1:hist_rows, :] = phist_ref[...]
        kbuf_ref[0:WINDOW, :] = ck_ref[...]
        vbuf_ref[0:WINDOW, :] = cv_ref[...]

    u = u_ref[...]
    ext_ref[hist_rows:hist_rows + blk, :] = u
    pos = pos0 + c * blk + lax.broadcasted_iota(jnp.int32, (blk, 1), 0)
    for gi, win in enumerate(POOL_WINDOWS):
        cols = slice(gi * d_cg, (gi + 1) * d_cg)
        s = ext_ref[:, cols]
        span = 1
        while span < win:
            s = s + pltpu.roll(s, span, axis=0)
            span *= 2
        cnt = jnp.minimum(pos + 1, win).astype(F32)
        d = s[hist_rows:, :] / cnt - u[:, cols]
        y = jnp.dot(d.astype(BF16), wpool_ref[gi], preferred_element_type=F32)
        mix_ref[:, cols] = (y * pscale_ref[:, cols]).astype(BF16)
    tail = ext_ref[blk:blk + hist_rows, :]
    ext_ref[0:hist_rows, :] = tail

    @pl.when(c == last)
    def _():
        pstate_ref[...] = ext_ref[1:hist_rows, :]

    lanes = 2 * hd
    cos4 = cos_ref[...]
    sin4 = sin_ref[...]
    lane = lax.broadcasted_iota(jnp.int32, (1, lanes), 1)
    first_half = jnp.bitwise_and(lane, hd - 1) < (hd // 2)
    left = lane < hd

    def rope(x):
        rot = jnp.where(first_half, pltpu.roll(x, lanes - hd // 2, axis=1), pltpu.roll(x, hd // 2, axis=1))
        return x * cos4 + rot * sin4

    kr = rope(k_ref[...])
    vv = v_ref[...]
    kbuf_ref[WINDOW:WINDOW + blk, :] = kr
    vbuf_ref[WINDOW:WINDOW + blk, :] = vv
    kall = kbuf_ref[...]
    vall = vbuf_ref[...]
    kswap = pltpu.roll(kall, hd, axis=1)
    vswap = pltpu.roll(vall, hd, axis=1)
    nk = WINDOW + blk
    chunk_shift = CHUNK.bit_length() - 1
    qi = jnp.right_shift(lax.broadcasted_iota(jnp.int32, (blk, 1), 0), chunk_shift)
    kj = lax.broadcasted_iota(jnp.int32, (1, nk), 1)
    kc = jnp.right_shift(kj, chunk_shift) - WINDOW // CHUNK
    ok = (kc <= qi) & (kc >= qi - WINDOW // CHUNK)
    if not has_cache:
        ok = ok & ((kj >= WINDOW) | (c > 0))
    scale = hd ** -0.5
    group = n_q_heads // n_kv_heads
    zeros = jnp.zeros_like(vall)
    kv_ops = [
        (jnp.where(left, kall, kswap).astype(BF16), jnp.where(left, vall, zeros).astype(BF16),
         jnp.where(left, zeros, vswap).astype(BF16)),
        (jnp.where(left, kswap, kall).astype(BF16), jnp.where(left, vswap, zeros).astype(BF16),
         jnp.where(left, zeros, vall).astype(BF16)),
    ]
    for pair in range(n_q_heads // 2):
        kvh = (2 * pair) // group
        assert (2 * pair + 1) // group == kvh
        kext, v_l, v_r = kv_ops[kvh]
        qr = rope(q_ref[:, pair * lanes:(pair + 1) * lanes])
        out = None
        for side, vsel in ((0, v_l), (1, v_r)):
            qh = jnp.where(left if side == 0 else jnp.logical_not(left), qr, 0.0).astype(BF16)
            sc = lax.dot_general(qh, kext, (((1,), (1,)), ((), ())), preferred_element_type=F32) * scale
            sc = jnp.where(ok, sc, NEG_INF)
            sink = sinks_ref[2 * pair + side]
            m = jnp.maximum(jnp.max(sc, axis=-1, keepdims=True), sink)
            e = jnp.exp(sc - m)
            den = jnp.sum(e, axis=-1, keepdims=True) + jnp.exp(sink - m)
            pr = (e / den).astype(BF16)
            part = jnp.dot(pr, vsel, preferred_element_type=F32)
            out = part if out is None else out + part
        mix_ref[:, d_c + pair * lanes:d_c + (pair + 1) * lanes] = out.astype(BF16)
    ktail = kbuf_ref[blk:blk + WINDOW, :]
    vtail = vbuf_ref[blk:blk + WINDOW, :]
    kbuf_ref[0:WINDOW, :] = ktail
    vbuf_ref[0:WINDOW, :] = vtail

    @pl.when(c == last)
    def _():
        kstate_ref[...] = kbuf_ref[nk - n_state_rows:nk, :]
        vstate_ref[...] = vbuf_ref[nk - n_state_rows:nk, :]


def odd_mixer(p, cos4, sin4, pool_hist, cache_k, cache_v, w_pool, pool_scale, sinks, *,
              blk, pos0, has_cache, n_state_rows):
    b, t, _ = p.shape
    d_c = pool_scale.shape[-1]
    n_q_heads = sinks.shape[-1]
    lanes = cache_k.shape[-1]
    n_kv_heads = 2
    hd = lanes // n_kv_heads
    d_q = n_q_heads * hd
    assert d_q == d_c and d_c % lanes == 0
    blk = _row_tile(t, blk)
    nc = t // blk
    assert blk % CHUNK == 0 or nc == 1
    kern = functools.partial(_odd_mixer_kernel, d_c=d_c, n_q_heads=n_q_heads, n_kv_heads=n_kv_heads,
                             hd=hd, pos0=pos0, has_cache=has_cache, n_state_rows=n_state_rows)
    kcol = (d_c + d_q) // lanes
    return pl.pallas_call(
        kern,
        grid=(b, nc),
        in_specs=[
            pl.BlockSpec(memory_space=pltpu.SMEM),
            pl.BlockSpec((None, blk, d_c), lambda bi, ci: (bi, ci, 0)),
            pl.BlockSpec((None, blk, d_q), lambda bi, ci: (bi, ci, 1)),
            pl.BlockSpec((None, blk, lanes), lambda bi, ci: (bi, ci, kcol)),
            pl.BlockSpec((None, blk, lanes), lambda bi, ci: (bi, ci, kcol + 1)),
            pl.BlockSpec((blk, lanes), lambda bi, ci: (ci, 0)),
            pl.BlockSpec((blk, lanes), lambda bi, ci: (ci, 0)),
            pl.BlockSpec((None, POOL_HIST, d_c), lambda bi, ci: (bi, 0, 0)),
            pl.BlockSpec((None, WINDOW, lanes), lambda bi, ci: (bi, 0, 0)),
            pl.BlockSpec((None, WINDOW, lanes), lambda bi, ci: (bi, 0, 0)),
            pl.BlockSpec(w_pool.shape, lambda bi, ci: (0, 0, 0)),
            pl.BlockSpec((1, d_c), lambda bi, ci: (0, 0)),
        ],
        out_specs=[
            pl.BlockSpec((None, blk, d_c + d_q), lambda bi, ci: (bi, ci, 0)),
            pl.BlockSpec((None, POOL_HIST, d_c), lambda bi, ci: (bi, 0, 0)),
            pl.BlockSpec((None, n_state_rows, lanes), lambda bi, ci: (bi, 0, 0)),
            pl.BlockSpec((None, n_state_rows, lanes), lambda bi, ci: (bi, 0, 0)),
        ],
        out_shape=[
            jax.ShapeDtypeStruct((b, t, d_c + d_q), BF16),
            jax.ShapeDtypeStruct((b, POOL_HIST, d_c), F32),
            jax.ShapeDtypeStruct((b, n_state_rows, lanes), F32),
            jax.ShapeDtypeStruct((b, n_state_rows, lanes), F32),
        ],
        scratch_shapes=[
            pltpu.VMEM((POOL_HIST + 1 + blk, d_c), F32),
            pltpu.VMEM((WINDOW + blk, lanes), F32),
            pltpu.VMEM((WINDOW + blk, lanes), F32),
        ],
        compiler_params=_params(),
        name="odd_mixer",
    )(sinks, p, p, p, p, cos4, sin4, pool_hist, cache_k, cache_v, w_pool, pool_scale.reshape(1, d_c))


def _out_proj_kernel(mix_ref, w_ref, x_ref, g_ref, o_ref):
    m = jnp.dot(mix_ref[...], w_ref[...], preferred_element_type=F32)
    o_ref[...] = x_ref[...] + _rms(m, g_ref[...])


def out_proj(mix, w, x, g, *, tm):
    rows, kdim = mix.shape
    d = x.shape[1]
    tm = _row_tile(rows, tm)
    return pl.pallas_call(
        _out_proj_kernel,
        grid=(rows // tm,),
        in_specs=[
            pl.BlockSpec((tm, kdim), lambda i: (i, 0)),
            pl.BlockSpec((kdim, d), lambda i: (0, 0)),
            pl.BlockSpec((tm, d), lambda i: (i, 0)),
            pl.BlockSpec((1, d), lambda i: (0, 0)),
        ],
        out_specs=pl.BlockSpec((tm, d), lambda i: (i, 0)),
        out_shape=jax.ShapeDtypeStruct((rows, d), F32),
        compiler_params=_params(),
        name="out_proj",
    )(mix, w, x, g.reshape(1, d))


def _ffn_kernel(x_ref, gpre_ref, wa_ref, wg_ref, wconv_ref, wd_ref, gpost_ref, hist_ref,
                o_ref, state_ref, h_ref, carry_ref, *, n_seq, blocks_per_seq):
    i = pl.program_id(0)
    j = pl.program_id(1)
    last_j = pl.num_programs(1) - 1
    tm = x_ref.shape[0]
    tf = wa_ref.shape[1]
    rows_per_seq = tm // n_seq

    @pl.when(j == 0)
    def _():
        h_ref[...] = _rms(x_ref[...], gpre_ref[...]).astype(BF16)

    h = h_ref[...]
    a = jnp.dot(h, wa_ref[...], preferred_element_type=F32)
    val = jnp.dot(h, wg_ref[...], preferred_element_type=F32)

    if blocks_per_seq > 1:
        @pl.when(i % blocks_per_seq == 0)
        def _():
            carry_ref[j] = hist_ref[0]
        hist = carry_ref[j]
        hist0 = hist[0:1, :]
        hist1 = hist[1:2, :]
    else:
        hist = hist_ref[...]
        hist0 = jnp.broadcast_to(hist[:, 0:1, :], (n_seq, rows_per_seq, tf)).reshape(tm, tf)
        hist1 = jnp.broadcast_to(hist[:, 1:2, :], (n_seq, rows_per_seq, tf)).reshape(tm, tf)
    assert rows_per_seq & (rows_per_seq - 1) == 0
    t = jnp.bitwise_and(lax.broadcasted_iota(jnp.int32, (tm, 1), 0), rows_per_seq - 1)
    prev1 = jnp.where(t == 0, hist1, pltpu.roll(a, 1, axis=0))
    prev2 = jnp.where(t == 0, hist0, jnp.where(t == 1, hist1, pltpu.roll(a, 2, axis=0)))
    w = wconv_ref[...]
    conv = prev2 * w[0:1] + prev1 * w[1:2] + a * w[2:3]
    z = (jax.nn.gelu(conv, approximate=True) * val).astype(BF16)
    part = jnp.dot(z, wd_ref[...], preferred_element_type=F32)

    tail = a.reshape(n_seq, rows_per_seq, tf)[:, rows_per_seq - 2:, :]
    state_ref[j] = tail
    if blocks_per_seq > 1:
        carry_ref[j] = tail[0]

    @pl.when(j == 0)
    def _():
        o_ref[...] = part

    @pl.when(j > 0)
    def _():
        o_ref[...] += part

    @pl.when(j == last_j)
    def _():
        o_ref[...] = x_ref[...] + _rms(o_ref[...], gpost_ref[...])


def conv_ffn(x, g_pre, w_a, w_g, w_conv, w_down, g_post, hist, *, seq_len, tm, tf):
    rows, d = x.shape
    d_ff = w_a.shape[1]
    tm = _row_tile(rows, tm)
    assert d_ff % tf == 0
    if seq_len >= tm:
        assert seq_len % tm == 0
        n_seq, blocks_per_seq = 1, seq_len // tm
        seq_of = lambda i: i // blocks_per_seq
    else:
        assert tm % seq_len == 0
        n_seq, blocks_per_seq = tm // seq_len, 1
        seq_of = lambda i: i
    kern = functools.partial(_ffn_kernel, n_seq=n_seq, blocks_per_seq=blocks_per_seq)
    n_ff = d_ff // tf
    y, state = pl.pallas_call(
        kern,
        grid=(rows // tm, d_ff // tf),
        in_specs=[
            pl.BlockSpec((tm, d), lambda i, j: (i, 0)),
            pl.BlockSpec((1, d), lambda i, j: (0, 0)),
            pl.BlockSpec((d, tf), lambda i, j: (0, j)),
            pl.BlockSpec((d, tf), lambda i, j: (0, j)),
            pl.BlockSpec((3, tf), lambda i, j: (0, j)),
            pl.BlockSpec((tf, d), lambda i, j: (j, 0)),
            pl.BlockSpec((1, d), lambda i, j: (0, 0)),
            pl.BlockSpec((n_seq, 2, tf), lambda i, j: (seq_of(i), 0, j)),
        ],
        out_specs=[
            pl.BlockSpec((tm, d), lambda i, j: (i, 0)),
            pl.BlockSpec((n_ff, n_seq, 2, tf), lambda i, j: (0, seq_of(i), 0, 0)),
        ],
        out_shape=[
            jax.ShapeDtypeStruct((rows, d), F32),
            jax.ShapeDtypeStruct((n_ff, rows // seq_len, 2, tf), F32),
        ],
        scratch_shapes=[pltpu.VMEM((tm, d), BF16), pltpu.VMEM((n_ff, 2, tf), F32)],
        compiler_params=_params(),
        name="conv_ffn",
    )(x, g_pre.reshape(1, d), w_a, w_g, w_conv, w_down, g_post.reshape(1, d), hist)
    return y, jnp.moveaxis(state, 0, 2).reshape(rows // seq_len, 2, d_ff)


def _ple_kernel(x_ref, p_ref, gpre_ref, wgate_ref, wproj_ref, gpost_ref, o_ref):
    x = x_ref[...]
    gate = jax.nn.sigmoid(jnp.dot(_rms(x, gpre_ref[...]).astype(BF16), wgate_ref[...],
                                  preferred_element_type=F32))
    emb = jnp.dot(p_ref[...].astype(BF16), wproj_ref[...], preferred_element_type=F32)
    o_ref[...] = x + _rms(gate * emb, gpost_ref[...])


def per_layer_embedding(x, p, g_pre, w_gate, w_proj, g_post, *, tm):
    rows, d = x.shape
    pd = p.shape[1]
    tm = _row_tile(rows, tm)
    return pl.pallas_call(
        _ple_kernel,
        grid=(rows // tm,),
        in_specs=[
            pl.BlockSpec((tm, d), lambda i: (i, 0)),
            pl.BlockSpec((tm, pd), lambda i: (i, 0)),
            pl.BlockSpec((1, d), lambda i: (0, 0)),
            pl.BlockSpec((d, d), lambda i: (0, 0)),
            pl.BlockSpec((pd, d), lambda i: (0, 0)),
            pl.BlockSpec((1, d), lambda i: (0, 0)),
        ],
        out_specs=pl.BlockSpec((tm, d), lambda i: (i, 0)),
        out_shape=jax.ShapeDtypeStruct((rows, d), F32),
        compiler_params=_params(),
        name="per_layer_embedding",
    )(x, p, g_pre.reshape(1, d), w_gate, w_proj, g_post.reshape(1, d))


def _rope_tables(pos, dim, reps):
    inv = 1.0 / (ROPE_THETA ** (jnp.arange(0, dim, 2, dtype=F32) / dim))
    ang = pos.astype(F32)[:, None] * inv[None, :]
    cos, sin = jnp.cos(ang), jnp.sin(ang)
    return (jnp.tile(jnp.concatenate([cos, cos], axis=-1), (1, reps)),
            jnp.tile(jnp.concatenate([-sin, sin], axis=-1), (1, reps)))


def _run_trunk(x, p, pos0, conv_st, ret_st, pool_st, ck, cv, ffn_st, w, cfg):
    b, t, d = x.shape
    rows = b * t
    depth = w['norm_mix_pre'].shape[0]
    pos = pos0 + jnp.arange(t, dtype=jnp.int32)
    d_a = w['w_conv_a'].shape[-1]
    n_heads, dk, dv = ret_st.shape[2:]
    lanes = w['kv_lanes']
    hd = lanes // 2
    even_tabs = _rope_tables(pos, dk, 1)
    odd_tabs = _rope_tables(pos, hd, lanes // hd)
    has_cache = ck is not None
    xf = x.reshape(rows, d)
    conv_new, ret_new, pool_new, k_new, v_new, ffn_new = [], [], [], [], [], []
    for i in range(depth):
        j = i // 2
        if i % 2 == 0:
            proj = norm_matmul(xf, w['norm_mix_pre'][i], w['w_in_even'][j], tm=cfg['tm_in'], tn=cfg['tn_even'])
            mix, c_s, r_s = even_mixer(proj.reshape(b, t, -1), even_tabs[0], even_tabs[1], conv_st[j],
                                       ret_st[j], w['w_conv_a'][j], w['ret_gn'][j], blk=cfg['blk_even'])
            conv_new.append(c_s)
            ret_new.append(r_s)
            w_out = w['w_out_even'][j]
        else:
            proj = norm_matmul(xf, w['norm_mix_pre'][i], w['w_in_odd'][j], tm=cfg['tm_in'], tn=cfg['tn_odd'])
            if has_cache:
                ckj, cvj = ck[j].reshape(b, WINDOW, lanes), cv[j].reshape(b, WINDOW, lanes)
                n_state_rows = t
            else:
                ckj = cvj = jnp.zeros((b, WINDOW, lanes), F32)
                n_state_rows = cfg['kv_rows']
            mix, p_s, k_s, v_s = odd_mixer(proj.reshape(b, t, -1), odd_tabs[0], odd_tabs[1], pool_st[j], ckj, cvj,
                                           w['w_pool'][j], w['pool_scale'][j], w['sinks'][j],
                                           blk=cfg['blk_odd'], pos0=pos0, has_cache=has_cache,
                                           n_state_rows=n_state_rows)
            pool_new.append(p_s)
            k_new.append(k_s.reshape(b, n_state_rows, 2, hd))
            v_new.append(v_s.reshape(b, n_state_rows, 2, hd))
            w_out = w['w_out_odd'][j]
        xf = out_proj(mix.reshape(rows, -1), w_out, xf, w['norm_mix_post'][i], tm=cfg['tm_out'])
        xf, f_s = conv_ffn(xf, w['norm_ffn_pre'][i], w['w_up_a'][i], w['w_up_g'][i], w['w_conv_ffn'][i],
                           w['w_down'][i], w['norm_ffn_post'][i], ffn_st[i], seq_len=t,
                           tm=cfg['tm_ffn'], tf=cfg['tf'])
        ffn_new.append(f_s)
        xf = per_layer_embedding(xf, p[i].reshape(rows, -1), w['norm_ple_pre'][i], w['w_ple_gate'][i],
                                 w['w_ple_proj'][i], w['norm_ple_post'][i], tm=cfg['tm_ple'])
    return xf.reshape(b, t, d), (jnp.stack(conv_new), jnp.stack(ret_new), jnp.stack(pool_new),
                                 jnp.stack(k_new), jnp.stack(v_new), jnp.stack(ffn_new))


def kernel(x_prompt, x_sample, state_conv, state_ret, state_pool, cache_k, cache_v, state_ffn, p_prompt, p_sample, norm_mix_pre, norm_mix_post, norm_ffn_pre, norm_ffn_post, norm_ple_pre, norm_ple_post, w_in_even, w_conv_a, ret_gn, w_out_even, w_in_odd, w_pool, pool_scale, sinks, w_out_odd, w_up, w_conv_ffn, w_down, w_ple_gate, w_ple_proj):
    depth = norm_mix_pre.shape[0]
    n_even, n_odd = w_in_even.shape[0], w_in_odd.shape[0]
    d_a = w_conv_a.shape[-1]
    n_heads, dk, dv = state_ret.shape[2:]
    d_c = pool_scale.shape[-1]
    kv_rows, n_kv, hd = cache_k.shape[2:]
    d_q = sinks.shape[-1] * hd
    d_ff = w_conv_ffn.shape[-1]
    hk, hv = n_heads * dk, n_heads * dv

    e = w_in_even
    o0 = 3 * d_a
    w_in_even_r = jnp.concatenate(
        [e[..., o0 + 2 * hk:o0 + 2 * hk + hv], e[..., o0 + 2 * hk + hv:], e[..., :o0],
         e[..., o0:o0 + hk], e[..., o0 + hk:o0 + 2 * hk]], axis=-1).astype(BF16)
    w = dict(
        norm_mix_pre=norm_mix_pre, norm_mix_post=norm_mix_post, norm_ffn_pre=norm_ffn_pre,
        norm_ffn_post=norm_ffn_post, norm_ple_pre=norm_ple_pre, norm_ple_post=norm_ple_post,
        w_in_even=w_in_even_r, w_conv_a=w_conv_a, ret_gn=ret_gn, w_out_even=w_out_even.astype(BF16),
        w_in_odd=w_in_odd.astype(BF16), w_pool=w_pool.astype(BF16), pool_scale=pool_scale, sinks=sinks,
        w_out_odd=w_out_odd.astype(BF16), w_up_a=w_up[..., :d_ff].astype(BF16),
        w_up_g=w_up[..., d_ff:].astype(BF16), w_conv_ffn=w_conv_ffn, w_down=w_down.astype(BF16),
        w_ple_gate=w_ple_gate.astype(BF16), w_ple_proj=w_ple_proj.astype(BF16), kv_lanes=n_kv * hd)

    b, s = x_prompt.shape[:2]
    dt = x_prompt.dtype
    cfg_p = dict(tm_in=1024, tn_even=1024, tn_odd=w_in_odd.shape[-1] // 2, blk_even=256, blk_odd=128,
                 tm_out=512, tm_ffn=512, tf=512, tm_ple=512, kv_rows=kv_rows)
    cfg_s = dict(cfg_p, blk_even=x_sample.shape[1], blk_odd=x_sample.shape[1])
    y_prompt, st_p = _run_trunk(
        x_prompt, p_prompt, 0,
        jnp.zeros((n_even, b, 2, d_a), dt), jnp.zeros((n_even, b, n_heads, dk, dv), dt),
        jnp.zeros((n_odd, b, POOL_HIST, d_c), dt), None, None,
        jnp.zeros((depth, b, 2, d_ff), dt), w, cfg_p)
    y_sample, st_s = _run_trunk(
        x_sample, p_sample, PAST_LEN, state_conv, state_ret, state_pool, cache_k, cache_v, state_ffn, w, cfg_s)
    conv_p, ret_p, pool_p, k_p, v_p, ffn_p = st_p
    conv_s, ret_s, pool_s, k_s, v_s, ffn_s = st_s
    return (y_prompt, y_sample, conv_p, conv_s, ret_p, ret_s, pool_p, pool_s, k_p, k_s, v_p, v_s, ffn_p, ffn_s)
```

```python
import functools
import math

import jax
import jax.numpy as jnp
from jax import lax
from jax.experimental import pallas as pl
from jax.experimental.pallas import tpu as pltpu

CHUNK = 64
WINDOW = 128
PAST_LEN = 4096
EPS = 1e-6
ROPE_THETA = 10000.0
NEG_INF = -1e30
POOL_WINDOWS = (2, 4, 8, 16)
POOL_HIST = max(POOL_WINDOWS) - 1

V7X_VMEM_BYTES = 64 * 1024 * 1024
VMEM_LIMIT_BYTES = 56 * 1024 * 1024
MXU_COLS = 256

F32 = jnp.float32
BF16 = jnp.bfloat16


def _params():
    return pltpu.CompilerParams(vmem_limit_bytes=VMEM_LIMIT_BYTES)


def _rms(x, g):
    return x * lax.rsqrt(jnp.mean(x * x, axis=-1, keepdims=True) + EPS) * g


def _row_tile(rows, want):
    t = min(rows, want)
    assert rows % t == 0, (rows, t)
    return t


def _norm_matmul_kernel(x_ref, g_ref, w_ref, o_ref, h_ref):
    @pl.when(pl.program_id(1) == 0)
    def _():
        h_ref[...] = _rms(x_ref[...], g_ref[...]).astype(BF16)

    o_ref[...] = jnp.dot(h_ref[...], w_ref[...], preferred_element_type=F32).astype(o_ref.dtype)


def norm_matmul(x, g, w, *, tm, tn, out_dtype=F32):
    rows, d = x.shape
    n = w.shape[1]
    tm = _row_tile(rows, tm)
    assert n % tn == 0
    return pl.pallas_call(
        _norm_matmul_kernel,
        grid=(rows // tm, n // tn),
        in_specs=[
            pl.BlockSpec((tm, d), lambda i, j: (i, 0)),
            pl.BlockSpec((1, d), lambda i, j: (0, 0)),
            pl.BlockSpec((d, tn), lambda i, j: (0, j)),
        ],
        out_specs=pl.BlockSpec((tm, tn), lambda i, j: (i, j)),
        out_shape=jax.ShapeDtypeStruct((rows, n), out_dtype),
        scratch_shapes=[pltpu.VMEM((tm, d), BF16)],
        compiler_params=_params(),
        name="norm_matmul",
    )(x, g.reshape(1, d), w)


def _even_mixer_kernel(*refs, n_heads, dk, dv, d_a):
    n_wide = (n_heads * dv) // d_a
    ha_ref, ba_ref, ca_ref, q_ref, k_ref = refs[:5]
    v_refs = refs[5:5 + n_wide]
    g_refs = refs[5 + n_wide:5 + 2 * n_wide]
    (cos_ref, sin_ref, chist_ref, rstate_ref, wconv_ref, gn_ref,
     mix_ref, cstate_ref, rout_ref, carry_ref, s_ref) = refs[5 + 2 * n_wide:]
    heads_per_ref = d_a // dv

    def head_cols(wide_refs, h):
        off = (h % heads_per_ref) * dv
        return wide_refs[h // heads_per_ref][:, off:off + dv]

    c = pl.program_id(1)
    last = pl.num_programs(1) - 1
    blk = q_ref.shape[0]

    @pl.when(c == 0)
    def _():
        carry_ref[...] = chist_ref[...]
        s_ref[...] = rstate_ref[...]

    u = ca_ref[...] * ha_ref[...]
    row = lax.broadcasted_iota(jnp.int32, u.shape, 0)
    h0 = carry_ref[0:1, :]
    h1 = carry_ref[1:2, :]
    prev1 = jnp.where(row == 0, h1, pltpu.roll(u, 1, axis=0))
    prev2 = jnp.where(row == 0, h0, jnp.where(row == 1, h1, pltpu.roll(u, 2, axis=0)))
    w = wconv_ref[...]
    conv = prev2 * w[0:1] + prev1 * w[1:2] + u * w[2:3]
    mix_ref[:, 0:d_a] = (ba_ref[...] * conv).astype(BF16)
    tail = u[blk - 2:blk, :]
    carry_ref[...] = tail

    @pl.when(c == last)
    def _():
        cstate_ref[...] = tail

    cosb = cos_ref[...]
    sinb = sin_ref[...]
    ii = lax.broadcasted_iota(jnp.int32, (blk, 1), 0).astype(F32)
    jj = lax.broadcasted_iota(jnp.int32, (1, blk), 1).astype(F32)
    rel = ii - jj
    log_gamma = [math.log1p(-(2.0 ** (-5.0 - h))) for h in range(n_heads)]
    stage = []
    for h in range(n_heads):
        lg = log_gamma[h]
        q = q_ref[:, h * dk:(h + 1) * dk]
        k = k_ref[:, h * dk:(h + 1) * dk]
        qr = q * cosb + pltpu.roll(q, dk // 2, axis=1) * sinb
        kr = (k * cosb + pltpu.roll(k, dk // 2, axis=1) * sinb) * (dk ** -0.5)
        vb = head_cols(v_refs, h).astype(BF16)
        sc = lax.dot_general(qr.astype(BF16), kr.astype(BF16), (((1,), (1,)), ((), ())),
                             preferred_element_type=F32)
        st = s_ref[h]
        qdec = jnp.exp((ii + 1.0) * lg)
        inter = jnp.dot((qr * qdec).astype(BF16), st.astype(BF16), preferred_element_type=F32)
        kdec = jnp.exp((blk - 1.0 - ii) * lg)
        kv = lax.dot_general((kr * kdec).astype(BF16), vb, (((0,), (0,)), ((), ())),
                             preferred_element_type=F32)
        s_ref[h] = math.exp(blk * lg) * st + kv
        stage.append((sc, inter, vb))
    for h, (sc, inter, vb) in enumerate(stage):
        dmat = jnp.where(rel >= 0, jnp.exp(jnp.maximum(rel, 0.0) * log_gamma[h]), 0.0)
        o = jnp.dot((sc * dmat).astype(BF16), vb, preferred_element_type=F32) + inter
        on = o * lax.rsqrt(jnp.mean(o * o, axis=-1, keepdims=True) + EPS) * gn_ref[h:h + 1, :]
        gate = head_cols(g_refs, h)
        mix_ref[:, d_a + h * dv:d_a + (h + 1) * dv] = (on * (gate * jax.nn.sigmoid(gate))).astype(BF16)

    @pl.when(c == last)
    def _():
        rout_ref[...] = s_ref[...]


def even_mixer(p, cosb, sinb, conv_hist, ret_state, w_conv, gn, *, blk):
    b, t, _ = p.shape
    n_heads, dk, dv = ret_state.shape[1:]
    d_a = w_conv.shape[1]
    hv = n_heads * dv
    assert hv % d_a == 0 and d_a % dv == 0 and n_heads * dk == d_a
    n_col_blocks = 5 + 2 * (hv // d_a)
    blk = _row_tile(t, blk)
    nc = t // blk
    col_block = lambda j: pl.BlockSpec((None, blk, d_a), lambda bi, ci: (bi, ci, j))
    kern = functools.partial(_even_mixer_kernel, n_heads=n_heads, dk=dk, dv=dv, d_a=d_a)
    return pl.pallas_call(
        kern,
        grid=(b, nc),
        in_specs=[col_block(j) for j in range(n_col_blocks)] + [
            pl.BlockSpec((blk, dk), lambda bi, ci: (ci, 0)),
            pl.BlockSpec((blk, dk), lambda bi, ci: (ci, 0)),
            pl.BlockSpec((None, 2, d_a), lambda bi, ci: (bi, 0, 0)),
            pl.BlockSpec((None, n_heads, dk, dv), lambda bi, ci: (bi, 0, 0, 0)),
            pl.BlockSpec((3, d_a), lambda bi, ci: (0, 0)),
            pl.BlockSpec((n_heads, dv), lambda bi, ci: (0, 0)),
        ],
        out_specs=[
            pl.BlockSpec((None, blk, d_a + hv), lambda bi, ci: (bi, ci, 0)),
            pl.BlockSpec((None, 2, d_a), lambda bi, ci: (bi, 0, 0)),
            pl.BlockSpec((None, n_heads, dk, dv), lambda bi, ci: (bi, 0, 0, 0)),
        ],
        out_shape=[
            jax.ShapeDtypeStruct((b, t, d_a + hv), BF16),
            jax.ShapeDtypeStruct((b, 2, d_a), F32),
            jax.ShapeDtypeStruct((b, n_heads, dk, dv), F32),
        ],
        scratch_shapes=[pltpu.VMEM((2, d_a), F32), pltpu.VMEM((n_heads, dk, dv), F32)],
        compiler_params=_params(),
        name="even_mixer",
    )(*([p] * n_col_blocks), cosb, sinb, conv_hist, ret_state, w_conv, gn)


def _odd_mixer_kernel(sinks_ref, u_ref, q_ref, k_ref, v_ref, cos_ref, sin_ref,
                      phist_ref, ck_ref, cv_ref, wpool_ref, pscale_ref,
                      mix_ref, pstate_ref, kstate_ref, vstate_ref,
                      ext_ref, kbuf_ref, vbuf_ref,
                      *, d_c, n_q_heads, n_kv_heads, hd, pos0, has_cache, n_state_rows):
    c = pl.program_id(1)
    last = pl.num_programs(1) - 1
    blk = u_ref.shape[0]
    hist_rows = POOL_HIST + 1
    d_cg = d_c // len(POOL_WINDOWS)

    @pl.when(c == 0)
    def _():
        ext_ref[0:1, :] = jnp.zeros((1, d_c), F32)
        ext_ref[1:hist_rows, :] = phist_ref[...]
        kbuf_ref[0:WINDOW, :] = ck_ref[...]
        vbuf_ref[0:WINDOW, :] = cv_ref[...]

    u = u_ref[...]
    ext_ref[hist_rows:hist_rows + blk, :] = u
    pos = pos0 + c * blk + lax.broadcasted_iota(jnp.int32, (blk, 1), 0)
    for gi, win in enumerate(POOL_WINDOWS):
        cols = slice(gi * d_cg, (gi + 1) * d_cg)
        s = ext_ref[:, cols]
        span = 1
        while span < win:
            s = s + pltpu.roll(s, span, axis=0)
            span *= 2
        cnt = jnp.minimum(pos + 1, win).astype(F32)
        d = s[hist_rows:, :] / cnt - u[:, cols]
        y = jnp.dot(d.astype(BF16), wpool_ref[gi], preferred_element_type=F32)
        mix_ref[:, cols] = (y * pscale_ref[:, cols]).astype(BF16)
    tail = ext_ref[blk:blk + hist_rows, :]
    ext_ref[0:hist_rows, :] = tail

    @pl.when(c == last)
    def _():
        pstate_ref[...] = ext_ref[1:hist_rows, :]

    lanes = 2 * hd
    cos4 = cos_ref[...]
    sin4 = sin_ref[...]
    lane = lax.broadcasted_iota(jnp.int32, (1, lanes), 1)
    first_half = jnp.bitwise_and(lane, hd - 1) < (hd // 2)
    left = lane < hd

    def rope(x):
        rot = jnp.where(first_half, pltpu.roll(x, lanes - hd // 2, axis=1), pltpu.roll(x, hd // 2, axis=1))
        return x * cos4 + rot * sin4

    kr = rope(k_ref[...])
    vv = v_ref[...]
    kbuf_ref[WINDOW:WINDOW + blk, :] = kr
    vbuf_ref[WINDOW:WINDOW + blk, :] = vv
    kall = kbuf_ref[...]
    vall = vbuf_ref[...]
    kswap = pltpu.roll(kall, hd, axis=1)
    vswap = pltpu.roll(vall, hd, axis=1)
    nk = WINDOW + blk
    chunk_shift = CHUNK.bit_length() - 1
    qi = jnp.right_shift(lax.broadcasted_iota(jnp.int32, (blk, 1), 0), chunk_shift)
    kj = lax.broadcasted_iota(jnp.int32, (1, nk), 1)
    kc = jnp.right_shift(kj, chunk_shift) - WINDOW // CHUNK
    ok = (kc <= qi) & (kc >= qi - WINDOW // CHUNK)
    if not has_cache:
        ok = ok & ((kj >= WINDOW) | (c > 0))
    scale = hd ** -0.5
    group = n_q_heads // n_kv_heads
    zeros = jnp.zeros_like(vall)
    kv_ops = [
        (jnp.where(left, kall, kswap).astype(BF16), jnp.where(left, vall, zeros).astype(BF16),
         jnp.where(left, zeros, vswap).astype(BF16)),
        (jnp.where(left, kswap, kall).astype(BF16), jnp.where(left, vswap, zeros).astype(BF16),
         jnp.where(left, zeros, vall).astype(BF16)),
    ]
    scores = []
    for head in range(n_q_heads):
        kvh = head // group
        if head % 2 == 0:
            qr = rope(q_ref[:, (head // 2) * lanes:(head // 2 + 1) * lanes])
        qh = jnp.where(left if head % 2 == 0 else jnp.logical_not(left), qr, 0.0).astype(BF16)
        scores.append(lax.dot_general(qh, kv_ops[kvh][0], (((1,), (1,)), ((), ())),
                                      preferred_element_type=F32))
    probs = []
    for head, sc in enumerate(scores):
        sc = jnp.where(ok, sc * scale, NEG_INF)
        sink = sinks_ref[head]
        m = jnp.maximum(jnp.max(sc, axis=-1, keepdims=True), sink)
        e = jnp.exp(sc - m)
        den = jnp.sum(e, axis=-1, keepdims=True) + jnp.exp(sink - m)
        probs.append((e / den).astype(BF16))
    for pair in range(n_q_heads // 2):
        kvh = (2 * pair) // group
        assert (2 * pair + 1) // group == kvh
        out = (jnp.dot(probs[2 * pair], kv_ops[kvh][1], preferred_element_type=F32)
               + jnp.dot(probs[2 * pair + 1], kv_ops[kvh][2], preferred_element_type=F32))
        mix_ref[:, d_c + pair * lanes:d_c + (pair + 1) * lanes] = out.astype(BF16)
    ktail = kbuf_ref[blk:blk + WINDOW, :]
    vtail = vbuf_ref[blk:blk + WINDOW, :]
    kbuf_ref[0:WINDOW, :] = ktail
    vbuf_ref[0:WINDOW, :] = vtail

    @pl.when(c == last)
    def _():
        kstate_ref[...] = kbuf_ref[nk - n_state_rows:nk, :]
        vstate_ref[...] = vbuf_ref[nk - n_state_rows:nk, :]


def odd_mixer(p, cos4, sin4, pool_hist, cache_k, cache_v, w_pool, pool_scale, sinks, *,
              blk, pos0, has_cache, n_state_rows):
    b, t, _ = p.shape
    d_c = pool_scale.shape[-1]
    n_q_heads = sinks.shape[-1]
    lanes = cache_k.shape[-1]
    n_kv_heads = 2
    hd = lanes // n_kv_heads
    d_q = n_q_heads * hd
    assert d_q == d_c and d_c % lanes == 0
    blk = _row_tile(t, blk)
    nc = t // blk
    assert blk % CHUNK == 0 or nc == 1
    kern = functools.partial(_odd_mixer_kernel, d_c=d_c, n_q_heads=n_q_heads, n_kv_heads=n_kv_heads,
                             hd=hd, pos0=pos0, has_cache=has_cache, n_state_rows=n_state_rows)
    kcol = (d_c + d_q) // lanes
    return pl.pallas_call(
        kern,
        grid=(b, nc),
        in_specs=[
            pl.BlockSpec(memory_space=pltpu.SMEM),
            pl.BlockSpec((None, blk, d_c), lambda bi, ci: (bi, ci, 0)),
            pl.BlockSpec((None, blk, d_q), lambda bi, ci: (bi, ci, 1)),
            pl.BlockSpec((None, blk, lanes), lambda bi, ci: (bi, ci, kcol)),
            pl.BlockSpec((None, blk, lanes), lambda bi, ci: (bi, ci, kcol + 1)),
            pl.BlockSpec((blk, lanes), lambda bi, ci: (ci, 0)),
            pl.BlockSpec((blk, lanes), lambda bi, ci: (ci, 0)),
            pl.BlockSpec((None, POOL_HIST, d_c), lambda bi, ci: (bi, 0, 0)),
            pl.BlockSpec((None, WINDOW, lanes), lambda bi, ci: (bi, 0, 0)),
            pl.BlockSpec((None, WINDOW, lanes), lambda bi, ci: (bi, 0, 0)),
            pl.BlockSpec(w_pool.shape, lambda bi, ci: (0, 0, 0)),
            pl.BlockSpec((1, d_c), lambda bi, ci: (0, 0)),
        ],
        out_specs=[
            pl.BlockSpec((None, blk, d_c + d_q), lambda bi, ci: (bi, ci, 0)),
            pl.BlockSpec((None, POOL_HIST, d_c), lambda bi, ci: (bi, 0, 0)),
            pl.BlockSpec((None, n_state_rows, lanes), lambda bi, ci: (bi, 0, 0)),
            pl.BlockSpec((None, n_state_rows, lanes), lambda bi, ci: (bi, 0, 0)),
        ],
        out_shape=[
            jax.ShapeDtypeStruct((b, t, d_c + d_q), BF16),
            jax.ShapeDtypeStruct((b, POOL_HIST, d_c), F32),
            jax.ShapeDtypeStruct((b, n_state_rows, lanes), F32),
            jax.ShapeDtypeStruct((b, n_state_rows, lanes), F32),
        ],
        scratch_shapes=[
            pltpu.VMEM((POOL_HIST + 1 + blk, d_c), F32),
            pltpu.VMEM((WINDOW + blk, lanes), F32),
            pltpu.VMEM((WINDOW + blk, lanes), F32),
        ],
        compiler_params=_params(),
        name="odd_mixer",
    )(sinks, p, p, p, p, cos4, sin4, pool_hist, cache_k, cache_v, w_pool, pool_scale.reshape(1, d_c))


def _row_groups(tm, n):
    n = n if tm % (8 * n) == 0 else 1
    return [slice(r * (tm // n), (r + 1) * (tm // n)) for r in range(n)]


def _out_proj_kernel(mix_ref, w_ref, x_ref, g_ref, o_ref):
    groups = _row_groups(x_ref.shape[0], 2)
    prods = [jnp.dot(mix_ref[rows, :], w_ref[...], preferred_element_type=F32) for rows in groups]
    for rows, m in zip(groups, prods):
        o_ref[rows, :] = x_ref[rows, :] + _rms(m, g_ref[...])


def out_proj(mix, w, x, g, *, tm):
    rows, kdim = mix.shape
    d = x.shape[1]
    tm = _row_tile(rows, tm)
    return pl.pallas_call(
        _out_proj_kernel,
        grid=(rows // tm,),
        in_specs=[
            pl.BlockSpec((tm, kdim), lambda i: (i, 0)),
            pl.BlockSpec((kdim, d), lambda i: (0, 0)),
            pl.BlockSpec((tm, d), lambda i: (i, 0)),
            pl.BlockSpec((1, d), lambda i: (0, 0)),
        ],
        out_specs=pl.BlockSpec((tm, d), lambda i: (i, 0)),
        out_shape=jax.ShapeDtypeStruct((rows, d), F32),
        compiler_params=_params(),
        name="out_proj",
    )(mix, w, x, g.reshape(1, d))


def _ffn_kernel(x_ref, gpre_ref, wa_ref, wg_ref, wconv_ref, wd_ref, gpost_ref, hist_ref,
                o_ref, state_ref, h_ref, carry_ref, *, n_seq, blocks_per_seq, sub):
    i = pl.program_id(0)
    j = pl.program_id(1)
    last_j = pl.num_programs(1) - 1
    tm = x_ref.shape[0]
    tf = wa_ref.shape[1]
    rows_per_seq = tm // n_seq

    @pl.when(j == 0)
    def _():
        h_ref[...] = _rms(x_ref[...], gpre_ref[...]).astype(BF16)
        o_ref[...] = jnp.zeros_like(o_ref)

    if blocks_per_seq > 1:
        @pl.when(i % blocks_per_seq == 0)
        def _():
            carry_ref[j] = hist_ref[0]

    h = h_ref[...]
    assert rows_per_seq & (rows_per_seq - 1) == 0
    t = jnp.bitwise_and(lax.broadcasted_iota(jnp.int32, (tm, 1), 0), rows_per_seq - 1)
    groups = [slice(s * sub, (s + 1) * sub) for s in range(tf // sub)]
    ups = [(jnp.dot(h, wa_ref[:, cols], preferred_element_type=F32),
            jnp.dot(h, wg_ref[:, cols], preferred_element_type=F32)) for cols in groups]
    for cols, (a, val) in zip(groups, ups):
        if blocks_per_seq > 1:
            hist0 = carry_ref[j, 0:1, cols]
            hist1 = carry_ref[j, 1:2, cols]
        else:
            hist = hist_ref[:, :, cols]
            hist0 = jnp.broadcast_to(hist[:, 0:1, :], (n_seq, rows_per_seq, sub)).reshape(tm, sub)
            hist1 = jnp.broadcast_to(hist[:, 1:2, :], (n_seq, rows_per_seq, sub)).reshape(tm, sub)
        prev1 = jnp.where(t == 0, hist1, pltpu.roll(a, 1, axis=0))
        prev2 = jnp.where(t == 0, hist0, jnp.where(t == 1, hist1, pltpu.roll(a, 2, axis=0)))
        w = wconv_ref[:, cols]
        conv = prev2 * w[0:1] + prev1 * w[1:2] + a * w[2:3]
        z = (jax.nn.gelu(conv, approximate=True) * val).astype(BF16)
        o_ref[...] += jnp.dot(z, wd_ref[cols, :], preferred_element_type=F32)
        tail = a.reshape(n_seq, rows_per_seq, sub)[:, rows_per_seq - 2:, :]
        state_ref[j, :, :, cols] = tail
        if blocks_per_seq > 1:
            carry_ref[j, :, cols] = tail[0]

    @pl.when(j == last_j)
    def _():
        o_ref[...] = x_ref[...] + _rms(o_ref[...], gpost_ref[...])


def conv_ffn(x, g_pre, w_up, w_conv, w_down, g_post, hist, *, seq_len, tm, tf):
    rows, d = x.shape
    d_ff = w_down.shape[0]
    tm = _row_tile(rows, tm)
    assert d_ff % tf == 0 and tf % MXU_COLS == 0
    if seq_len >= tm:
        assert seq_len % tm == 0
        n_seq, blocks_per_seq = 1, seq_len // tm
        seq_of = lambda i: i // blocks_per_seq
    else:
        assert tm % seq_len == 0
        n_seq, blocks_per_seq = tm // seq_len, 1
        seq_of = lambda i: i
    kern = functools.partial(_ffn_kernel, n_seq=n_seq, blocks_per_seq=blocks_per_seq, sub=MXU_COLS)
    n_ff = d_ff // tf
    y, state = pl.pallas_call(
        kern,
        grid=(rows // tm, d_ff // tf),
        in_specs=[
            pl.BlockSpec((tm, d), lambda i, j: (i, 0)),
            pl.BlockSpec((1, d), lambda i, j: (0, 0)),
            pl.BlockSpec((d, tf), lambda i, j: (0, j)),
            pl.BlockSpec((d, tf), lambda i, j: (0, j + n_ff)),
            pl.BlockSpec((3, tf), lambda i, j: (0, j)),
            pl.BlockSpec((tf, d), lambda i, j: (j, 0)),
            pl.BlockSpec((1, d), lambda i, j: (0, 0)),
            pl.BlockSpec((n_seq, 2, tf), lambda i, j: (seq_of(i), 0, j)),
        ],
        out_specs=[
            pl.BlockSpec((tm, d), lambda i, j: (i, 0)),
            pl.BlockSpec((n_ff, n_seq, 2, tf), lambda i, j: (0, seq_of(i), 0, 0)),
        ],
        out_shape=[
            jax.ShapeDtypeStruct((rows, d), F32),
            jax.ShapeDtypeStruct((n_ff, rows // seq_len, 2, tf), F32),
        ],
        scratch_shapes=[pltpu.VMEM((tm, d), BF16), pltpu.VMEM((n_ff, 2, tf), F32)],
        compiler_params=_params(),
        name="conv_ffn",
    )(x, g_pre.reshape(1, d), w_up, w_up, w_conv, w_down, g_post.reshape(1, d), hist)
    return y, jnp.moveaxis(state, 0, 2).reshape(rows // seq_len, 2, d_ff)


def _ple_kernel(x_ref, p_ref, gpre_ref, wgate_ref, wproj_ref, gpost_ref, o_ref):
    groups = _row_groups(x_ref.shape[0], 2)
    prods = []
    for rows in groups:
        h = _rms(x_ref[rows, :], gpre_ref[...]).astype(BF16)
        prods.append((jnp.dot(h, wgate_ref[...], preferred_element_type=F32),
                      jnp.dot(p_ref[rows, :].astype(BF16), wproj_ref[...], preferred_element_type=F32)))
    for rows, (pre, emb) in zip(groups, prods):
        o_ref[rows, :] = x_ref[rows, :] + _rms(jax.nn.sigmoid(pre) * emb, gpost_ref[...])


def per_layer_embedding(x, p, g_pre, w_gate, w_proj, g_post, *, tm):
    rows, d = x.shape
    pd = p.shape[1]
    tm = _row_tile(rows, tm)
    return pl.pallas_call(
        _ple_kernel,
        grid=(rows // tm,),
        in_specs=[
            pl.BlockSpec((tm, d), lambda i: (i, 0)),
            pl.BlockSpec((tm, pd), lambda i: (i, 0)),
            pl.BlockSpec((1, d), lambda i: (0, 0)),
            pl.BlockSpec((d, d), lambda i: (0, 0)),
            pl.BlockSpec((pd, d), lambda i: (0, 0)),
            pl.BlockSpec((1, d), lambda i: (0, 0)),
        ],
        out_specs=pl.BlockSpec((tm, d), lambda i: (i, 0)),
        out_shape=jax.ShapeDtypeStruct((rows, d), F32),
        compiler_params=_params(),
        name="per_layer_embedding",
    )(x, p, g_pre.reshape(1, d), w_gate, w_proj, g_post.reshape(1, d))


def _rope_tables(pos, dim, reps):
    inv = 1.0 / (ROPE_THETA ** (jnp.arange(0, dim, 2, dtype=F32) / dim))
    ang = pos.astype(F32)[:, None] * inv[None, :]
    cos, sin = jnp.cos(ang), jnp.sin(ang)
    return (jnp.tile(jnp.concatenate([cos, cos], axis=-1), (1, reps)),
            jnp.tile(jnp.concatenate([-sin, sin], axis=-1), (1, reps)))


def _run_trunk(x, p, pos0, conv_st, ret_st, pool_st, ck, cv, ffn_st, w, cfg):
    b, t, d = x.shape
    rows = b * t
    depth = w['norm_mix_pre'].shape[0]
    pos = pos0 + jnp.arange(t, dtype=jnp.int32)
    d_a = w['w_conv_a'].shape[-1]
    n_heads, dk, dv = ret_st.shape[2:]
    lanes = w['kv_lanes']
    hd = lanes // 2
    even_tabs = _rope_tables(pos, dk, 1)
    odd_tabs = _rope_tables(pos, hd, lanes // hd)
    has_cache = ck is not None
    xf = x.reshape(rows, d)
    conv_new, ret_new, pool_new, k_new, v_new, ffn_new = [], [], [], [], [], []
    for i in range(depth):
        j = i // 2
        if i % 2 == 0:
            proj = norm_matmul(xf, w['norm_mix_pre'][i], w['w_in_even'][j], tm=cfg['tm_in'], tn=cfg['tn_even'])
            mix, c_s, r_s = even_mixer(proj.reshape(b, t, -1), even_tabs[0], even_tabs[1], conv_st[j],
                                       ret_st[j], w['w_conv_a'][j], w['ret_gn'][j], blk=cfg['blk_even'])
            conv_new.append(c_s)
            ret_new.append(r_s)
            w_out = w['w_out_even'][j]
        else:
            proj = norm_matmul(xf, w['norm_mix_pre'][i], w['w_in_odd'][j], tm=cfg['tm_in'], tn=cfg['tn_odd'])
            if has_cache:
                ckj, cvj = ck[j].reshape(b, WINDOW, lanes), cv[j].reshape(b, WINDOW, lanes)
                n_state_rows = t
            else:
                ckj = cvj = jnp.zeros((b, WINDOW, lanes), F32)
                n_state_rows = cfg['kv_rows']
            mix, p_s, k_s, v_s = odd_mixer(proj.reshape(b, t, -1), odd_tabs[0], odd_tabs[1], pool_st[j], ckj, cvj,
                                           w['w_pool'][j], w['pool_scale'][j], w['sinks'][j],
                                           blk=cfg['blk_odd'], pos0=pos0, has_cache=has_cache,
                                           n_state_rows=n_state_rows)
            pool_new.append(p_s)
            k_new.append(k_s.reshape(b, n_state_rows, 2, hd))
            v_new.append(v_s.reshape(b, n_state_rows, 2, hd))
            w_out = w['w_out_odd'][j]
        xf = out_proj(mix.reshape(rows, -1), w_out, xf, w['norm_mix_post'][i], tm=cfg['tm_out'])
        xf, f_s = conv_ffn(xf, w['norm_ffn_pre'][i], w['w_up'][i], w['w_conv_ffn'][i],
                           w['w_down'][i], w['norm_ffn_post'][i], ffn_st[i], seq_len=t,
                           tm=cfg['tm_ffn'], tf=cfg['tf'])
        ffn_new.append(f_s)
        xf = per_layer_embedding(xf, p[i].reshape(rows, -1), w['norm_ple_pre'][i], w['w_ple_gate'][i],
                                 w['w_ple_proj'][i], w['norm_ple_post'][i], tm=cfg['tm_ple'])
    return xf.reshape(b, t, d), (jnp.stack(conv_new), jnp.stack(ret_new), jnp.stack(pool_new),
                                 jnp.stack(k_new), jnp.stack(v_new), jnp.stack(ffn_new))


def kernel(x_prompt, x_sample, state_conv, state_ret, state_pool, cache_k, cache_v, state_ffn, p_prompt, p_sample, norm_mix_pre, norm_mix_post, norm_ffn_pre, norm_ffn_post, norm_ple_pre, norm_ple_post, w_in_even, w_conv_a, ret_gn, w_out_even, w_in_odd, w_pool, pool_scale, sinks, w_out_odd, w_up, w_conv_ffn, w_down, w_ple_gate, w_ple_proj):
    depth = norm_mix_pre.shape[0]
    n_even, n_odd = w_in_even.shape[0], w_in_odd.shape[0]
    d_a = w_conv_a.shape[-1]
    n_heads, dk, dv = state_ret.shape[2:]
    d_c = pool_scale.shape[-1]
    kv_rows, n_kv, hd = cache_k.shape[2:]
    d_q = sinks.shape[-1] * hd
    d_ff = w_conv_ffn.shape[-1]
    hk, hv = n_heads * dk, n_heads * dv

    w = dict(
        norm_mix_pre=norm_mix_pre, norm_mix_post=norm_mix_post, norm_ffn_pre=norm_ffn_pre,
        norm_ffn_post=norm_ffn_post, norm_ple_pre=norm_ple_pre, norm_ple_post=norm_ple_post,
        w_in_even=w_in_even.astype(BF16), w_conv_a=w_conv_a, ret_gn=ret_gn, w_out_even=w_out_even.astype(BF16),
        w_in_odd=w_in_odd.astype(BF16), w_pool=w_pool.astype(BF16), pool_scale=pool_scale, sinks=sinks,
        w_out_odd=w_out_odd.astype(BF16), w_up=w_up.astype(BF16), w_conv_ffn=w_conv_ffn, w_down=w_down.astype(BF16),
        w_ple_gate=w_ple_gate.astype(BF16), w_ple_proj=w_ple_proj.astype(BF16), kv_lanes=n_kv * hd)

    b, s = x_prompt.shape[:2]
    dt = x_prompt.dtype
    cfg_p = dict(tm_in=1024, tn_even=1024, tn_odd=w_in_odd.shape[-1] // 2, blk_even=256, blk_odd=128,
                 tm_out=512, tm_ffn=512, tf=512, tm_ple=512, kv_rows=kv_rows)
    cfg_s = dict(cfg_p, blk_even=x_sample.shape[1], blk_odd=x_sample.shape[1])
    y_prompt, st_p = _run_trunk(
        x_prompt, p_prompt, 0,
        jnp.zeros((n_even, b, 2, d_a), dt), jnp.zeros((n_even, b, n_heads, dk, dv), dt),
        jnp.zeros((n_odd, b, POOL_HIST, d_c), dt), None, None,
        jnp.zeros((depth, b, 2, d_ff), dt), w, cfg_p)
    y_sample, st_s = _run_trunk(
        x_sample, p_sample, PAST_LEN, state_conv, state_ret, state_pool, cache_k, cache_v, state_ffn, w, cfg_s)
    conv_p, ret_p, pool_p, k_p, v_p, ffn_p = st_p
    conv_s, ret_s, pool_s, k_s, v_s, ffn_s = st_s
    return (y_prompt, y_sample, conv_p, conv_s, ret_p, ret_s, pool_p, pool_s, k_p, k_s, v_p, v_s, ffn_p, ffn_s)
```

```python
import functools
import math

import jax
import jax.numpy as jnp
from jax import lax
from jax.experimental import pallas as pl
from jax.experimental.pallas import tpu as pltpu

CHUNK = 64
WINDOW = 128
PAST_LEN = 4096
EPS = 1e-6
ROPE_THETA = 10000.0
NEG_INF = -1e30
POOL_WINDOWS = (2, 4, 8, 16)
POOL_HIST = max(POOL_WINDOWS) - 1

V7X_VMEM_BYTES = 64 * 1024 * 1024
VMEM_LIMIT_BYTES = 58 * 1024 * 1024
MXU_COLS = 256

F32 = jnp.float32
BF16 = jnp.bfloat16


def _params():
    return pltpu.CompilerParams(vmem_limit_bytes=VMEM_LIMIT_BYTES)


def _rms(x, g):
    return x * lax.rsqrt(jnp.mean(x * x, axis=-1, keepdims=True) + EPS) * g


ROW_CHUNK = 256


def _for_row_chunks(n_rows, fn):
    if n_rows <= ROW_CHUNK or n_rows % ROW_CHUNK:
        fn(slice(0, n_rows))
        return

    def body(r, carry):
        fn(pl.ds(pl.multiple_of(r * ROW_CHUNK, ROW_CHUNK), ROW_CHUNK))
        return carry

    lax.fori_loop(0, n_rows // ROW_CHUNK, body, 0)


def _row_tile(rows, want):
    t = min(rows, want)
    assert rows % t == 0, (rows, t)
    return t


class LayerWeight:
    def __init__(self, stack, layer):
        self.stack, self.layer = stack, layer
        self.shape = stack.shape[1:]

    def spec(self, block, index_map):
        layer = self.layer
        return pl.BlockSpec((None,) + block, lambda *g: (layer,) + index_map(*g))


def _weight_in(w, block, index_map):
    if isinstance(w, LayerWeight):
        return (w.stack, w.spec(block, index_map), pl.BlockSpec(block, index_map),
                jax.ShapeDtypeStruct(w.shape, BF16))
    return w, pl.BlockSpec(block, index_map), None, None


def _load_weight(w_ref, copy_ref):
    w = w_ref[...]
    if copy_ref is not None:
        w = w.astype(BF16)
        copy_ref[...] = w
    return w


def _norm_matmul_kernel(x_ref, g_ref, w_ref, *rest, emit):
    o_ref, copy_ref, h_ref = rest if emit else (rest[0], None, rest[1])

    @pl.when(pl.program_id(1) == 0)
    def _():
        def start(rows):
            h_ref[rows, :] = _rms(x_ref[rows, :], g_ref[...]).astype(BF16)

        _for_row_chunks(x_ref.shape[0], start)

    w = _load_weight(w_ref, copy_ref)
    o_ref[...] = jnp.dot(h_ref[...], w, preferred_element_type=F32)


def norm_matmul(x, g, w, *, tm, tn):
    rows, d = x.shape
    n = w.shape[1]
    tm = _row_tile(rows, tm)
    assert n % tn == 0
    w_arg, w_spec, copy_spec, copy_shape = _weight_in(w, (d, tn), lambda i, j: (0, j))
    emit = copy_spec is not None
    assert not emit or rows == tm, "a bf16 copy needs every weight tile visited exactly once"
    out_specs = [pl.BlockSpec((tm, tn), lambda i, j: (i, j))]
    out_shape = [jax.ShapeDtypeStruct((rows, n), F32)]
    if emit:
        out_specs.append(copy_spec)
        out_shape.append(copy_shape)
    out = pl.pallas_call(
        functools.partial(_norm_matmul_kernel, emit=emit),
        grid=(rows // tm, n // tn),
        in_specs=[
            pl.BlockSpec((tm, d), lambda i, j: (i, 0)),
            pl.BlockSpec((1, d), lambda i, j: (0, 0)),
            w_spec,
        ],
        out_specs=out_specs,
        out_shape=out_shape,
        scratch_shapes=[pltpu.VMEM((tm, d), BF16)],
        compiler_params=_params(),
        name="norm_matmul",
    )(x, g.reshape(1, d), w_arg)
    return (out[0], out[1]) if emit else (out[0], w)


def _even_mixer_kernel(*refs, n_heads, dk, dv, d_a):
    n_wide = (n_heads * dv) // d_a
    ha_ref, ba_ref, ca_ref, q_ref, k_ref = refs[:5]
    v_refs = refs[5:5 + n_wide]
    g_refs = refs[5 + n_wide:5 + 2 * n_wide]
    (cos_ref, sin_ref, chist_ref, rstate_ref, wconv_ref, gn_ref,
     mix_ref, cstate_ref, rout_ref, carry_ref, s_ref) = refs[5 + 2 * n_wide:]
    heads_per_ref = d_a // dv

    def head_cols(wide_refs, h):
        off = (h % heads_per_ref) * dv
        return wide_refs[h // heads_per_ref][:, off:off + dv]

    c = pl.program_id(1)
    last = pl.num_programs(1) - 1
    blk = q_ref.shape[0]

    @pl.when(c == 0)
    def _():
        carry_ref[...] = chist_ref[...]
        s_ref[...] = rstate_ref[...]

    u = ca_ref[...] * ha_ref[...]
    row = lax.broadcasted_iota(jnp.int32, u.shape, 0)
    h0 = carry_ref[0:1, :]
    h1 = carry_ref[1:2, :]
    prev1 = jnp.where(row == 0, h1, pltpu.roll(u, 1, axis=0))
    prev2 = jnp.where(row == 0, h0, jnp.where(row == 1, h1, pltpu.roll(u, 2, axis=0)))
    w = wconv_ref[...]
    conv = prev2 * w[0:1] + prev1 * w[1:2] + u * w[2:3]
    mix_ref[:, 0:d_a] = (ba_ref[...] * conv).astype(BF16)
    tail = u[blk - 2:blk, :]
    carry_ref[...] = tail

    @pl.when(c == last)
    def _():
        cstate_ref[...] = tail

    cosb = cos_ref[...]
    sinb = sin_ref[...]
    ii = lax.broadcasted_iota(jnp.int32, (blk, 1), 0).astype(F32)
    jj = lax.broadcasted_iota(jnp.int32, (1, blk), 1).astype(F32)
    rel = ii - jj
    log_gamma = [math.log1p(-(2.0 ** (-5.0 - h))) for h in range(n_heads)]
    stage = []
    for h in range(n_heads):
        lg = log_gamma[h]
        q = q_ref[:, h * dk:(h + 1) * dk]
        k = k_ref[:, h * dk:(h + 1) * dk]
        qr = q * cosb + pltpu.roll(q, dk // 2, axis=1) * sinb
        kr = (k * cosb + pltpu.roll(k, dk // 2, axis=1) * sinb) * (dk ** -0.5)
        vb = head_cols(v_refs, h).astype(BF16)
        sc = lax.dot_general(qr.astype(BF16), kr.astype(BF16), (((1,), (1,)), ((), ())),
                             preferred_element_type=F32)
        st = s_ref[h]
        qdec = jnp.exp((ii + 1.0) * lg)
        inter = jnp.dot((qr * qdec).astype(BF16), st.astype(BF16), preferred_element_type=F32)
        kdec = jnp.exp((blk - 1.0 - ii) * lg)
        kv = lax.dot_general((kr * kdec).astype(BF16), vb, (((0,), (0,)), ((), ())),
                             preferred_element_type=F32)
        s_ref[h] = math.exp(blk * lg) * st + kv
        stage.append((sc, inter, vb))
    for h, (sc, inter, vb) in enumerate(stage):
        dmat = jnp.where(rel >= 0, jnp.exp(jnp.maximum(rel, 0.0) * log_gamma[h]), 0.0)
        o = jnp.dot((sc * dmat).astype(BF16), vb, preferred_element_type=F32) + inter
        on = o * lax.rsqrt(jnp.mean(o * o, axis=-1, keepdims=True) + EPS) * gn_ref[h:h + 1, :]
        gate = head_cols(g_refs, h)
        mix_ref[:, d_a + h * dv:d_a + (h + 1) * dv] = (on * (gate * jax.nn.sigmoid(gate))).astype(BF16)

    @pl.when(c == last)
    def _():
        rout_ref[...] = s_ref[...]


def even_mixer(p, cosb, sinb, conv_hist, ret_state, w_conv, gn, *, blk):
    b, t, _ = p.shape
    n_heads, dk, dv = ret_state.shape[1:]
    d_a = w_conv.shape[1]
    hv = n_heads * dv
    assert hv % d_a == 0 and d_a % dv == 0 and n_heads * dk == d_a
    n_col_blocks = 5 + 2 * (hv // d_a)
    blk = _row_tile(t, blk)
    nc = t // blk
    col_block = lambda j: pl.BlockSpec((None, blk, d_a), lambda bi, ci: (bi, ci, j))
    kern = functools.partial(_even_mixer_kernel, n_heads=n_heads, dk=dk, dv=dv, d_a=d_a)
    return pl.pallas_call(
        kern,
        grid=(b, nc),
        in_specs=[col_block(j) for j in range(n_col_blocks)] + [
            pl.BlockSpec((blk, dk), lambda bi, ci: (ci, 0)),
            pl.BlockSpec((blk, dk), lambda bi, ci: (ci, 0)),
            pl.BlockSpec((None, 2, d_a), lambda bi, ci: (bi, 0, 0)),
            pl.BlockSpec((None, n_heads, dk, dv), lambda bi, ci: (bi, 0, 0, 0)),
            pl.BlockSpec((3, d_a), lambda bi, ci: (0, 0)),
            pl.BlockSpec((n_heads, dv), lambda bi, ci: (0, 0)),
        ],
        out_specs=[
            pl.BlockSpec((None, blk, d_a + hv), lambda bi, ci: (bi, ci, 0)),
            pl.BlockSpec((None, 2, d_a), lambda bi, ci: (bi, 0, 0)),
            pl.BlockSpec((None, n_heads, dk, dv), lambda bi, ci: (bi, 0, 0, 0)),
        ],
        out_shape=[
            jax.ShapeDtypeStruct((b, t, d_a + hv), BF16),
            jax.ShapeDtypeStruct((b, 2, d_a), F32),
            jax.ShapeDtypeStruct((b, n_heads, dk, dv), F32),
        ],
        scratch_shapes=[pltpu.VMEM((2, d_a), F32), pltpu.VMEM((n_heads, dk, dv), F32)],
        compiler_params=_params(),
        name="even_mixer",
    )(*([p] * n_col_blocks), cosb, sinb, conv_hist, ret_state, w_conv, gn)


def _odd_mixer_kernel(sinks_ref, u_ref, q_ref, k_ref, v_ref, cos_ref, sin_ref,
                      phist_ref, ck_ref, cv_ref, wpool_ref, pscale_ref,
                      mix_ref, pstate_ref, kstate_ref, vstate_ref,
                      ext_ref, kbuf_ref, vbuf_ref,
                      *, d_c, n_q_heads, n_kv_heads, hd, pos0, has_cache, n_state_rows):
    c = pl.program_id(1)
    last = pl.num_programs(1) - 1
    blk = u_ref.shape[0]
    hist_rows = POOL_HIST + 1
    d_cg = d_c // len(POOL_WINDOWS)

    @pl.when(c == 0)
    def _():
        ext_ref[0:1, :] = jnp.zeros((1, d_c), F32)
        ext_ref[1:hist_rows, :] = phist_ref[...]
        kbuf_ref[0:WINDOW, :] = ck_ref[...]
        vbuf_ref[0:WINDOW, :] = cv_ref[...]

    u = u_ref[...]
    ext_ref[hist_rows:hist_rows + blk, :] = u
    pos = pos0 + c * blk + lax.broadcasted_iota(jnp.int32, (blk, 1), 0)
    for gi, win in enumerate(POOL_WINDOWS):
        cols = slice(gi * d_cg, (gi + 1) * d_cg)
        s = ext_ref[:, cols]
        span = 1
        while span < win:
            s = s + pltpu.roll(s, span, axis=0)
            span *= 2
        cnt = jnp.minimum(pos + 1, win).astype(F32)
        d = s[hist_rows:, :] / cnt - u[:, cols]
        y = jnp.dot(d.astype(BF16), wpool_ref[gi], preferred_element_type=F32)
        mix_ref[:, cols] = (y * pscale_ref[:, cols]).astype(BF16)
    tail = ext_ref[blk:blk + hist_rows, :]
    ext_ref[0:hist_rows, :] = tail

    @pl.when(c == last)
    def _():
        pstate_ref[...] = ext_ref[1:hist_rows, :]

    lanes = 2 * hd
    cos4 = cos_ref[...]
    sin4 = sin_ref[...]
    lane = lax.broadcasted_iota(jnp.int32, (1, lanes), 1)
    first_half = jnp.bitwise_and(lane, hd - 1) < (hd // 2)
    left = lane < hd

    def rope(x):
        rot = jnp.where(first_half, pltpu.roll(x, lanes - hd // 2, axis=1), pltpu.roll(x, hd // 2, axis=1))
        return x * cos4 + rot * sin4

    kr = rope(k_ref[...])
    vv = v_ref[...]
    kbuf_ref[WINDOW:WINDOW + blk, :] = kr
    vbuf_ref[WINDOW:WINDOW + blk, :] = vv
    kall = kbuf_ref[...]
    vall = vbuf_ref[...]
    kswap = pltpu.roll(kall, hd, axis=1)
    vswap = pltpu.roll(vall, hd, axis=1)
    nk = WINDOW + blk
    chunk_shift = CHUNK.bit_length() - 1
    qi = jnp.right_shift(lax.broadcasted_iota(jnp.int32, (blk, 1), 0), chunk_shift)
    kj = lax.broadcasted_iota(jnp.int32, (1, nk), 1)
    kc = jnp.right_shift(kj, chunk_shift) - WINDOW // CHUNK
    ok = (kc <= qi) & (kc >= qi - WINDOW // CHUNK)
    if not has_cache:
        ok = ok & ((kj >= WINDOW) | (c > 0))
    scale = hd ** -0.5
    group = n_q_heads // n_kv_heads
    zeros = jnp.zeros_like(vall)
    kv_ops = [
        (jnp.where(left, kall, kswap).astype(BF16), jnp.where(left, vall, zeros).astype(BF16),
         jnp.where(left, zeros, vswap).astype(BF16)),
        (jnp.where(left, kswap, kall).astype(BF16), jnp.where(left, vswap, zeros).astype(BF16),
         jnp.where(left, zeros, vall).astype(BF16)),
    ]
    scores = []
    for head in range(n_q_heads):
        kvh = head // group
        if head % 2 == 0:
            qr = rope(q_ref[:, (head // 2) * lanes:(head // 2 + 1) * lanes])
        qh = jnp.where(left if head % 2 == 0 else jnp.logical_not(left), qr, 0.0).astype(BF16)
        scores.append(lax.dot_general(qh, kv_ops[kvh][0], (((1,), (1,)), ((), ())),
                                      preferred_element_type=F32))
    probs = []
    for head, sc in enumerate(scores):
        sc = jnp.where(ok, sc * scale, NEG_INF)
        sink = sinks_ref[head]
        m = jnp.maximum(jnp.max(sc, axis=-1, keepdims=True), sink)
        e = jnp.exp(sc - m)
        den = jnp.sum(e, axis=-1, keepdims=True) + jnp.exp(sink - m)
        probs.append((e / den).astype(BF16))
    for pair in range(n_q_heads // 2):
        kvh = (2 * pair) // group
        assert (2 * pair + 1) // group == kvh
        out = (jnp.dot(probs[2 * pair], kv_ops[kvh][1], preferred_element_type=F32)
               + jnp.dot(probs[2 * pair + 1], kv_ops[kvh][2], preferred_element_type=F32))
        mix_ref[:, d_c + pair * lanes:d_c + (pair + 1) * lanes] = out.astype(BF16)
    ktail = kbuf_ref[blk:blk + WINDOW, :]
    vtail = vbuf_ref[blk:blk + WINDOW, :]
    kbuf_ref[0:WINDOW, :] = ktail
    vbuf_ref[0:WINDOW, :] = vtail

    @pl.when(c == last)
    def _():
        kstate_ref[...] = kbuf_ref[nk - n_state_rows:nk, :]
        vstate_ref[...] = vbuf_ref[nk - n_state_rows:nk, :]


def odd_mixer(p, cos4, sin4, pool_hist, cache_k, cache_v, w_pool, pool_scale, sinks, *,
              blk, pos0, has_cache, n_state_rows):
    b, t, _ = p.shape
    d_c = pool_scale.shape[-1]
    n_q_heads = sinks.shape[-1]
    lanes = cache_k.shape[-1]
    n_kv_heads = 2
    hd = lanes // n_kv_heads
    d_q = n_q_heads * hd
    assert d_q == d_c and d_c % lanes == 0
    blk = _row_tile(t, blk)
    nc = t // blk
    assert blk % CHUNK == 0 or nc == 1
    kern = functools.partial(_odd_mixer_kernel, d_c=d_c, n_q_heads=n_q_heads, n_kv_heads=n_kv_heads,
                             hd=hd, pos0=pos0, has_cache=has_cache, n_state_rows=n_state_rows)
    kcol = (d_c + d_q) // lanes
    return pl.pallas_call(
        kern,
        grid=(b, nc),
        in_specs=[
            pl.BlockSpec(memory_space=pltpu.SMEM),
            pl.BlockSpec((None, blk, d_c), lambda bi, ci: (bi, ci, 0)),
            pl.BlockSpec((None, blk, d_q), lambda bi, ci: (bi, ci, 1)),
            pl.BlockSpec((None, blk, lanes), lambda bi, ci: (bi, ci, kcol)),
            pl.BlockSpec((None, blk, lanes), lambda bi, ci: (bi, ci, kcol + 1)),
            pl.BlockSpec((blk, lanes), lambda bi, ci: (ci, 0)),
            pl.BlockSpec((blk, lanes), lambda bi, ci: (ci, 0)),
            pl.BlockSpec((None, POOL_HIST, d_c), lambda bi, ci: (bi, 0, 0)),
            pl.BlockSpec((None, WINDOW, lanes), lambda bi, ci: (bi, 0, 0)),
            pl.BlockSpec((None, WINDOW, lanes), lambda bi, ci: (bi, 0, 0)),
            pl.BlockSpec(w_pool.shape, lambda bi, ci: (0, 0, 0)),
            pl.BlockSpec((1, d_c), lambda bi, ci: (0, 0)),
        ],
        out_specs=[
            pl.BlockSpec((None, blk, d_c + d_q), lambda bi, ci: (bi, ci, 0)),
            pl.BlockSpec((None, POOL_HIST, d_c), lambda bi, ci: (bi, 0, 0)),
            pl.BlockSpec((None, n_state_rows, lanes), lambda bi, ci: (bi, 0, 0)),
            pl.BlockSpec((None, n_state_rows, lanes), lambda bi, ci: (bi, 0, 0)),
        ],
        out_shape=[
            jax.ShapeDtypeStruct((b, t, d_c + d_q), BF16),
            jax.ShapeDtypeStruct((b, POOL_HIST, d_c), F32),
            jax.ShapeDtypeStruct((b, n_state_rows, lanes), F32),
            jax.ShapeDtypeStruct((b, n_state_rows, lanes), F32),
        ],
        scratch_shapes=[
            pltpu.VMEM((POOL_HIST + 1 + blk, d_c), F32),
            pltpu.VMEM((WINDOW + blk, lanes), F32),
            pltpu.VMEM((WINDOW + blk, lanes), F32),
        ],
        compiler_params=_params(),
        name="odd_mixer",
    )(sinks, p, p, p, p, cos4, sin4, pool_hist, cache_k, cache_v, w_pool, pool_scale.reshape(1, d_c))


def _accumulate(o_ref, part, k):
    @pl.when(k == 0)
    def _():
        o_ref[...] = part

    @pl.when(k > 0)
    def _():
        o_ref[...] += part


def _out_proj_kernel(mix_ref, w_ref, x_ref, g_ref, o_ref, *rest, n_k):
    copy_ref = rest[0] if rest else None
    k = pl.program_id(1)
    part = jnp.dot(mix_ref[...], _load_weight(w_ref, copy_ref), preferred_element_type=F32)
    if n_k == 1:
        o_ref[...] = x_ref[...] + _rms(part, g_ref[...])
    else:
        _accumulate(o_ref, part, k)

        @pl.when(k == n_k - 1)
        def _():
            o_ref[...] = x_ref[...] + _rms(o_ref[...], g_ref[...])


def out_proj(mix, w, x, g, *, tm, tk=None):
    rows, kdim = mix.shape
    d = x.shape[1]
    tm = _row_tile(rows, tm)
    tk = kdim if tk is None else tk
    assert kdim % tk == 0
    w_arg, w_spec, copy_spec, copy_shape = _weight_in(w, (tk, d), lambda i, k: (k, 0))
    emit = copy_spec is not None
    assert not emit or rows == tm, "a bf16 copy needs every weight tile visited exactly once"
    out_specs = [pl.BlockSpec((tm, d), lambda i, k: (i, 0))]
    out_shape = [jax.ShapeDtypeStruct((rows, d), F32)]
    if emit:
        out_specs.append(copy_spec)
        out_shape.append(copy_shape)
    out = pl.pallas_call(
        functools.partial(_out_proj_kernel, n_k=kdim // tk),
        grid=(rows // tm, kdim // tk),
        in_specs=[
            pl.BlockSpec((tm, tk), lambda i, k: (i, k)),
            w_spec,
            pl.BlockSpec((tm, d), lambda i, k: (i, 0)),
            pl.BlockSpec((1, d), lambda i, k: (0, 0)),
        ],
        out_specs=out_specs,
        out_shape=out_shape,
        compiler_params=_params(),
        name="out_proj",
    )(mix, w_arg, x, g.reshape(1, d))
    return (out[0], out[1]) if emit else (out[0], w)


def _ffn_kernel(x_ref, gpre_ref, wa_ref, wg_ref, wconv_ref, wd_ref, gpost_ref, hist_ref,
                o_ref, state_ref, *rest, n_seq, blocks_per_seq, sub, emit):
    if emit:
        wa_copy_ref, wg_copy_ref, wd_copy_ref, h_ref, carry_ref = rest
    else:
        wa_copy_ref = wg_copy_ref = wd_copy_ref = None
        h_ref, carry_ref = rest
    i = pl.program_id(0)
    j = pl.program_id(1)
    last_j = pl.num_programs(1) - 1
    tm = x_ref.shape[0]
    tf = wa_ref.shape[1]
    rows_per_seq = tm // n_seq

    @pl.when(j == 0)
    def _():
        def start(rows):
            x = x_ref[rows, :]
            h_ref[rows, :] = _rms(x, gpre_ref[...]).astype(BF16)
            o_ref[rows, :] = jnp.zeros_like(x)

        _for_row_chunks(tm, start)

    if blocks_per_seq > 1:
        @pl.when(i % blocks_per_seq == 0)
        def _():
            carry_ref[j] = hist_ref[0]

    h = h_ref[...]
    assert rows_per_seq & (rows_per_seq - 1) == 0
    t = jnp.bitwise_and(lax.broadcasted_iota(jnp.int32, (tm, 1), 0), rows_per_seq - 1)
    groups = [slice(s * sub, (s + 1) * sub) for s in range(tf // sub)]
    if emit:
        wa, wg, wd = (_load_weight(wa_ref, wa_copy_ref), _load_weight(wg_ref, wg_copy_ref),
                      _load_weight(wd_ref, wd_copy_ref))
    else:
        wa, wg, wd = wa_ref, wg_ref, wd_ref
    ups = [(jnp.dot(h, wa[:, cols], preferred_element_type=F32),
            jnp.dot(h, wg[:, cols], preferred_element_type=F32)) for cols in groups]
    for cols, (a, val) in zip(groups, ups):
        if blocks_per_seq > 1:
            hist0 = carry_ref[j, 0:1, cols]
            hist1 = carry_ref[j, 1:2, cols]
        else:
            hist = hist_ref[:, :, cols]
            hist0 = jnp.broadcast_to(hist[:, 0:1, :], (n_seq, rows_per_seq, sub)).reshape(tm, sub)
            hist1 = jnp.broadcast_to(hist[:, 1:2, :], (n_seq, rows_per_seq, sub)).reshape(tm, sub)
        prev1 = jnp.where(t == 0, hist1, pltpu.roll(a, 1, axis=0))
        prev2 = jnp.where(t == 0, hist0, jnp.where(t == 1, hist1, pltpu.roll(a, 2, axis=0)))
        w = wconv_ref[:, cols]
        conv = prev2 * w[0:1] + prev1 * w[1:2] + a * w[2:3]
        z = (jax.nn.gelu(conv, approximate=True) * val).astype(BF16)
        o_ref[...] += jnp.dot(z, wd[cols, :], preferred_element_type=F32)
        tail = a.reshape(n_seq, rows_per_seq, sub)[:, rows_per_seq - 2:, :]
        state_ref[j, :, :, cols] = tail
        if blocks_per_seq > 1:
            carry_ref[j, :, cols] = tail[0]

    @pl.when(j == last_j)
    def _():
        def finish(rows):
            o_ref[rows, :] = x_ref[rows, :] + _rms(o_ref[rows, :], gpost_ref[...])

        _for_row_chunks(tm, finish)


def conv_ffn(x, g_pre, w_up, w_conv, w_down, g_post, hist, *, seq_len, tm, tf):
    rows, d = x.shape
    d_ff = w_down.shape[0]
    tm = _row_tile(rows, tm)
    assert d_ff % tf == 0 and tf % MXU_COLS == 0
    n_ff = d_ff // tf
    emit = isinstance(w_up, LayerWeight)
    assert emit == isinstance(w_down, LayerWeight)
    if emit:
        assert rows == tm, "a bf16 copy needs every weight tile visited exactly once"
        half = jax.ShapeDtypeStruct((d, d_ff), BF16)
        up_args = (w_up.stack, w_up.stack)
        up_specs = [w_up.spec((d, tf), lambda i, j: (0, j)), w_up.spec((d, tf), lambda i, j: (0, j + n_ff))]
        copy_specs = [pl.BlockSpec((d, tf), lambda i, j: (0, j)), pl.BlockSpec((d, tf), lambda i, j: (0, j))]
        copy_shapes = [half, half]
    else:
        up_args = tuple(w_up)
        up_specs = [pl.BlockSpec((d, tf), lambda i, j: (0, j)), pl.BlockSpec((d, tf), lambda i, j: (0, j))]
        copy_specs, copy_shapes = [], []
    wd_arg, wd_spec, wd_copy_spec, wd_copy_shape = _weight_in(w_down, (tf, d), lambda i, j: (j, 0))
    if emit:
        copy_specs.append(wd_copy_spec)
        copy_shapes.append(wd_copy_shape)
    if seq_len >= tm:
        assert seq_len % tm == 0
        n_seq, blocks_per_seq = 1, seq_len // tm
        seq_of = lambda i: i // blocks_per_seq
    else:
        assert tm % seq_len == 0
        n_seq, blocks_per_seq = tm // seq_len, 1
        seq_of = lambda i: i
    kern = functools.partial(_ffn_kernel, n_seq=n_seq, blocks_per_seq=blocks_per_seq, sub=MXU_COLS, emit=emit)
    out = pl.pallas_call(
        kern,
        grid=(rows // tm, n_ff),
        in_specs=[
            pl.BlockSpec((tm, d), lambda i, j: (i, 0)),
            pl.BlockSpec((1, d), lambda i, j: (0, 0)),
            up_specs[0],
            up_specs[1],
            pl.BlockSpec((3, tf), lambda i, j: (0, j)),
            wd_spec,
            pl.BlockSpec((1, d), lambda i, j: (0, 0)),
            pl.BlockSpec((n_seq, 2, tf), lambda i, j: (seq_of(i), 0, j)),
        ],
        out_specs=[
            pl.BlockSpec((tm, d), lambda i, j: (i, 0)),
            pl.BlockSpec((n_ff, n_seq, 2, tf), lambda i, j: (0, seq_of(i), 0, 0)),
        ] + copy_specs,
        out_shape=[
            jax.ShapeDtypeStruct((rows, d), F32),
            jax.ShapeDtypeStruct((n_ff, rows // seq_len, 2, tf), F32),
        ] + copy_shapes,
        scratch_shapes=[pltpu.VMEM((tm, d), BF16), pltpu.VMEM((n_ff, 2, tf), F32)],
        compiler_params=_params(),
        name="conv_ffn",
    )(x, g_pre.reshape(1, d), *up_args, w_conv, wd_arg, g_post.reshape(1, d), hist)
    state = jnp.moveaxis(out[1], 0, 2).reshape(rows // seq_len, 2, d_ff)
    weights = ((out[2], out[3]), out[4]) if emit else (tuple(w_up), w_down)
    return out[0], state, weights


def _ple_kernel(x_ref, p_ref, gpre_ref, wgate_ref, wproj_ref, gpost_ref, o_ref, *rest, n_k, emit):
    wgate_copy_ref, wproj_copy_ref = rest[:2] if emit else (None, None)
    h_ref = rest[-1] if n_k > 1 else None
    k = pl.program_id(1)
    tk = wgate_ref.shape[0]

    def finish(pre):
        emb = jnp.dot(p_ref[...].astype(BF16), _load_weight(wproj_ref, wproj_copy_ref),
                      preferred_element_type=F32)
        o_ref[...] = x_ref[...] + _rms(jax.nn.sigmoid(pre) * emb, gpost_ref[...])

    if n_k == 1:
        h = _rms(x_ref[...], gpre_ref[...]).astype(BF16)
        finish(jnp.dot(h, _load_weight(wgate_ref, wgate_copy_ref), preferred_element_type=F32))
    else:
        @pl.when(k == 0)
        def _():
            h = _rms(x_ref[...], gpre_ref[...]).astype(BF16)
            for kk in range(n_k):
                h_ref[kk] = h[:, kk * tk:(kk + 1) * tk]

        _accumulate(o_ref, jnp.dot(h_ref[k], _load_weight(wgate_ref, wgate_copy_ref),
                                   preferred_element_type=F32), k)

        @pl.when(k == n_k - 1)
        def _():
            finish(o_ref[...])


def per_layer_embedding(x, p, g_pre, w_gate, w_proj, g_post, *, tm, tk=None):
    rows, d = x.shape
    pd = p.shape[1]
    tm = _row_tile(rows, tm)
    tk = d if tk is None else tk
    assert d % tk == 0
    n_k = d // tk
    wg_arg, wg_spec, wg_copy_spec, wg_copy_shape = _weight_in(w_gate, (tk, d), lambda i, k: (k, 0))
    wp_arg, wp_spec, wp_copy_spec, wp_copy_shape = _weight_in(w_proj, (pd, d), lambda i, k: (0, 0))
    emit = wg_copy_spec is not None
    assert emit == (wp_copy_spec is not None)
    assert not emit or rows == tm, "a bf16 copy needs every weight tile visited exactly once"
    out_specs = [pl.BlockSpec((tm, d), lambda i, k: (i, 0))]
    out_shape = [jax.ShapeDtypeStruct((rows, d), F32)]
    if emit:
        out_specs += [wg_copy_spec, wp_copy_spec]
        out_shape += [wg_copy_shape, wp_copy_shape]
    out = pl.pallas_call(
        functools.partial(_ple_kernel, n_k=n_k, emit=emit),
        grid=(rows // tm, n_k),
        in_specs=[
            pl.BlockSpec((tm, d), lambda i, k: (i, 0)),
            pl.BlockSpec((tm, pd), lambda i, k: (i, 0)),
            pl.BlockSpec((1, d), lambda i, k: (0, 0)),
            wg_spec,
            wp_spec,
            pl.BlockSpec((1, d), lambda i, k: (0, 0)),
        ],
        out_specs=out_specs,
        out_shape=out_shape,
        scratch_shapes=[pltpu.VMEM((n_k, tm, tk), BF16)] if n_k > 1 else [],
        compiler_params=_params(),
        name="per_layer_embedding",
    )(x, p, g_pre.reshape(1, d), wg_arg, wp_arg, g_post.reshape(1, d))
    return (out[0], (out[1], out[2])) if emit else (out[0], (w_gate, w_proj))


def _rope_tables(pos, dim, reps):
    inv = 1.0 / (ROPE_THETA ** (jnp.arange(0, dim, 2, dtype=F32) / dim))
    ang = pos.astype(F32)[:, None] * inv[None, :]
    cos, sin = jnp.cos(ang), jnp.sin(ang)
    return (jnp.tile(jnp.concatenate([cos, cos], axis=-1), (1, reps)),
            jnp.tile(jnp.concatenate([-sin, sin], axis=-1), (1, reps)))


def _layer(i, x, p_i, grp, states, wts, prm, cfg):
    b, t = grp['b'], grp['t']
    rows = b * t
    j = i // 2
    wq = {}
    if i % 2 == 0:
        proj, wq['w_in'] = norm_matmul(x, prm['norm_mix_pre'][i], wts['w_in'], tm=cfg['tm_in'], tn=cfg['tn_even'])
        mix, c_s, r_s = even_mixer(proj.reshape(b, t, -1), *grp['even_tabs'], states['conv'][j], states['ret'][j],
                                   prm['w_conv_a'][j], prm['ret_gn'][j], blk=cfg['blk_even'])
        new = dict(conv=c_s, ret=r_s)
    else:
        proj, wq['w_in'] = norm_matmul(x, prm['norm_mix_pre'][i], wts['w_in'], tm=cfg['tm_in'], tn=cfg['tn_odd'])
        lanes = grp['kv_lanes']
        has_cache = states['cache_k'] is not None
        if has_cache:
            ck = states['cache_k'][j].reshape(b, WINDOW, lanes)
            cv = states['cache_v'][j].reshape(b, WINDOW, lanes)
            n_state_rows = t
        else:
            ck = cv = jnp.zeros((b, WINDOW, lanes), F32)
            n_state_rows = cfg['kv_rows']
        mix, p_s, k_s, v_s = odd_mixer(proj.reshape(b, t, -1), *grp['odd_tabs'], states['pool'][j], ck, cv,
                                       prm['w_pool'][j], prm['pool_scale'][j], prm['sinks'][j],
                                       blk=cfg['blk_odd'], pos0=grp['pos0'], has_cache=has_cache,
                                       n_state_rows=n_state_rows)
        kv_shape = (b, n_state_rows, 2, lanes // 2)
        new = dict(pool=p_s, k=k_s.reshape(kv_shape), v=v_s.reshape(kv_shape))
    x, wq['w_out'] = out_proj(mix.reshape(rows, -1), wts['w_out'], x, prm['norm_mix_post'][i],
                              tm=cfg['tm_out'], tk=cfg['tk_out'])
    x, new['ffn'], (wq['w_up'], wq['w_down']) = conv_ffn(
        x, prm['norm_ffn_pre'][i], wts['w_up'], prm['w_conv_ffn'][i], wts['w_down'], prm['norm_ffn_post'][i],
        states['ffn'][i], seq_len=t, tm=cfg['tm_ffn'], tf=cfg['tf'])
    x, (wq['w_gate'], wq['w_proj']) = per_layer_embedding(
        x, p_i, prm['norm_ple_pre'][i], wts['w_gate'], wts['w_proj'], prm['norm_ple_post'][i],
        tm=cfg['tm_ple'], tk=cfg['tk_ple'])
    return x, new, wq


def kernel(x_prompt, x_sample, state_conv, state_ret, state_pool, cache_k, cache_v, state_ffn, p_prompt, p_sample, norm_mix_pre, norm_mix_post, norm_ffn_pre, norm_ffn_post, norm_ple_pre, norm_ple_post, w_in_even, w_conv_a, ret_gn, w_out_even, w_in_odd, w_pool, pool_scale, sinks, w_out_odd, w_up, w_conv_ffn, w_down, w_ple_gate, w_ple_proj):
    depth = norm_mix_pre.shape[0]
    n_even, n_odd = w_in_even.shape[0], w_in_odd.shape[0]
    d = x_prompt.shape[-1]
    d_a = w_conv_a.shape[-1]
    n_heads, dk, dv = state_ret.shape[2:]
    d_c = pool_scale.shape[-1]
    kv_rows, n_kv, hd = cache_k.shape[2:]
    d_ff = w_conv_ffn.shape[-1]
    lanes = n_kv * hd
    dt = x_prompt.dtype

    prm = dict(norm_mix_pre=norm_mix_pre, norm_mix_post=norm_mix_post, norm_ffn_pre=norm_ffn_pre,
               norm_ffn_post=norm_ffn_post, norm_ple_pre=norm_ple_pre, norm_ple_post=norm_ple_post,
               w_conv_a=w_conv_a, ret_gn=ret_gn, w_pool=w_pool.astype(BF16), pool_scale=pool_scale, sinks=sinks,
               w_conv_ffn=w_conv_ffn)

    def group(x, pos0):
        b, t = x.shape[:2]
        pos = pos0 + jnp.arange(t, dtype=jnp.int32)
        return dict(b=b, t=t, pos0=pos0, kv_lanes=lanes, even_tabs=_rope_tables(pos, dk, 1),
                    odd_tabs=_rope_tables(pos, hd, lanes // hd))

    bp, sp = x_prompt.shape[:2]
    bs, ts = x_sample.shape[:2]
    grp_p, grp_s = group(x_prompt, 0), group(x_sample, PAST_LEN)
    st_p = dict(conv=jnp.zeros((n_even, bp, 2, d_a), dt), ret=jnp.zeros((n_even, bp, n_heads, dk, dv), dt),
                pool=jnp.zeros((n_odd, bp, POOL_HIST, d_c), dt), cache_k=None, cache_v=None,
                ffn=jnp.zeros((depth, bp, 2, d_ff), dt))
    st_s = dict(conv=state_conv, ret=state_ret, pool=state_pool, cache_k=cache_k, cache_v=cache_v, ffn=state_ffn)
    cfg_s = dict(tm_in=bs * ts, tn_even=1024, tn_odd=w_in_odd.shape[-1] // 2, blk_even=ts, blk_odd=ts,
                 tm_out=bs * ts, tk_out=512, tm_ffn=bs * ts, tf=256, tm_ple=bs * ts, tk_ple=512, kv_rows=kv_rows)
    cfg_p = dict(tm_in=1024, tn_even=1024, tn_odd=w_in_odd.shape[-1] // 2, blk_even=256, blk_odd=128,
                 tm_out=512, tk_out=None, tm_ffn=1024, tf=512, tm_ple=512, tk_ple=None, kv_rows=kv_rows)
    xp = x_prompt.reshape(bp * sp, d)
    xs = x_sample.reshape(bs * ts, d)
    new_p, new_s = [], []
    for i in range(depth):
        j = i // 2
        raw = dict(w_in=LayerWeight(w_in_even if i % 2 == 0 else w_in_odd, j),
                   w_out=LayerWeight(w_out_even if i % 2 == 0 else w_out_odd, j),
                   w_up=LayerWeight(w_up, i), w_down=LayerWeight(w_down, i),
                   w_gate=LayerWeight(w_ple_gate, i), w_proj=LayerWeight(w_ple_proj, i))
        xs, st, wq = _layer(i, xs, p_sample[i].reshape(bs * ts, -1), grp_s, st_s, raw, prm, cfg_s)
        new_s.append(st)
        xp, st, _ = _layer(i, xp, p_prompt[i].reshape(bp * sp, -1), grp_p, st_p, wq, prm, cfg_p)
        new_p.append(st)

    def stacked(new, key):
        return jnp.stack([st[key] for st in new if key in st])

    outs = [xp.reshape(bp, sp, d), xs.reshape(bs, ts, d)]
    for key in ('conv', 'ret', 'pool', 'k', 'v', 'ffn'):
        outs += [stacked(new_p, key), stacked(new_s, key)]
    return tuple(outs)
```

```python
import functools
import math

import jax
import jax.numpy as jnp
from jax import lax
from jax.experimental import pallas as pl
from jax.experimental.pallas import tpu as pltpu

CHUNK = 64
WINDOW = 128
PAST_LEN = 4096
EPS = 1e-6
ROPE_THETA = 10000.0
NEG_INF = -1e30
POOL_WINDOWS = (2, 4, 8, 16)
POOL_HIST = max(POOL_WINDOWS) - 1

V7X_VMEM_BYTES = 64 * 1024 * 1024
VMEM_LIMIT_BYTES = 58 * 1024 * 1024
MXU_COLS = 256

F32 = jnp.float32
BF16 = jnp.bfloat16


def _params():
    return pltpu.CompilerParams(vmem_limit_bytes=VMEM_LIMIT_BYTES)


def _rms(x, g):
    return x * lax.rsqrt(jnp.mean(x * x, axis=-1, keepdims=True) + EPS) * g


ROW_CHUNK = 256


def _for_row_chunks(n_rows, fn):
    if n_rows <= ROW_CHUNK or n_rows % ROW_CHUNK:
        fn(slice(0, n_rows))
        return

    def body(r, carry):
        fn(pl.ds(pl.multiple_of(r * ROW_CHUNK, ROW_CHUNK), ROW_CHUNK))
        return carry

    lax.fori_loop(0, n_rows // ROW_CHUNK, body, 0)


def _resident_spec(block, index_map):
    return pl.BlockSpec(block, index_map, pipeline_mode=pl.Buffered(1))


def _row_tile(rows, want):
    t = min(rows, want)
    assert rows % t == 0, (rows, t)
    return t


class LayerWeight:
    def __init__(self, stack, layer):
        self.stack, self.layer = stack, layer
        self.shape = stack.shape[1:]

    def spec(self, block, index_map):
        layer = self.layer
        return pl.BlockSpec((None,) + block, lambda *g: (layer,) + index_map(*g))


def _weight_in(w, block, index_map):
    if isinstance(w, LayerWeight):
        return (w.stack, w.spec(block, index_map), pl.BlockSpec(block, index_map),
                jax.ShapeDtypeStruct(w.shape, BF16))
    return w, pl.BlockSpec(block, index_map), None, None


def _load_weight(w_ref, copy_ref):
    w = w_ref[...]
    if copy_ref is not None:
        w = w.astype(BF16)
        copy_ref[...] = w
    return w


def _norm_matmul_kernel(x_ref, g_ref, w_ref, *rest, emit):
    o_ref, copy_ref, h_ref = rest if emit else (rest[0], None, rest[1])

    @pl.when(pl.program_id(1) == 0)
    def _():
        def start(rows):
            h_ref[rows, :] = _rms(x_ref[rows, :], g_ref[...]).astype(BF16)

        _for_row_chunks(x_ref.shape[0], start)

    w = _load_weight(w_ref, copy_ref)
    o_ref[...] = jnp.dot(h_ref[...], w, preferred_element_type=F32)


def norm_matmul(x, g, w, *, tm, tn):
    rows, d = x.shape
    n = w.shape[1]
    tm = _row_tile(rows, tm)
    assert n % tn == 0
    w_arg, w_spec, copy_spec, copy_shape = _weight_in(w, (d, tn), lambda i, j: (0, j))
    emit = copy_spec is not None
    assert not emit or rows == tm, "a bf16 copy needs every weight tile visited exactly once"
    out_specs = [pl.BlockSpec((tm, tn), lambda i, j: (i, j))]
    out_shape = [jax.ShapeDtypeStruct((rows, n), F32)]
    if emit:
        out_specs.append(copy_spec)
        out_shape.append(copy_shape)
    out = pl.pallas_call(
        functools.partial(_norm_matmul_kernel, emit=emit),
        grid=(rows // tm, n // tn),
        in_specs=[
            pl.BlockSpec((tm, d), lambda i, j: (i, 0)),
            pl.BlockSpec((1, d), lambda i, j: (0, 0)),
            w_spec,
        ],
        out_specs=out_specs,
        out_shape=out_shape,
        scratch_shapes=[pltpu.VMEM((tm, d), BF16)],
        compiler_params=_params(),
        name="norm_matmul",
    )(x, g.reshape(1, d), w_arg)
    return (out[0], out[1]) if emit else (out[0], w)


def _even_mixer_kernel(*refs, n_heads, dk, dv, d_a):
    n_wide = (n_heads * dv) // d_a
    ha_ref, ba_ref, ca_ref, q_ref, k_ref = refs[:5]
    v_refs = refs[5:5 + n_wide]
    g_refs = refs[5 + n_wide:5 + 2 * n_wide]
    (cos_ref, sin_ref, chist_ref, rstate_ref, wconv_ref, gn_ref,
     mix_ref, cstate_ref, rout_ref, carry_ref, s_ref, dmat_ref, qdec_ref, kdec_ref) = refs[5 + 2 * n_wide:]
    heads_per_ref = d_a // dv

    def head_cols(wide_refs, h):
        off = (h % heads_per_ref) * dv
        return wide_refs[h // heads_per_ref][:, off:off + dv]

    c = pl.program_id(1)
    last = pl.num_programs(1) - 1
    blk = q_ref.shape[0]

    @pl.when(c == 0)
    def _():
        carry_ref[...] = chist_ref[...]
        s_ref[...] = rstate_ref[...]

    u = ca_ref[...] * ha_ref[...]
    row = lax.broadcasted_iota(jnp.int32, u.shape, 0)
    h0 = carry_ref[0:1, :]
    h1 = carry_ref[1:2, :]
    prev1 = jnp.where(row == 0, h1, pltpu.roll(u, 1, axis=0))
    prev2 = jnp.where(row == 0, h0, jnp.where(row == 1, h1, pltpu.roll(u, 2, axis=0)))
    w = wconv_ref[...]
    conv = prev2 * w[0:1] + prev1 * w[1:2] + u * w[2:3]
    mix_ref[:, 0:d_a] = (ba_ref[...] * conv).astype(BF16)
    tail = u[blk - 2:blk, :]
    carry_ref[...] = tail

    @pl.when(c == last)
    def _():
        cstate_ref[...] = tail

    cosb = cos_ref[...]
    sinb = sin_ref[...]
    log_gamma = [math.log1p(-(2.0 ** (-5.0 - h))) for h in range(n_heads)]

    @pl.when((pl.program_id(0) == 0) & (c == 0))
    def _():
        rel = (lax.broadcasted_iota(jnp.int32, (blk, blk), 0)
               - lax.broadcasted_iota(jnp.int32, (blk, blk), 1)).astype(F32)
        ii = lax.broadcasted_iota(jnp.int32, (blk, dk), 0).astype(F32)
        for h, lg in enumerate(log_gamma):
            dmat_ref[h] = jnp.where(rel >= 0, jnp.exp(jnp.maximum(rel, 0.0) * lg), 0.0)
            qdec_ref[h] = jnp.exp((ii + 1.0) * lg)
            kdec_ref[h] = jnp.exp((blk - 1.0 - ii) * lg)

    stage = []
    for h in range(n_heads):
        q = q_ref[:, h * dk:(h + 1) * dk]
        k = k_ref[:, h * dk:(h + 1) * dk]
        qr = q * cosb + pltpu.roll(q, dk // 2, axis=1) * sinb
        kr = (k * cosb + pltpu.roll(k, dk // 2, axis=1) * sinb) * (dk ** -0.5)
        vb = head_cols(v_refs, h).astype(BF16)
        sc = lax.dot_general(qr.astype(BF16), kr.astype(BF16), (((1,), (1,)), ((), ())),
                             preferred_element_type=F32)
        st = s_ref[h]
        inter = jnp.dot((qr * qdec_ref[h]).astype(BF16), st.astype(BF16), preferred_element_type=F32)
        kv = lax.dot_general((kr * kdec_ref[h]).astype(BF16), vb, (((0,), (0,)), ((), ())),
                             preferred_element_type=F32)
        s_ref[h] = math.exp(blk * log_gamma[h]) * st + kv
        stage.append((sc, inter, vb))
    for h, (sc, inter, vb) in enumerate(stage):
        o = jnp.dot((sc * dmat_ref[h]).astype(BF16), vb, preferred_element_type=F32) + inter
        on = o * lax.rsqrt(jnp.mean(o * o, axis=-1, keepdims=True) + EPS) * gn_ref[h:h + 1, :]
        gate = head_cols(g_refs, h)
        mix_ref[:, d_a + h * dv:d_a + (h + 1) * dv] = (on * (gate * jax.nn.sigmoid(gate))).astype(BF16)

    @pl.when(c == last)
    def _():
        rout_ref[...] = s_ref[...]


def even_mixer(p, cosb, sinb, conv_hist, ret_state, w_conv, gn, *, blk):
    b, t, _ = p.shape
    n_heads, dk, dv = ret_state.shape[1:]
    d_a = w_conv.shape[1]
    hv = n_heads * dv
    assert hv % d_a == 0 and d_a % dv == 0 and n_heads * dk == d_a
    n_col_blocks = 5 + 2 * (hv // d_a)
    blk = _row_tile(t, blk)
    nc = t // blk
    col_block = lambda j: pl.BlockSpec((None, blk, d_a), lambda bi, ci: (bi, ci, j))
    kern = functools.partial(_even_mixer_kernel, n_heads=n_heads, dk=dk, dv=dv, d_a=d_a)
    return pl.pallas_call(
        kern,
        grid=(b, nc),
        in_specs=[col_block(j) for j in range(n_col_blocks)] + [
            pl.BlockSpec((blk, dk), lambda bi, ci: (ci, 0)),
            pl.BlockSpec((blk, dk), lambda bi, ci: (ci, 0)),
            pl.BlockSpec((None, 2, d_a), lambda bi, ci: (bi, 0, 0)),
            pl.BlockSpec((None, n_heads, dk, dv), lambda bi, ci: (bi, 0, 0, 0)),
            pl.BlockSpec((3, d_a), lambda bi, ci: (0, 0)),
            pl.BlockSpec((n_heads, dv), lambda bi, ci: (0, 0)),
        ],
        out_specs=[
            pl.BlockSpec((None, blk, d_a + hv), lambda bi, ci: (bi, ci, 0)),
            pl.BlockSpec((None, 2, d_a), lambda bi, ci: (bi, 0, 0)),
            pl.BlockSpec((None, n_heads, dk, dv), lambda bi, ci: (bi, 0, 0, 0)),
        ],
        out_shape=[
            jax.ShapeDtypeStruct((b, t, d_a + hv), BF16),
            jax.ShapeDtypeStruct((b, 2, d_a), F32),
            jax.ShapeDtypeStruct((b, n_heads, dk, dv), F32),
        ],
        scratch_shapes=[pltpu.VMEM((2, d_a), F32), pltpu.VMEM((n_heads, dk, dv), F32),
                        pltpu.VMEM((n_heads, blk, blk), F32), pltpu.VMEM((n_heads, blk, dk), F32),
                        pltpu.VMEM((n_heads, blk, dk), F32)],
        compiler_params=_params(),
        name="even_mixer",
    )(*([p] * n_col_blocks), cosb, sinb, conv_hist, ret_state, w_conv, gn)


def _odd_mixer_kernel(sinks_ref, u_ref, q_ref, k_ref, v_ref, cos_ref, sin_ref,
                      phist_ref, ck_ref, cv_ref, wpool_ref, pscale_ref,
                      mix_ref, pstate_ref, kstate_ref, vstate_ref,
                      ext_ref, kbuf_ref, vbuf_ref,
                      *, d_c, n_q_heads, n_kv_heads, hd, pos0, has_cache, n_state_rows):
    c = pl.program_id(1)
    last = pl.num_programs(1) - 1
    blk = u_ref.shape[0]
    hist_rows = POOL_HIST + 1
    d_cg = d_c // len(POOL_WINDOWS)

    @pl.when(c == 0)
    def _():
        ext_ref[0:1, :] = jnp.zeros((1, d_c), F32)
        ext_ref[1:hist_rows, :] = phist_ref[...]
        kbuf_ref[0:WINDOW, :] = ck_ref[...]
        vbuf_ref[0:WINDOW, :] = cv_ref[...]

    u = u_ref[...]
    ext_ref[hist_rows:hist_rows + blk, :] = u
    pos = pos0 + c * blk + lax.broadcasted_iota(jnp.int32, (blk, 1), 0)
    for gi, win in enumerate(POOL_WINDOWS):
        cols = slice(gi * d_cg, (gi + 1) * d_cg)
        s = ext_ref[:, cols]
        span = 1
        while span < win:
            s = s + pltpu.roll(s, span, axis=0)
            span *= 2
        cnt = jnp.minimum(pos + 1, win).astype(F32)
        d = s[hist_rows:, :] / cnt - u[:, cols]
        y = jnp.dot(d.astype(BF16), wpool_ref[gi], preferred_element_type=F32)
        mix_ref[:, cols] = (y * pscale_ref[:, cols]).astype(BF16)
    tail = ext_ref[blk:blk + hist_rows, :]
    ext_ref[0:hist_rows, :] = tail

    @pl.when(c == last)
    def _():
        pstate_ref[...] = ext_ref[1:hist_rows, :]

    lanes = 2 * hd
    cos4 = cos_ref[...]
    sin4 = sin_ref[...]
    lane = lax.broadcasted_iota(jnp.int32, (1, lanes), 1)
    first_half = jnp.bitwise_and(lane, hd - 1) < (hd // 2)
    left = lane < hd

    def rope(x):
        rot = jnp.where(first_half, pltpu.roll(x, lanes - hd // 2, axis=1), pltpu.roll(x, hd // 2, axis=1))
        return x * cos4 + rot * sin4

    kr = rope(k_ref[...])
    vv = v_ref[...]
    kbuf_ref[WINDOW:WINDOW + blk, :] = kr
    vbuf_ref[WINDOW:WINDOW + blk, :] = vv
    kall = kbuf_ref[...]
    vall = vbuf_ref[...]
    kswap = pltpu.roll(kall, hd, axis=1)
    vswap = pltpu.roll(vall, hd, axis=1)
    nk = WINDOW + blk
    chunk_shift = CHUNK.bit_length() - 1
    qi = jnp.right_shift(lax.broadcasted_iota(jnp.int32, (blk, 1), 0), chunk_shift)
    kj = lax.broadcasted_iota(jnp.int32, (1, nk), 1)
    kc = jnp.right_shift(kj, chunk_shift) - WINDOW // CHUNK
    ok = (kc <= qi) & (kc >= qi - WINDOW // CHUNK)
    if not has_cache:
        ok = ok & ((kj >= WINDOW) | (c > 0))
    scale = hd ** -0.5
    group = n_q_heads // n_kv_heads
    zeros = jnp.zeros_like(vall)
    kv_ops = [
        (jnp.where(left, kall, kswap).astype(BF16), jnp.where(left, vall, zeros).astype(BF16),
         jnp.where(left, zeros, vswap).astype(BF16)),
        (jnp.where(left, kswap, kall).astype(BF16), jnp.where(left, vswap, zeros).astype(BF16),
         jnp.where(left, zeros, vall).astype(BF16)),
    ]
    scores = []
    for head in range(n_q_heads):
        kvh = head // group
        if head % 2 == 0:
            qr = rope(q_ref[:, (head // 2) * lanes:(head // 2 + 1) * lanes])
        qh = jnp.where(left if head % 2 == 0 else jnp.logical_not(left), qr, 0.0).astype(BF16)
        scores.append(lax.dot_general(qh, kv_ops[kvh][0], (((1,), (1,)), ((), ())),
                                      preferred_element_type=F32))
    probs = []
    for head, sc in enumerate(scores):
        sc = jnp.where(ok, sc * scale, NEG_INF)
        sink = sinks_ref[head]
        m = jnp.maximum(jnp.max(sc, axis=-1, keepdims=True), sink)
        e = jnp.exp(sc - m)
        den = jnp.sum(e, axis=-1, keepdims=True) + jnp.exp(sink - m)
        probs.append((e / den).astype(BF16))
    for pair in range(n_q_heads // 2):
        kvh = (2 * pair) // group
        assert (2 * pair + 1) // group == kvh
        out = (jnp.dot(probs[2 * pair], kv_ops[kvh][1], preferred_element_type=F32)
               + jnp.dot(probs[2 * pair + 1], kv_ops[kvh][2], preferred_element_type=F32))
        mix_ref[:, d_c + pair * lanes:d_c + (pair + 1) * lanes] = out.astype(BF16)
    ktail = kbuf_ref[blk:blk + WINDOW, :]
    vtail = vbuf_ref[blk:blk + WINDOW, :]
    kbuf_ref[0:WINDOW, :] = ktail
    vbuf_ref[0:WINDOW, :] = vtail

    @pl.when(c == last)
    def _():
        kstate_ref[...] = kbuf_ref[nk - n_state_rows:nk, :]
        vstate_ref[...] = vbuf_ref[nk - n_state_rows:nk, :]


def odd_mixer(p, cos4, sin4, pool_hist, cache_k, cache_v, w_pool, pool_scale, sinks, *,
              blk, pos0, has_cache, n_state_rows):
    b, t, _ = p.shape
    d_c = pool_scale.shape[-1]
    n_q_heads = sinks.shape[-1]
    lanes = cache_k.shape[-1]
    n_kv_heads = 2
    hd = lanes // n_kv_heads
    d_q = n_q_heads * hd
    assert d_q == d_c and d_c % lanes == 0
    blk = _row_tile(t, blk)
    nc = t // blk
    assert blk % CHUNK == 0 or nc == 1
    kern = functools.partial(_odd_mixer_kernel, d_c=d_c, n_q_heads=n_q_heads, n_kv_heads=n_kv_heads,
                             hd=hd, pos0=pos0, has_cache=has_cache, n_state_rows=n_state_rows)
    kcol = (d_c + d_q) // lanes
    return pl.pallas_call(
        kern,
        grid=(b, nc),
        in_specs=[
            pl.BlockSpec(memory_space=pltpu.SMEM),
            pl.BlockSpec((None, blk, d_c), lambda bi, ci: (bi, ci, 0)),
            pl.BlockSpec((None, blk, d_q), lambda bi, ci: (bi, ci, 1)),
            pl.BlockSpec((None, blk, lanes), lambda bi, ci: (bi, ci, kcol)),
            pl.BlockSpec((None, blk, lanes), lambda bi, ci: (bi, ci, kcol + 1)),
            pl.BlockSpec((blk, lanes), lambda bi, ci: (ci, 0)),
            pl.BlockSpec((blk, lanes), lambda bi, ci: (ci, 0)),
            pl.BlockSpec((None, POOL_HIST, d_c), lambda bi, ci: (bi, 0, 0)),
            pl.BlockSpec((None, WINDOW, lanes), lambda bi, ci: (bi, 0, 0)),
            pl.BlockSpec((None, WINDOW, lanes), lambda bi, ci: (bi, 0, 0)),
            pl.BlockSpec(w_pool.shape, lambda bi, ci: (0, 0, 0)),
            pl.BlockSpec((1, d_c), lambda bi, ci: (0, 0)),
        ],
        out_specs=[
            pl.BlockSpec((None, blk, d_c + d_q), lambda bi, ci: (bi, ci, 0)),
            pl.BlockSpec((None, POOL_HIST, d_c), lambda bi, ci: (bi, 0, 0)),
            pl.BlockSpec((None, n_state_rows, lanes), lambda bi, ci: (bi, 0, 0)),
            pl.BlockSpec((None, n_state_rows, lanes), lambda bi, ci: (bi, 0, 0)),
        ],
        out_shape=[
            jax.ShapeDtypeStruct((b, t, d_c + d_q), BF16),
            jax.ShapeDtypeStruct((b, POOL_HIST, d_c), F32),
            jax.ShapeDtypeStruct((b, n_state_rows, lanes), F32),
            jax.ShapeDtypeStruct((b, n_state_rows, lanes), F32),
        ],
        scratch_shapes=[
            pltpu.VMEM((POOL_HIST + 1 + blk, d_c), F32),
            pltpu.VMEM((WINDOW + blk, lanes), F32),
            pltpu.VMEM((WINDOW + blk, lanes), F32),
        ],
        compiler_params=_params(),
        name="odd_mixer",
    )(sinks, p, p, p, p, cos4, sin4, pool_hist, cache_k, cache_v, w_pool, pool_scale.reshape(1, d_c))


def _accumulate(o_ref, part, k):
    @pl.when(k == 0)
    def _():
        o_ref[...] = part

    @pl.when(k > 0)
    def _():
        o_ref[...] += part


def _out_proj_kernel(mix_ref, w_ref, x_ref, g_ref, o_ref, *rest, n_k):
    copy_ref = rest[0] if rest else None
    k = pl.program_id(1)
    part = jnp.dot(mix_ref[...], _load_weight(w_ref, copy_ref), preferred_element_type=F32)
    if n_k == 1:
        o_ref[...] = x_ref[...] + _rms(part, g_ref[...])
    else:
        _accumulate(o_ref, part, k)

        @pl.when(k == n_k - 1)
        def _():
            o_ref[...] = x_ref[...] + _rms(o_ref[...], g_ref[...])


def out_proj(mix, w, x, g, *, tm, tk=None):
    rows, kdim = mix.shape
    d = x.shape[1]
    tm = _row_tile(rows, tm)
    tk = kdim if tk is None else tk
    assert kdim % tk == 0
    w_arg, w_spec, copy_spec, copy_shape = _weight_in(w, (tk, d), lambda i, k: (k, 0))
    emit = copy_spec is not None
    assert not emit or rows == tm, "a bf16 copy needs every weight tile visited exactly once"
    if not emit and tk == kdim:
        w_spec = _resident_spec((tk, d), lambda i, k: (0, 0))
    out_specs = [pl.BlockSpec((tm, d), lambda i, k: (i, 0))]
    out_shape = [jax.ShapeDtypeStruct((rows, d), F32)]
    if emit:
        out_specs.append(copy_spec)
        out_shape.append(copy_shape)
    out = pl.pallas_call(
        functools.partial(_out_proj_kernel, n_k=kdim // tk),
        grid=(rows // tm, kdim // tk),
        in_specs=[
            pl.BlockSpec((tm, tk), lambda i, k: (i, k)),
            w_spec,
            pl.BlockSpec((tm, d), lambda i, k: (i, 0)),
            pl.BlockSpec((1, d), lambda i, k: (0, 0)),
        ],
        out_specs=out_specs,
        out_shape=out_shape,
        compiler_params=_params(),
        name="out_proj",
    )(mix, w_arg, x, g.reshape(1, d))
    return (out[0], out[1]) if emit else (out[0], w)


def _ffn_kernel(x_ref, gpre_ref, wa_ref, wg_ref, wconv_ref, wd_ref, gpost_ref, hist_ref,
                o_ref, state_ref, *rest, n_seq, blocks_per_seq, sub, emit):
    if emit:
        wa_copy_ref, wg_copy_ref, wd_copy_ref, h_ref, carry_ref = rest
    else:
        wa_copy_ref = wg_copy_ref = wd_copy_ref = None
        h_ref, carry_ref = rest
    i = pl.program_id(0)
    j = pl.program_id(1)
    last_j = pl.num_programs(1) - 1
    tm = x_ref.shape[0]
    tf = wa_ref.shape[1]
    rows_per_seq = tm // n_seq

    @pl.when(j == 0)
    def _():
        def start(rows):
            x = x_ref[rows, :]
            h_ref[rows, :] = _rms(x, gpre_ref[...]).astype(BF16)
            o_ref[rows, :] = jnp.zeros_like(x)

        _for_row_chunks(tm, start)

    if blocks_per_seq > 1:
        @pl.when(i % blocks_per_seq == 0)
        def _():
            carry_ref[j] = hist_ref[0]

    h = h_ref[...]
    assert rows_per_seq & (rows_per_seq - 1) == 0
    t = jnp.bitwise_and(lax.broadcasted_iota(jnp.int32, (tm, 1), 0), rows_per_seq - 1)
    groups = [slice(s * sub, (s + 1) * sub) for s in range(tf // sub)]
    if emit:
        wa, wg, wd = (_load_weight(wa_ref, wa_copy_ref), _load_weight(wg_ref, wg_copy_ref),
                      _load_weight(wd_ref, wd_copy_ref))
    else:
        wa, wg, wd = wa_ref, wg_ref, wd_ref
    ups = [(jnp.dot(h, wa[:, cols], preferred_element_type=F32),
            jnp.dot(h, wg[:, cols], preferred_element_type=F32)) for cols in groups]
    for cols, (a, val) in zip(groups, ups):
        if blocks_per_seq > 1:
            hist0 = carry_ref[j, 0:1, cols]
            hist1 = carry_ref[j, 1:2, cols]
        else:
            hist = hist_ref[:, :, cols]
            hist0 = jnp.broadcast_to(hist[:, 0:1, :], (n_seq, rows_per_seq, sub)).reshape(tm, sub)
            hist1 = jnp.broadcast_to(hist[:, 1:2, :], (n_seq, rows_per_seq, sub)).reshape(tm, sub)
        prev1 = jnp.where(t == 0, hist1, pltpu.roll(a, 1, axis=0))
        prev2 = jnp.where(t == 0, hist0, jnp.where(t == 1, hist1, pltpu.roll(a, 2, axis=0)))
        w = wconv_ref[:, cols]
        conv = prev2 * w[0:1] + prev1 * w[1:2] + a * w[2:3]
        z = (jax.nn.gelu(conv, approximate=True) * val).astype(BF16)
        o_ref[...] += jnp.dot(z, wd[cols, :], preferred_element_type=F32)
        tail = a.reshape(n_seq, rows_per_seq, sub)[:, rows_per_seq - 2:, :]
        state_ref[j, :, :, cols] = tail
        if blocks_per_seq > 1:
            carry_ref[j, :, cols] = tail[0]

    @pl.when(j == last_j)
    def _():
        def finish(rows):
            o_ref[rows, :] = x_ref[rows, :] + _rms(o_ref[rows, :], gpost_ref[...])

        _for_row_chunks(tm, finish)


def conv_ffn(x, g_pre, w_up, w_conv, w_down, g_post, hist, *, seq_len, tm, tf):
    rows, d = x.shape
    d_ff = w_down.shape[0]
    tm = _row_tile(rows, tm)
    assert d_ff % tf == 0 and tf % MXU_COLS == 0
    n_ff = d_ff // tf
    emit = isinstance(w_up, LayerWeight)
    assert emit == isinstance(w_down, LayerWeight)
    if emit:
        assert rows == tm, "a bf16 copy needs every weight tile visited exactly once"
        half = jax.ShapeDtypeStruct((d, d_ff), BF16)
        up_args = (w_up.stack, w_up.stack)
        up_specs = [w_up.spec((d, tf), lambda i, j: (0, j)), w_up.spec((d, tf), lambda i, j: (0, j + n_ff))]
        copy_specs = [pl.BlockSpec((d, tf), lambda i, j: (0, j)), pl.BlockSpec((d, tf), lambda i, j: (0, j))]
        copy_shapes = [half, half]
    else:
        up_args = tuple(w_up)
        up_specs = [pl.BlockSpec((d, tf), lambda i, j: (0, j)), pl.BlockSpec((d, tf), lambda i, j: (0, j))]
        copy_specs, copy_shapes = [], []
    wd_arg, wd_spec, wd_copy_spec, wd_copy_shape = _weight_in(w_down, (tf, d), lambda i, j: (j, 0))
    if emit:
        copy_specs.append(wd_copy_spec)
        copy_shapes.append(wd_copy_shape)
    if seq_len >= tm:
        assert seq_len % tm == 0
        n_seq, blocks_per_seq = 1, seq_len // tm
        seq_of = lambda i: i // blocks_per_seq
    else:
        assert tm % seq_len == 0
        n_seq, blocks_per_seq = tm // seq_len, 1
        seq_of = lambda i: i
    kern = functools.partial(_ffn_kernel, n_seq=n_seq, blocks_per_seq=blocks_per_seq, sub=MXU_COLS, emit=emit)
    out = pl.pallas_call(
        kern,
        grid=(rows // tm, n_ff),
        in_specs=[
            (_resident_spec if rows == tm else pl.BlockSpec)((tm, d), lambda i, j: (i, 0)),
            pl.BlockSpec((1, d), lambda i, j: (0, 0)),
            up_specs[0],
            up_specs[1],
            pl.BlockSpec((3, tf), lambda i, j: (0, j)),
            wd_spec,
            pl.BlockSpec((1, d), lambda i, j: (0, 0)),
            pl.BlockSpec((n_seq, 2, tf), lambda i, j: (seq_of(i), 0, j)),
        ],
        out_specs=[
            pl.BlockSpec((tm, d), lambda i, j: (i, 0)),
            pl.BlockSpec((n_ff, n_seq, 2, tf), lambda i, j: (0, seq_of(i), 0, 0)),
        ] + copy_specs,
        out_shape=[
            jax.ShapeDtypeStruct((rows, d), F32),
            jax.ShapeDtypeStruct((n_ff, rows // seq_len, 2, tf), F32),
        ] + copy_shapes,
        scratch_shapes=[pltpu.VMEM((tm, d), BF16), pltpu.VMEM((n_ff, 2, tf), F32)],
        compiler_params=_params(),
        name="conv_ffn",
    )(x, g_pre.reshape(1, d), *up_args, w_conv, wd_arg, g_post.reshape(1, d), hist)
    state = jnp.moveaxis(out[1], 0, 2).reshape(rows // seq_len, 2, d_ff)
    weights = ((out[2], out[3]), out[4]) if emit else (tuple(w_up), w_down)
    return out[0], state, weights


def _ple_kernel(x_ref, p_ref, gpre_ref, wgate_ref, wproj_ref, gpost_ref, o_ref, *rest, n_k, emit):
    wgate_copy_ref, wproj_copy_ref = rest[:2] if emit else (None, None)
    h_ref = rest[-1] if n_k > 1 else None
    k = pl.program_id(1)
    tk = wgate_ref.shape[0]

    def finish(pre):
        emb = jnp.dot(p_ref[...].astype(BF16), _load_weight(wproj_ref, wproj_copy_ref),
                      preferred_element_type=F32)
        o_ref[...] = x_ref[...] + _rms(jax.nn.sigmoid(pre) * emb, gpost_ref[...])

    if n_k == 1:
        h = _rms(x_ref[...], gpre_ref[...]).astype(BF16)
        finish(jnp.dot(h, _load_weight(wgate_ref, wgate_copy_ref), preferred_element_type=F32))
    else:
        @pl.when(k == 0)
        def _():
            h = _rms(x_ref[...], gpre_ref[...]).astype(BF16)
            for kk in range(n_k):
                h_ref[kk] = h[:, kk * tk:(kk + 1) * tk]

        _accumulate(o_ref, jnp.dot(h_ref[k], _load_weight(wgate_ref, wgate_copy_ref),
                                   preferred_element_type=F32), k)

        @pl.when(k == n_k - 1)
        def _():
            finish(o_ref[...])


def per_layer_embedding(x, p, g_pre, w_gate, w_proj, g_post, *, tm, tk=None):
    rows, d = x.shape
    pd = p.shape[1]
    tm = _row_tile(rows, tm)
    tk = d if tk is None else tk
    assert d % tk == 0
    n_k = d // tk
    wg_arg, wg_spec, wg_copy_spec, wg_copy_shape = _weight_in(w_gate, (tk, d), lambda i, k: (k, 0))
    wp_arg, wp_spec, wp_copy_spec, wp_copy_shape = _weight_in(w_proj, (pd, d), lambda i, k: (0, 0))
    emit = wg_copy_spec is not None
    assert emit == (wp_copy_spec is not None)
    assert not emit or rows == tm, "a bf16 copy needs every weight tile visited exactly once"
    if not emit:
        wp_spec = _resident_spec((pd, d), lambda i, k: (0, 0))
        if n_k == 1:
            wg_spec = _resident_spec((tk, d), lambda i, k: (0, 0))
    out_specs = [pl.BlockSpec((tm, d), lambda i, k: (i, 0))]
    out_shape = [jax.ShapeDtypeStruct((rows, d), F32)]
    if emit:
        out_specs += [wg_copy_spec, wp_copy_spec]
        out_shape += [wg_copy_shape, wp_copy_shape]
    out = pl.pallas_call(
        functools.partial(_ple_kernel, n_k=n_k, emit=emit),
        grid=(rows // tm, n_k),
        in_specs=[
            pl.BlockSpec((tm, d), lambda i, k: (i, 0)),
            pl.BlockSpec((tm, pd), lambda i, k: (i, 0)),
            pl.BlockSpec((1, d), lambda i, k: (0, 0)),
            wg_spec,
            wp_spec,
            pl.BlockSpec((1, d), lambda i, k: (0, 0)),
        ],
        out_specs=out_specs,
        out_shape=out_shape,
        scratch_shapes=[pltpu.VMEM((n_k, tm, tk), BF16)] if n_k > 1 else [],
        compiler_params=_params(),
        name="per_layer_embedding",
    )(x, p, g_pre.reshape(1, d), wg_arg, wp_arg, g_post.reshape(1, d))
    return (out[0], (out[1], out[2])) if emit else (out[0], (w_gate, w_proj))


def _rope_tables(pos, dim, reps):
    inv = 1.0 / (ROPE_THETA ** (jnp.arange(0, dim, 2, dtype=F32) / dim))
    ang = pos.astype(F32)[:, None] * inv[None, :]
    cos, sin = jnp.cos(ang), jnp.sin(ang)
    return (jnp.tile(jnp.concatenate([cos, cos], axis=-1), (1, reps)),
            jnp.tile(jnp.concatenate([-sin, sin], axis=-1), (1, reps)))


def _layer(i, x, p_i, grp, states, wts, prm, cfg):
    b, t = grp['b'], grp['t']
    rows = b * t
    j = i // 2
    wq = {}
    if i % 2 == 0:
        proj, wq['w_in'] = norm_matmul(x, prm['norm_mix_pre'][i], wts['w_in'], tm=cfg['tm_in'], tn=cfg['tn_even'])
        mix, c_s, r_s = even_mixer(proj.reshape(b, t, -1), *grp['even_tabs'], states['conv'][j], states['ret'][j],
                                   prm['w_conv_a'][j], prm['ret_gn'][j], blk=cfg['blk_even'])
        new = dict(conv=c_s, ret=r_s)
    else:
        proj, wq['w_in'] = norm_matmul(x, prm['norm_mix_pre'][i], wts['w_in'], tm=cfg['tm_in'], tn=cfg['tn_odd'])
        lanes = grp['kv_lanes']
        has_cache = states['cache_k'] is not None
        if has_cache:
            ck = states['cache_k'][j].reshape(b, WINDOW, lanes)
            cv = states['cache_v'][j].reshape(b, WINDOW, lanes)
            n_state_rows = t
        else:
            ck = cv = jnp.zeros((b, WINDOW, lanes), F32)
            n_state_rows = cfg['kv_rows']
        mix, p_s, k_s, v_s = odd_mixer(proj.reshape(b, t, -1), *grp['odd_tabs'], states['pool'][j], ck, cv,
                                       prm['w_pool'][j], prm['pool_scale'][j], prm['sinks'][j],
                                       blk=cfg['blk_odd'], pos0=grp['pos0'], has_cache=has_cache,
                                       n_state_rows=n_state_rows)
        kv_shape = (b, n_state_rows, 2, lanes // 2)
        new = dict(pool=p_s, k=k_s.reshape(kv_shape), v=v_s.reshape(kv_shape))
    x, wq['w_out'] = out_proj(mix.reshape(rows, -1), wts['w_out'], x, prm['norm_mix_post'][i],
                              tm=cfg['tm_out'], tk=cfg['tk_out'])
    x, new['ffn'], (wq['w_up'], wq['w_down']) = conv_ffn(
        x, prm['norm_ffn_pre'][i], wts['w_up'], prm['w_conv_ffn'][i], wts['w_down'], prm['norm_ffn_post'][i],
        states['ffn'][i], seq_len=t, tm=cfg['tm_ffn'], tf=cfg['tf'])
    x, (wq['w_gate'], wq['w_proj']) = per_layer_embedding(
        x, p_i, prm['norm_ple_pre'][i], wts['w_gate'], wts['w_proj'], prm['norm_ple_post'][i],
        tm=cfg['tm_ple'], tk=cfg['tk_ple'])
    return x, new, wq


def kernel(x_prompt, x_sample, state_conv, state_ret, state_pool, cache_k, cache_v, state_ffn, p_prompt, p_sample, norm_mix_pre, norm_mix_post, norm_ffn_pre, norm_ffn_post, norm_ple_pre, norm_ple_post, w_in_even, w_conv_a, ret_gn, w_out_even, w_in_odd, w_pool, pool_scale, sinks, w_out_odd, w_up, w_conv_ffn, w_down, w_ple_gate, w_ple_proj):
    depth = norm_mix_pre.shape[0]
    n_even, n_odd = w_in_even.shape[0], w_in_odd.shape[0]
    d = x_prompt.shape[-1]
    d_a = w_conv_a.shape[-1]
    n_heads, dk, dv = state_ret.shape[2:]
    d_c = pool_scale.shape[-1]
    kv_rows, n_kv, hd = cache_k.shape[2:]
    d_ff = w_conv_ffn.shape[-1]
    lanes = n_kv * hd
    dt = x_prompt.dtype

    prm = dict(norm_mix_pre=norm_mix_pre, norm_mix_post=norm_mix_post, norm_ffn_pre=norm_ffn_pre,
               norm_ffn_post=norm_ffn_post, norm_ple_pre=norm_ple_pre, norm_ple_post=norm_ple_post,
               w_conv_a=w_conv_a, ret_gn=ret_gn, w_pool=w_pool.astype(BF16), pool_scale=pool_scale, sinks=sinks,
               w_conv_ffn=w_conv_ffn)

    def group(x, pos0):
        b, t = x.shape[:2]
        pos = pos0 + jnp.arange(t, dtype=jnp.int32)
        return dict(b=b, t=t, pos0=pos0, kv_lanes=lanes, even_tabs=_rope_tables(pos, dk, 1),
                    odd_tabs=_rope_tables(pos, hd, lanes // hd))

    bp, sp = x_prompt.shape[:2]
    bs, ts = x_sample.shape[:2]
    grp_p, grp_s = group(x_prompt, 0), group(x_sample, PAST_LEN)
    st_p = dict(conv=jnp.zeros((n_even, bp, 2, d_a), dt), ret=jnp.zeros((n_even, bp, n_heads, dk, dv), dt),
                pool=jnp.zeros((n_odd, bp, POOL_HIST, d_c), dt), cache_k=None, cache_v=None,
                ffn=jnp.zeros((depth, bp, 2, d_ff), dt))
    st_s = dict(conv=state_conv, ret=state_ret, pool=state_pool, cache_k=cache_k, cache_v=cache_v, ffn=state_ffn)
    cfg_s = dict(tm_in=bs * ts, tn_even=1024, tn_odd=w_in_odd.shape[-1] // 2, blk_even=ts, blk_odd=ts,
                 tm_out=bs * ts, tk_out=512, tm_ffn=bs * ts, tf=512, tm_ple=bs * ts, tk_ple=512, kv_rows=kv_rows)
    cfg_p = dict(tm_in=1024, tn_even=1024, tn_odd=w_in_odd.shape[-1] // 2, blk_even=256, blk_odd=128,
                 tm_out=512, tk_out=None, tm_ffn=1024, tf=512, tm_ple=1024, tk_ple=None, kv_rows=kv_rows)
    xp = x_prompt.reshape(bp * sp, d)
    xs = x_sample.reshape(bs * ts, d)
    new_p, new_s = [], []
    for i in range(depth):
        j = i // 2
        raw = dict(w_in=LayerWeight(w_in_even if i % 2 == 0 else w_in_odd, j),
                   w_out=LayerWeight(w_out_even if i % 2 == 0 else w_out_odd, j),
                   w_up=LayerWeight(w_up, i), w_down=LayerWeight(w_down, i),
                   w_gate=LayerWeight(w_ple_gate, i), w_proj=LayerWeight(w_ple_proj, i))
        xs, st, wq = _layer(i, xs, p_sample[i].reshape(bs * ts, -1), grp_s, st_s, raw, prm, cfg_s)
        new_s.append(st)
        xp, st, _ = _layer(i, xp, p_prompt[i].reshape(bp * sp, -1), grp_p, st_p, wq, prm, cfg_p)
        new_p.append(st)

    def stacked(new, key):
        return jnp.stack([st[key] for st in new if key in st])

    outs = [xp.reshape(bp, sp, d), xs.reshape(bs, ts, d)]
    for key in ('conv', 'ret', 'pool', 'k', 'v', 'ffn'):
        outs += [stacked(new_p, key), stacked(new_s, key)]
    return tuple(outs)
```

```python
import functools
import math

import jax
import jax.numpy as jnp
from jax import lax
from jax.experimental import pallas as pl
from jax.experimental.pallas import tpu as pltpu

CHUNK = 64
WINDOW = 128
PAST_LEN = 4096
EPS = 1e-6
ROPE_THETA = 10000.0
NEG_INF = -1e30
POOL_WINDOWS = (2, 4, 8, 16)
POOL_HIST = max(POOL_WINDOWS) - 1

V7X_VMEM_BYTES = 64 * 1024 * 1024
VMEM_LIMIT_BYTES = 58 * 1024 * 1024
MXU_COLS = 256

F32 = jnp.float32
BF16 = jnp.bfloat16


def _params():
    return pltpu.CompilerParams(vmem_limit_bytes=VMEM_LIMIT_BYTES)


def _rms(x, g):
    return x * lax.rsqrt(jnp.mean(x * x, axis=-1, keepdims=True) + EPS) * g


ROW_CHUNK = 256


def _for_row_chunks(n_rows, fn):
    if n_rows <= ROW_CHUNK or n_rows % ROW_CHUNK:
        fn(slice(0, n_rows))
        return

    def body(r, carry):
        fn(pl.ds(pl.multiple_of(r * ROW_CHUNK, ROW_CHUNK), ROW_CHUNK))
        return carry

    lax.fori_loop(0, n_rows // ROW_CHUNK, body, 0)


def _resident_spec(block, index_map):
    return pl.BlockSpec(block, index_map, pipeline_mode=pl.Buffered(1))


def _row_tile(rows, want):
    t = min(rows, want)
    assert rows % t == 0, (rows, t)
    return t


class LayerWeight:
    def __init__(self, stack, layer):
        self.stack, self.layer = stack, layer
        self.shape = stack.shape[1:]

    def spec(self, block, index_map):
        layer = self.layer
        return pl.BlockSpec((None,) + block, lambda *g: (layer,) + index_map(*g))


def _weight_in(w, block, index_map):
    if isinstance(w, LayerWeight):
        return (w.stack, w.spec(block, index_map), pl.BlockSpec(block, index_map),
                jax.ShapeDtypeStruct(w.shape, BF16))
    return w, pl.BlockSpec(block, index_map), None, None


def _load_weight(w_ref, copy_ref):
    w = w_ref[...]
    if copy_ref is not None:
        w = w.astype(BF16)
        copy_ref[...] = w
    return w


def _norm_matmul_kernel(x_ref, g_ref, w_ref, *rest, emit):
    o_ref, copy_ref, h_ref = rest if emit else (rest[0], None, rest[1])

    @pl.when(pl.program_id(1) == 0)
    def _():
        def start(rows):
            h_ref[rows, :] = _rms(x_ref[rows, :], g_ref[...]).astype(BF16)

        _for_row_chunks(x_ref.shape[0], start)

    w = _load_weight(w_ref, copy_ref)
    o_ref[...] = jnp.dot(h_ref[...], w, preferred_element_type=F32)


def norm_matmul(x, g, w, *, tm, tn):
    rows, d = x.shape
    n = w.shape[1]
    tm = _row_tile(rows, tm)
    assert n % tn == 0
    w_arg, w_spec, copy_spec, copy_shape = _weight_in(w, (d, tn), lambda i, j: (0, j))
    emit = copy_spec is not None
    assert not emit or rows == tm, "a bf16 copy needs every weight tile visited exactly once"
    out_specs = [pl.BlockSpec((tm, tn), lambda i, j: (i, j))]
    out_shape = [jax.ShapeDtypeStruct((rows, n), F32)]
    if emit:
        out_specs.append(copy_spec)
        out_shape.append(copy_shape)
    out = pl.pallas_call(
        functools.partial(_norm_matmul_kernel, emit=emit),
        grid=(rows // tm, n // tn),
        in_specs=[
            pl.BlockSpec((tm, d), lambda i, j: (i, 0)),
            pl.BlockSpec((1, d), lambda i, j: (0, 0)),
            w_spec,
        ],
        out_specs=out_specs,
        out_shape=out_shape,
        scratch_shapes=[pltpu.VMEM((tm, d), BF16)],
        compiler_params=_params(),
        name="norm_matmul",
    )(x, g.reshape(1, d), w_arg)
    return (out[0], out[1]) if emit else (out[0], w)


def _even_mixer_kernel(p_ref, cos_ref, sin_ref, chist_ref, rstate_ref, wconv_ref, gn_ref,
                       mix_ref, cstate_ref, rout_ref, carry_ref, s_ref, dmat_ref, qdec_ref, kdec_ref,
                       *, n_heads, dk, dv, d_a):
    q0 = 3 * d_a
    k0 = q0 + n_heads * dk
    v0 = k0 + n_heads * dk
    g0 = v0 + n_heads * dv
    c = pl.program_id(1)
    last = pl.num_programs(1) - 1
    blk = p_ref.shape[0]

    @pl.when(c == 0)
    def _():
        carry_ref[...] = chist_ref[...]
        s_ref[...] = rstate_ref[...]

    u = p_ref[:, 2 * d_a:3 * d_a] * p_ref[:, 0:d_a]
    row = lax.broadcasted_iota(jnp.int32, u.shape, 0)
    h0 = carry_ref[0:1, :]
    h1 = carry_ref[1:2, :]
    prev1 = jnp.where(row == 0, h1, pltpu.roll(u, 1, axis=0))
    prev2 = jnp.where(row == 0, h0, jnp.where(row == 1, h1, pltpu.roll(u, 2, axis=0)))
    w = wconv_ref[...]
    conv = prev2 * w[0:1] + prev1 * w[1:2] + u * w[2:3]
    mix_ref[:, 0:d_a] = (p_ref[:, d_a:2 * d_a] * conv).astype(BF16)
    tail = u[blk - 2:blk, :]
    carry_ref[...] = tail

    @pl.when(c == last)
    def _():
        cstate_ref[...] = tail

    cosb = cos_ref[...]
    sinb = sin_ref[...]
    log_gamma = [math.log1p(-(2.0 ** (-5.0 - h))) for h in range(n_heads)]

    @pl.when((pl.program_id(0) == 0) & (c == 0))
    def _():
        rel = (lax.broadcasted_iota(jnp.int32, (blk, blk), 0)
               - lax.broadcasted_iota(jnp.int32, (blk, blk), 1)).astype(F32)
        ii = lax.broadcasted_iota(jnp.int32, (blk, dk), 0).astype(F32)
        for h, lg in enumerate(log_gamma):
            dmat_ref[h] = jnp.where(rel >= 0, jnp.exp(jnp.maximum(rel, 0.0) * lg), 0.0)
            qdec_ref[h] = jnp.exp((ii + 1.0) * lg)
            kdec_ref[h] = jnp.exp((blk - 1.0 - ii) * lg)

    stage = []
    for h in range(n_heads):
        q = p_ref[:, q0 + h * dk:q0 + (h + 1) * dk]
        k = p_ref[:, k0 + h * dk:k0 + (h + 1) * dk]
        qr = q * cosb + pltpu.roll(q, dk // 2, axis=1) * sinb
        kr = (k * cosb + pltpu.roll(k, dk // 2, axis=1) * sinb) * (dk ** -0.5)
        vb = p_ref[:, v0 + h * dv:v0 + (h + 1) * dv].astype(BF16)
        sc = lax.dot_general(qr.astype(BF16), kr.astype(BF16), (((1,), (1,)), ((), ())),
                             preferred_element_type=F32)
        st = s_ref[h]
        inter = jnp.dot((qr * qdec_ref[h]).astype(BF16), st.astype(BF16), preferred_element_type=F32)
        kv = lax.dot_general((kr * kdec_ref[h]).astype(BF16), vb, (((0,), (0,)), ((), ())),
                             preferred_element_type=F32)
        s_ref[h] = math.exp(blk * log_gamma[h]) * st + kv
        stage.append((sc, inter, vb))
    for h, (sc, inter, vb) in enumerate(stage):
        o = jnp.dot((sc * dmat_ref[h]).astype(BF16), vb, preferred_element_type=F32) + inter
        on = o * lax.rsqrt(jnp.mean(o * o, axis=-1, keepdims=True) + EPS) * gn_ref[h:h + 1, :]
        gate = p_ref[:, g0 + h * dv:g0 + (h + 1) * dv]
        mix_ref[:, d_a + h * dv:d_a + (h + 1) * dv] = (on * (gate * jax.nn.sigmoid(gate))).astype(BF16)

    @pl.when(c == last)
    def _():
        rout_ref[...] = s_ref[...]


def even_mixer(p, cosb, sinb, conv_hist, ret_state, w_conv, gn, *, blk):
    b, t, width = p.shape
    n_heads, dk, dv = ret_state.shape[1:]
    d_a = w_conv.shape[1]
    hv = n_heads * dv
    assert width == 3 * d_a + 2 * n_heads * dk + 2 * hv
    blk = _row_tile(t, blk)
    nc = t // blk
    kern = functools.partial(_even_mixer_kernel, n_heads=n_heads, dk=dk, dv=dv, d_a=d_a)
    return pl.pallas_call(
        kern,
        grid=(b, nc),
        in_specs=[
            pl.BlockSpec((None, blk, width), lambda bi, ci: (bi, ci, 0)),
            pl.BlockSpec((blk, dk), lambda bi, ci: (ci, 0)),
            pl.BlockSpec((blk, dk), lambda bi, ci: (ci, 0)),
            pl.BlockSpec((None, 2, d_a), lambda bi, ci: (bi, 0, 0)),
            pl.BlockSpec((None, n_heads, dk, dv), lambda bi, ci: (bi, 0, 0, 0)),
            pl.BlockSpec((3, d_a), lambda bi, ci: (0, 0)),
            pl.BlockSpec((n_heads, dv), lambda bi, ci: (0, 0)),
        ],
        out_specs=[
            pl.BlockSpec((None, blk, d_a + hv), lambda bi, ci: (bi, ci, 0)),
            pl.BlockSpec((None, 2, d_a), lambda bi, ci: (bi, 0, 0)),
            pl.BlockSpec((None, n_heads, dk, dv), lambda bi, ci: (bi, 0, 0, 0)),
        ],
        out_shape=[
            jax.ShapeDtypeStruct((b, t, d_a + hv), BF16),
            jax.ShapeDtypeStruct((b, 2, d_a), F32),
            jax.ShapeDtypeStruct((b, n_heads, dk, dv), F32),
        ],
        scratch_shapes=[pltpu.VMEM((2, d_a), F32), pltpu.VMEM((n_heads, dk, dv), F32),
                        pltpu.VMEM((n_heads, blk, blk), F32), pltpu.VMEM((n_heads, blk, dk), F32),
                        pltpu.VMEM((n_heads, blk, dk), F32)],
        compiler_params=_params(),
        name="even_mixer",
    )(p, cosb, sinb, conv_hist, ret_state, w_conv, gn)


def _odd_mixer_kernel(sinks_ref, p_ref, cos_ref, sin_ref,
                      phist_ref, ck_ref, cv_ref, wpool_ref, pscale_ref,
                      mix_ref, pstate_ref, kstate_ref, vstate_ref,
                      ext_ref, kbuf_ref, vbuf_ref,
                      *, d_c, n_q_heads, n_kv_heads, hd, pos0, has_cache, n_state_rows):
    c = pl.program_id(1)
    last = pl.num_programs(1) - 1
    blk = p_ref.shape[0]
    hist_rows = POOL_HIST + 1
    d_cg = d_c // len(POOL_WINDOWS)
    lanes = 2 * hd
    q0 = d_c
    k0 = q0 + n_q_heads * hd
    v0 = k0 + lanes

    @pl.when(c == 0)
    def _():
        ext_ref[0:1, :] = jnp.zeros((1, d_c), F32)
        ext_ref[1:hist_rows, :] = phist_ref[...]
        kbuf_ref[0:WINDOW, :] = ck_ref[...]
        vbuf_ref[0:WINDOW, :] = cv_ref[...]

    u = p_ref[:, 0:d_c]
    ext_ref[hist_rows:hist_rows + blk, :] = u
    pos = pos0 + c * blk + lax.broadcasted_iota(jnp.int32, (blk, 1), 0)
    for gi, win in enumerate(POOL_WINDOWS):
        cols = slice(gi * d_cg, (gi + 1) * d_cg)
        s = ext_ref[:, cols]
        span = 1
        while span < win:
            s = s + pltpu.roll(s, span, axis=0)
            span *= 2
        cnt = jnp.minimum(pos + 1, win).astype(F32)
        d = s[hist_rows:, :] / cnt - u[:, cols]
        y = jnp.dot(d.astype(BF16), wpool_ref[gi], preferred_element_type=F32)
        mix_ref[:, cols] = (y * pscale_ref[:, cols]).astype(BF16)
    tail = ext_ref[blk:blk + hist_rows, :]
    ext_ref[0:hist_rows, :] = tail

    @pl.when(c == last)
    def _():
        pstate_ref[...] = ext_ref[1:hist_rows, :]

    cos4 = cos_ref[...]
    sin4 = sin_ref[...]
    lane = lax.broadcasted_iota(jnp.int32, (1, lanes), 1)
    first_half = jnp.bitwise_and(lane, hd - 1) < (hd // 2)
    left = lane < hd

    def rope(x):
        rot = jnp.where(first_half, pltpu.roll(x, lanes - hd // 2, axis=1), pltpu.roll(x, hd // 2, axis=1))
        return x * cos4 + rot * sin4

    kr = rope(p_ref[:, k0:k0 + lanes])
    vv = p_ref[:, v0:v0 + lanes]
    kbuf_ref[WINDOW:WINDOW + blk, :] = kr
    vbuf_ref[WINDOW:WINDOW + blk, :] = vv
    kall = kbuf_ref[...]
    vall = vbuf_ref[...]
    kswap = pltpu.roll(kall, hd, axis=1)
    vswap = pltpu.roll(vall, hd, axis=1)
    nk = WINDOW + blk
    chunk_shift = CHUNK.bit_length() - 1
    qi = jnp.right_shift(lax.broadcasted_iota(jnp.int32, (blk, 1), 0), chunk_shift)
    kj = lax.broadcasted_iota(jnp.int32, (1, nk), 1)
    kc = jnp.right_shift(kj, chunk_shift) - WINDOW // CHUNK
    ok = (kc <= qi) & (kc >= qi - WINDOW // CHUNK)
    if not has_cache:
        ok = ok & ((kj >= WINDOW) | (c > 0))
    scale = hd ** -0.5
    group = n_q_heads // n_kv_heads
    zeros = jnp.zeros_like(vall)
    kv_ops = [
        (jnp.where(left, kall, kswap).astype(BF16), jnp.where(left, vall, zeros).astype(BF16),
         jnp.where(left, zeros, vswap).astype(BF16)),
        (jnp.where(left, kswap, kall).astype(BF16), jnp.where(left, vswap, zeros).astype(BF16),
         jnp.where(left, zeros, vall).astype(BF16)),
    ]
    scores = []
    for head in range(n_q_heads):
        kvh = head // group
        if head % 2 == 0:
            qr = rope(p_ref[:, q0 + (head // 2) * lanes:q0 + (head // 2 + 1) * lanes])
        qh = jnp.where(left if head % 2 == 0 else jnp.logical_not(left), qr, 0.0).astype(BF16)
        scores.append(lax.dot_general(qh, kv_ops[kvh][0], (((1,), (1,)), ((), ())),
                                      preferred_element_type=F32))
    probs = []
    for head, sc in enumerate(scores):
        sc = jnp.where(ok, sc * scale, NEG_INF)
        sink = sinks_ref[head]
        m = jnp.maximum(jnp.max(sc, axis=-1, keepdims=True), sink)
        e = jnp.exp(sc - m)
        den = jnp.sum(e, axis=-1, keepdims=True) + jnp.exp(sink - m)
        probs.append((e / den).astype(BF16))
    for pair in range(n_q_heads // 2):
        kvh = (2 * pair) // group
        assert (2 * pair + 1) // group == kvh
        out = (jnp.dot(probs[2 * pair], kv_ops[kvh][1], preferred_element_type=F32)
               + jnp.dot(probs[2 * pair + 1], kv_ops[kvh][2], preferred_element_type=F32))
        mix_ref[:, d_c + pair * lanes:d_c + (pair + 1) * lanes] = out.astype(BF16)
    ktail = kbuf_ref[blk:blk + WINDOW, :]
    vtail = vbuf_ref[blk:blk + WINDOW, :]
    kbuf_ref[0:WINDOW, :] = ktail
    vbuf_ref[0:WINDOW, :] = vtail

    @pl.when(c == last)
    def _():
        kstate_ref[...] = kbuf_ref[nk - n_state_rows:nk, :]
        vstate_ref[...] = vbuf_ref[nk - n_state_rows:nk, :]


def odd_mixer(p, cos4, sin4, pool_hist, cache_k, cache_v, w_pool, pool_scale, sinks, *,
              blk, pos0, has_cache, n_state_rows):
    b, t, width = p.shape
    d_c = pool_scale.shape[-1]
    n_q_heads = sinks.shape[-1]
    lanes = cache_k.shape[-1]
    n_kv_heads = 2
    hd = lanes // n_kv_heads
    d_q = n_q_heads * hd
    assert d_q == d_c and d_c % lanes == 0 and width == d_c + d_q + 2 * lanes
    blk = _row_tile(t, blk)
    nc = t // blk
    assert blk % CHUNK == 0 or nc == 1
    kern = functools.partial(_odd_mixer_kernel, d_c=d_c, n_q_heads=n_q_heads, n_kv_heads=n_kv_heads,
                             hd=hd, pos0=pos0, has_cache=has_cache, n_state_rows=n_state_rows)
    return pl.pallas_call(
        kern,
        grid=(b, nc),
        in_specs=[
            pl.BlockSpec(memory_space=pltpu.SMEM),
            pl.BlockSpec((None, blk, width), lambda bi, ci: (bi, ci, 0)),
            pl.BlockSpec((blk, lanes), lambda bi, ci: (ci, 0)),
            pl.BlockSpec((blk, lanes), lambda bi, ci: (ci, 0)),
            pl.BlockSpec((None, POOL_HIST, d_c), lambda bi, ci: (bi, 0, 0)),
            pl.BlockSpec((None, WINDOW, lanes), lambda bi, ci: (bi, 0, 0)),
            pl.BlockSpec((None, WINDOW, lanes), lambda bi, ci: (bi, 0, 0)),
            pl.BlockSpec(w_pool.shape, lambda bi, ci: (0, 0, 0)),
            pl.BlockSpec((1, d_c), lambda bi, ci: (0, 0)),
        ],
        out_specs=[
            pl.BlockSpec((None, blk, d_c + d_q), lambda bi, ci: (bi, ci, 0)),
            pl.BlockSpec((None, POOL_HIST, d_c), lambda bi, ci: (bi, 0, 0)),
            pl.BlockSpec((None, n_state_rows, lanes), lambda bi, ci: (bi, 0, 0)),
            pl.BlockSpec((None, n_state_rows, lanes), lambda bi, ci: (bi, 0, 0)),
        ],
        out_shape=[
            jax.ShapeDtypeStruct((b, t, d_c + d_q), BF16),
            jax.ShapeDtypeStruct((b, POOL_HIST, d_c), F32),
            jax.ShapeDtypeStruct((b, n_state_rows, lanes), F32),
            jax.ShapeDtypeStruct((b, n_state_rows, lanes), F32),
        ],
        scratch_shapes=[
            pltpu.VMEM((POOL_HIST + 1 + blk, d_c), F32),
            pltpu.VMEM((WINDOW + blk, lanes), F32),
            pltpu.VMEM((WINDOW + blk, lanes), F32),
        ],
        compiler_params=_params(),
        name="odd_mixer",
    )(sinks, p, cos4, sin4, pool_hist, cache_k, cache_v, w_pool, pool_scale.reshape(1, d_c))


def _accumulate(o_ref, part, k):
    @pl.when(k == 0)
    def _():
        o_ref[...] = part

    @pl.when(k > 0)
    def _():
        o_ref[...] += part


def _out_proj_kernel(mix_ref, w_ref, x_ref, g_ref, o_ref, *rest, n_k):
    copy_ref = rest[0] if rest else None
    k = pl.program_id(1)
    part = jnp.dot(mix_ref[...], _load_weight(w_ref, copy_ref), preferred_element_type=F32)
    if n_k == 1:
        o_ref[...] = x_ref[...] + _rms(part, g_ref[...])
    else:
        _accumulate(o_ref, part, k)

        @pl.when(k == n_k - 1)
        def _():
            o_ref[...] = x_ref[...] + _rms(o_ref[...], g_ref[...])


def out_proj(mix, w, x, g, *, tm, tk=None):
    rows, kdim = mix.shape
    d = x.shape[1]
    tm = _row_tile(rows, tm)
    tk = kdim if tk is None else tk
    assert kdim % tk == 0
    w_arg, w_spec, copy_spec, copy_shape = _weight_in(w, (tk, d), lambda i, k: (k, 0))
    emit = copy_spec is not None
    assert not emit or rows == tm, "a bf16 copy needs every weight tile visited exactly once"
    if not emit and tk == kdim:
        w_spec = _resident_spec((tk, d), lambda i, k: (0, 0))
    out_specs = [pl.BlockSpec((tm, d), lambda i, k: (i, 0))]
    out_shape = [jax.ShapeDtypeStruct((rows, d), F32)]
    if emit:
        out_specs.append(copy_spec)
        out_shape.append(copy_shape)
    out = pl.pallas_call(
        functools.partial(_out_proj_kernel, n_k=kdim // tk),
        grid=(rows // tm, kdim // tk),
        in_specs=[
            pl.BlockSpec((tm, tk), lambda i, k: (i, k)),
            w_spec,
            pl.BlockSpec((tm, d), lambda i, k: (i, 0)),
            pl.BlockSpec((1, d), lambda i, k: (0, 0)),
        ],
        out_specs=out_specs,
        out_shape=out_shape,
        compiler_params=_params(),
        name="out_proj",
    )(mix, w_arg, x, g.reshape(1, d))
    return (out[0], out[1]) if emit else (out[0], w)


def _ffn_kernel(x_ref, gpre_ref, wa_ref, wg_ref, wconv_ref, wd_ref, gpost_ref, hist_ref,
                o_ref, state_ref, *rest, n_seq, blocks_per_seq, sub, emit):
    if emit:
        wa_copy_ref, wg_copy_ref, wd_copy_ref, h_ref, carry_ref = rest
    else:
        wa_copy_ref = wg_copy_ref = wd_copy_ref = None
        h_ref, carry_ref = rest
    i = pl.program_id(0)
    j = pl.program_id(1)
    last_j = pl.num_programs(1) - 1
    tm = x_ref.shape[0]
    tf = wa_ref.shape[1]
    rows_per_seq = tm // n_seq

    @pl.when(j == 0)
    def _():
        def start(rows):
            x = x_ref[rows, :]
            h_ref[rows, :] = _rms(x, gpre_ref[...]).astype(BF16)
            o_ref[rows, :] = jnp.zeros_like(x)

        _for_row_chunks(tm, start)

    if blocks_per_seq > 1:
        @pl.when(i % blocks_per_seq == 0)
        def _():
            carry_ref[j] = hist_ref[0]

    h = h_ref[...]
    assert rows_per_seq & (rows_per_seq - 1) == 0
    t = jnp.bitwise_and(lax.broadcasted_iota(jnp.int32, (tm, 1), 0), rows_per_seq - 1)
    groups = [slice(s * sub, (s + 1) * sub) for s in range(tf // sub)]
    if emit:
        wa, wg, wd = (_load_weight(wa_ref, wa_copy_ref), _load_weight(wg_ref, wg_copy_ref),
                      _load_weight(wd_ref, wd_copy_ref))
    else:
        wa, wg, wd = wa_ref, wg_ref, wd_ref
    ups = [(jnp.dot(h, wa[:, cols], preferred_element_type=F32),
            jnp.dot(h, wg[:, cols], preferred_element_type=F32)) for cols in groups]
    for cols, (a, val) in zip(groups, ups):
        if blocks_per_seq > 1:
            hist0 = carry_ref[j, 0:1, cols]
            hist1 = carry_ref[j, 1:2, cols]
        else:
            hist = hist_ref[:, :, cols]
            hist0 = jnp.broadcast_to(hist[:, 0:1, :], (n_seq, rows_per_seq, sub)).reshape(tm, sub)
            hist1 = jnp.broadcast_to(hist[:, 1:2, :], (n_seq, rows_per_seq, sub)).reshape(tm, sub)
        prev1 = jnp.where(t == 0, hist1, pltpu.roll(a, 1, axis=0))
        prev2 = jnp.where(t == 0, hist0, jnp.where(t == 1, hist1, pltpu.roll(a, 2, axis=0)))
        w = wconv_ref[:, cols]
        conv = prev2 * w[0:1] + prev1 * w[1:2] + a * w[2:3]
        z = (jax.nn.gelu(conv, approximate=True) * val).astype(BF16)
        o_ref[...] += jnp.dot(z, wd[cols, :], preferred_element_type=F32)
        tail = a.reshape(n_seq, rows_per_seq, sub)[:, rows_per_seq - 2:, :]
        state_ref[j, :, :, cols] = tail
        if blocks_per_seq > 1:
            carry_ref[j, :, cols] = tail[0]

    @pl.when(j == last_j)
    def _():
        def finish(rows):
            o_ref[rows, :] = x_ref[rows, :] + _rms(o_ref[rows, :], gpost_ref[...])

        _for_row_chunks(tm, finish)


def conv_ffn(x, g_pre, w_up, w_conv, w_down, g_post, hist, *, seq_len, tm, tf):
    rows, d = x.shape
    d_ff = w_down.shape[0]
    tm = _row_tile(rows, tm)
    assert d_ff % tf == 0 and tf % MXU_COLS == 0
    n_ff = d_ff // tf
    emit = isinstance(w_up, LayerWeight)
    assert emit == isinstance(w_down, LayerWeight)
    if emit:
        assert rows == tm, "a bf16 copy needs every weight tile visited exactly once"
        half = jax.ShapeDtypeStruct((d, d_ff), BF16)
        up_args = (w_up.stack, w_up.stack)
        up_specs = [w_up.spec((d, tf), lambda i, j: (0, j)), w_up.spec((d, tf), lambda i, j: (0, j + n_ff))]
        copy_specs = [pl.BlockSpec((d, tf), lambda i, j: (0, j)), pl.BlockSpec((d, tf), lambda i, j: (0, j))]
        copy_shapes = [half, half]
    else:
        up_args = tuple(w_up)
        up_specs = [pl.BlockSpec((d, tf), lambda i, j: (0, j)), pl.BlockSpec((d, tf), lambda i, j: (0, j))]
        copy_specs, copy_shapes = [], []
    wd_arg, wd_spec, wd_copy_spec, wd_copy_shape = _weight_in(w_down, (tf, d), lambda i, j: (j, 0))
    if emit:
        copy_specs.append(wd_copy_spec)
        copy_shapes.append(wd_copy_shape)
    if seq_len >= tm:
        assert seq_len % tm == 0
        n_seq, blocks_per_seq = 1, seq_len // tm
        seq_of = lambda i: i // blocks_per_seq
    else:
        assert tm % seq_len == 0
        n_seq, blocks_per_seq = tm // seq_len, 1
        seq_of = lambda i: i
    kern = functools.partial(_ffn_kernel, n_seq=n_seq, blocks_per_seq=blocks_per_seq, sub=MXU_COLS, emit=emit)
    out = pl.pallas_call(
        kern,
        grid=(rows // tm, n_ff),
        in_specs=[
            (_resident_spec if rows == tm else pl.BlockSpec)((tm, d), lambda i, j: (i, 0)),
            pl.BlockSpec((1, d), lambda i, j: (0, 0)),
            up_specs[0],
            up_specs[1],
            pl.BlockSpec((3, tf), lambda i, j: (0, j)),
            wd_spec,
            pl.BlockSpec((1, d), lambda i, j: (0, 0)),
            pl.BlockSpec((n_seq, 2, tf), lambda i, j: (seq_of(i), 0, j)),
        ],
        out_specs=[
            pl.BlockSpec((tm, d), lambda i, j: (i, 0)),
            pl.BlockSpec((n_ff, n_seq, 2, tf), lambda i, j: (0, seq_of(i), 0, 0)),
        ] + copy_specs,
        out_shape=[
            jax.ShapeDtypeStruct((rows, d), F32),
            jax.ShapeDtypeStruct((n_ff, rows // seq_len, 2, tf), F32),
        ] + copy_shapes,
        scratch_shapes=[pltpu.VMEM((tm, d), BF16), pltpu.VMEM((n_ff, 2, tf), F32)],
        compiler_params=_params(),
        name="conv_ffn",
    )(x, g_pre.reshape(1, d), *up_args, w_conv, wd_arg, g_post.reshape(1, d), hist)
    state = jnp.moveaxis(out[1], 0, 2).reshape(rows // seq_len, 2, d_ff)
    weights = ((out[2], out[3]), out[4]) if emit else (tuple(w_up), w_down)
    return out[0], state, weights


def _ple_kernel(x_ref, p_ref, gpre_ref, wgate_ref, wproj_ref, gpost_ref, o_ref, *rest, n_k, emit):
    wgate_copy_ref, wproj_copy_ref = rest[:2] if emit else (None, None)
    h_ref = rest[-1] if n_k > 1 else None
    k = pl.program_id(1)
    tk = wgate_ref.shape[0]

    def finish(pre):
        emb = jnp.dot(p_ref[...].astype(BF16), _load_weight(wproj_ref, wproj_copy_ref),
                      preferred_element_type=F32)
        o_ref[...] = x_ref[...] + _rms(jax.nn.sigmoid(pre) * emb, gpost_ref[...])

    if n_k == 1:
        h = _rms(x_ref[...], gpre_ref[...]).astype(BF16)
        finish(jnp.dot(h, _load_weight(wgate_ref, wgate_copy_ref), preferred_element_type=F32))
    else:
        @pl.when(k == 0)
        def _():
            h = _rms(x_ref[...], gpre_ref[...]).astype(BF16)
            for kk in range(n_k):
                h_ref[kk] = h[:, kk * tk:(kk + 1) * tk]

        _accumulate(o_ref, jnp.dot(h_ref[k], _load_weight(wgate_ref, wgate_copy_ref),
                                   preferred_element_type=F32), k)

        @pl.when(k == n_k - 1)
        def _():
            finish(o_ref[...])


def per_layer_embedding(x, p, layer, g_pre, w_gate, w_proj, g_post, *, tm, tk=None):
    rows, d = x.shape
    pd = p.shape[-1]
    tm = _row_tile(rows, tm)
    tk = d if tk is None else tk
    assert d % tk == 0
    n_k = d // tk
    wg_arg, wg_spec, wg_copy_spec, wg_copy_shape = _weight_in(w_gate, (tk, d), lambda i, k: (k, 0))
    wp_arg, wp_spec, wp_copy_spec, wp_copy_shape = _weight_in(w_proj, (pd, d), lambda i, k: (0, 0))
    emit = wg_copy_spec is not None
    assert emit == (wp_copy_spec is not None)
    assert not emit or rows == tm, "a bf16 copy needs every weight tile visited exactly once"
    if not emit:
        wp_spec = _resident_spec((pd, d), lambda i, k: (0, 0))
        if n_k == 1:
            wg_spec = _resident_spec((tk, d), lambda i, k: (0, 0))
    out_specs = [pl.BlockSpec((tm, d), lambda i, k: (i, 0))]
    out_shape = [jax.ShapeDtypeStruct((rows, d), F32)]
    if emit:
        out_specs += [wg_copy_spec, wp_copy_spec]
        out_shape += [wg_copy_shape, wp_copy_shape]
    out = pl.pallas_call(
        functools.partial(_ple_kernel, n_k=n_k, emit=emit),
        grid=(rows // tm, n_k),
        in_specs=[
            pl.BlockSpec((tm, d), lambda i, k: (i, 0)),
            pl.BlockSpec((None, tm, pd), lambda i, k: (layer, i, 0)),
            pl.BlockSpec((1, d), lambda i, k: (0, 0)),
            wg_spec,
            wp_spec,
            pl.BlockSpec((1, d), lambda i, k: (0, 0)),
        ],
        out_specs=out_specs,
        out_shape=out_shape,
        scratch_shapes=[pltpu.VMEM((n_k, tm, tk), BF16)] if n_k > 1 else [],
        compiler_params=_params(),
        name="per_layer_embedding",
    )(x, p, g_pre.reshape(1, d), wg_arg, wp_arg, g_post.reshape(1, d))
    return (out[0], (out[1], out[2])) if emit else (out[0], (w_gate, w_proj))


def _rope_tables(pos, dim, reps):
    inv = 1.0 / (ROPE_THETA ** (jnp.arange(0, dim, 2, dtype=F32) / dim))
    ang = pos.astype(F32)[:, None] * inv[None, :]
    cos, sin = jnp.cos(ang), jnp.sin(ang)
    return (jnp.tile(jnp.concatenate([cos, cos], axis=-1), (1, reps)),
            jnp.tile(jnp.concatenate([-sin, sin], axis=-1), (1, reps)))


def _layer(i, x, p, grp, states, wts, prm, cfg):
    b, t = grp['b'], grp['t']
    rows = b * t
    j = i // 2
    wq = {}
    if i % 2 == 0:
        proj, wq['w_in'] = norm_matmul(x, prm['norm_mix_pre'][i], wts['w_in'], tm=cfg['tm_in'], tn=cfg['tn_even'])
        mix, c_s, r_s = even_mixer(proj.reshape(b, t, -1), *grp['even_tabs'], states['conv'][j], states['ret'][j],
                                   prm['w_conv_a'][j], prm['ret_gn'][j], blk=cfg['blk_even'])
        new = dict(conv=c_s, ret=r_s)
    else:
        proj, wq['w_in'] = norm_matmul(x, prm['norm_mix_pre'][i], wts['w_in'], tm=cfg['tm_in'], tn=cfg['tn_odd'])
        lanes = grp['kv_lanes']
        has_cache = states['cache_k'] is not None
        if has_cache:
            ck = states['cache_k'][j].reshape(b, WINDOW, lanes)
            cv = states['cache_v'][j].reshape(b, WINDOW, lanes)
            n_state_rows = t
        else:
            ck = cv = jnp.zeros((b, WINDOW, lanes), F32)
            n_state_rows = cfg['kv_rows']
        mix, p_s, k_s, v_s = odd_mixer(proj.reshape(b, t, -1), *grp['odd_tabs'], states['pool'][j], ck, cv,
                                       prm['w_pool'][j], prm['pool_scale'][j], prm['sinks'][j],
                                       blk=cfg['blk_odd'], pos0=grp['pos0'], has_cache=has_cache,
                                       n_state_rows=n_state_rows)
        kv_shape = (b, n_state_rows, 2, lanes // 2)
        new = dict(pool=p_s, k=k_s.reshape(kv_shape), v=v_s.reshape(kv_shape))
    x, wq['w_out'] = out_proj(mix.reshape(rows, -1), wts['w_out'], x, prm['norm_mix_post'][i],
                              tm=cfg['tm_out'], tk=cfg['tk_out'])
    x, new['ffn'], (wq['w_up'], wq['w_down']) = conv_ffn(
        x, prm['norm_ffn_pre'][i], wts['w_up'], prm['w_conv_ffn'][i], wts['w_down'], prm['norm_ffn_post'][i],
        states['ffn'][i], seq_len=t, tm=cfg['tm_ffn'], tf=cfg['tf'])
    x, (wq['w_gate'], wq['w_proj']) = per_layer_embedding(
        x, p, i, prm['norm_ple_pre'][i], wts['w_gate'], wts['w_proj'], prm['norm_ple_post'][i],
        tm=cfg['tm_ple'], tk=cfg['tk_ple'])
    return x, new, wq


def kernel(x_prompt, x_sample, state_conv, state_ret, state_pool, cache_k, cache_v, state_ffn, p_prompt, p_sample, norm_mix_pre, norm_mix_post, norm_ffn_pre, norm_ffn_post, norm_ple_pre, norm_ple_post, w_in_even, w_conv_a, ret_gn, w_out_even, w_in_odd, w_pool, pool_scale, sinks, w_out_odd, w_up, w_conv_ffn, w_down, w_ple_gate, w_ple_proj):
    depth = norm_mix_pre.shape[0]
    n_even, n_odd = w_in_even.shape[0], w_in_odd.shape[0]
    d = x_prompt.shape[-1]
    d_a = w_conv_a.shape[-1]
    n_heads, dk, dv = state_ret.shape[2:]
    d_c = pool_scale.shape[-1]
    kv_rows, n_kv, hd = cache_k.shape[2:]
    d_ff = w_conv_ffn.shape[-1]
    lanes = n_kv * hd
    dt = x_prompt.dtype

    prm = dict(norm_mix_pre=norm_mix_pre, norm_mix_post=norm_mix_post, norm_ffn_pre=norm_ffn_pre,
               norm_ffn_post=norm_ffn_post, norm_ple_pre=norm_ple_pre, norm_ple_post=norm_ple_post,
               w_conv_a=w_conv_a, ret_gn=ret_gn, w_pool=w_pool.astype(BF16), pool_scale=pool_scale, sinks=sinks,
               w_conv_ffn=w_conv_ffn)

    def group(x, pos0):
        b, t = x.shape[:2]
        pos = pos0 + jnp.arange(t, dtype=jnp.int32)
        return dict(b=b, t=t, pos0=pos0, kv_lanes=lanes, even_tabs=_rope_tables(pos, dk, 1),
                    odd_tabs=_rope_tables(pos, hd, lanes // hd))

    bp, sp = x_prompt.shape[:2]
    bs, ts = x_sample.shape[:2]
    grp_p, grp_s = group(x_prompt, 0), group(x_sample, PAST_LEN)
    st_p = dict(conv=jnp.zeros((n_even, bp, 2, d_a), dt), ret=jnp.zeros((n_even, bp, n_heads, dk, dv), dt),
                pool=jnp.zeros((n_odd, bp, POOL_HIST, d_c), dt), cache_k=None, cache_v=None,
                ffn=jnp.zeros((depth, bp, 2, d_ff), dt))
    st_s = dict(conv=state_conv, ret=state_ret, pool=state_pool, cache_k=cache_k, cache_v=cache_v, ffn=state_ffn)
    cfg_s = dict(tm_in=bs * ts, tn_even=1024, tn_odd=w_in_odd.shape[-1] // 2, blk_even=ts, blk_odd=ts,
                 tm_out=bs * ts, tk_out=512, tm_ffn=bs * ts, tf=512, tm_ple=bs * ts, tk_ple=512, kv_rows=kv_rows)
    cfg_p = dict(tm_in=1024, tn_even=1024, tn_odd=w_in_odd.shape[-1] // 2, blk_even=256, blk_odd=128,
                 tm_out=512, tk_out=None, tm_ffn=1024, tf=512, tm_ple=1024, tk_ple=None, kv_rows=kv_rows)
    xp = x_prompt.reshape(bp * sp, d)
    xs = x_sample.reshape(bs * ts, d)
    pp = p_prompt.reshape(depth, bp * sp, -1)
    ps = p_sample.reshape(depth, bs * ts, -1)
    new_p, new_s = [], []
    for i in range(depth):
        j = i // 2
        raw = dict(w_in=LayerWeight(w_in_even if i % 2 == 0 else w_in_odd, j),
                   w_out=LayerWeight(w_out_even if i % 2 == 0 else w_out_odd, j),
                   w_up=LayerWeight(w_up, i), w_down=LayerWeight(w_down, i),
                   w_gate=LayerWeight(w_ple_gate, i), w_proj=LayerWeight(w_ple_proj, i))
        xs, st, wq = _layer(i, xs, ps, grp_s, st_s, raw, prm, cfg_s)
        new_s.append(st)
        xp, st, _ = _layer(i, xp, pp, grp_p, st_p, wq, prm, cfg_p)
        new_p.append(st)

    def stacked(new, key):
        parts = [st[key] for st in new if key in st]
        return parts[0][None] if len(parts) == 1 else jnp.stack(parts)

    outs = [xp.reshape(bp, sp, d), xs.reshape(bs, ts, d)]
    for key in ('conv', 'ret', 'pool', 'k', 'v', 'ffn'):
        outs += [stacked(new_p, key), stacked(new_s, key)]
    return tuple(outs)
```

```python
import functools
import math

import jax
import jax.numpy as jnp
from jax import lax
from jax.experimental import pallas as pl
from jax.experimental.pallas import tpu as pltpu

CHUNK = 64
WINDOW = 128
PAST_LEN = 4096
EPS = 1e-6
ROPE_THETA = 10000.0
NEG_INF = -1e30
POOL_WINDOWS = (2, 4, 8, 16)
POOL_HIST = max(POOL_WINDOWS) - 1

V7X_VMEM_BYTES = 64 * 1024 * 1024
VMEM_LIMIT_BYTES = 58 * 1024 * 1024
MXU_COLS = 256
SUBLANES = 8

F32 = jnp.float32
BF16 = jnp.bfloat16


def _params():
    return pltpu.CompilerParams(vmem_limit_bytes=VMEM_LIMIT_BYTES)


def _rms(x, g):
    return x * lax.rsqrt(jnp.mean(x * x, axis=-1, keepdims=True) + EPS) * g


def _gelu_tanh(x):
    c1 = math.sqrt(2.0 / math.pi)
    half = 0.5 * x
    return half + half * jnp.tanh(x * (c1 + (c1 * 0.044715) * (x * x)))


ROW_CHUNK = 256


def _for_row_chunks(n_rows, fn):
    if n_rows <= ROW_CHUNK or n_rows % ROW_CHUNK:
        fn(slice(0, n_rows))
        return

    def body(r, carry):
        fn(pl.ds(pl.multiple_of(r * ROW_CHUNK, ROW_CHUNK), ROW_CHUNK))
        return carry

    lax.fori_loop(0, n_rows // ROW_CHUNK, body, 0)


def _resident_spec(block, index_map):
    return pl.BlockSpec(block, index_map, pipeline_mode=pl.Buffered(1))


def _row_tile(rows, want):
    t = min(rows, want)
    assert rows % t == 0, (rows, t)
    return t


class LayerWeight:
    def __init__(self, stack, layer):
        self.stack, self.layer = stack, layer
        self.shape = stack.shape[1:]

    def spec(self, block, index_map):
        layer = self.layer
        return pl.BlockSpec((None,) + block, lambda *g: (layer,) + index_map(*g))


def _weight_in(w, block, index_map):
    if isinstance(w, LayerWeight):
        return (w.stack, w.spec(block, index_map), pl.BlockSpec(block, index_map),
                jax.ShapeDtypeStruct(w.shape, BF16))
    return w, pl.BlockSpec(block, index_map), None, None


def _load_weight(w_ref, copy_ref):
    w = w_ref[...]
    if copy_ref is not None:
        w = w.astype(BF16)
        copy_ref[...] = w
    return w


def _norm_matmul_kernel(x_ref, g_ref, w_ref, *rest, emit):
    o_ref, copy_ref, h_ref = rest if emit else (rest[0], None, rest[1])

    @pl.when(pl.program_id(1) == 0)
    def _():
        def start(rows):
            h_ref[rows, :] = _rms(x_ref[rows, :], g_ref[...]).astype(BF16)

        _for_row_chunks(x_ref.shape[0], start)

    w = _load_weight(w_ref, copy_ref)
    o_ref[...] = jnp.dot(h_ref[...], w, preferred_element_type=F32)


def norm_matmul(x, g, w, *, tm, tn):
    rows, d = x.shape
    n = w.shape[1]
    tm = _row_tile(rows, tm)
    assert n % tn == 0
    w_arg, w_spec, copy_spec, copy_shape = _weight_in(w, (d, tn), lambda i, j: (0, j))
    emit = copy_spec is not None
    assert not emit or rows == tm, "a bf16 copy needs every weight tile visited exactly once"
    out_specs = [pl.BlockSpec((tm, tn), lambda i, j: (i, j))]
    out_shape = [jax.ShapeDtypeStruct((rows, n), F32)]
    if emit:
        out_specs.append(copy_spec)
        out_shape.append(copy_shape)
    out = pl.pallas_call(
        functools.partial(_norm_matmul_kernel, emit=emit),
        grid=(rows // tm, n // tn),
        in_specs=[
            pl.BlockSpec((tm, d), lambda i, j: (i, 0)),
            pl.BlockSpec((1, d), lambda i, j: (0, 0)),
            w_spec,
        ],
        out_specs=out_specs,
        out_shape=out_shape,
        scratch_shapes=[pltpu.VMEM((tm, d), BF16)],
        compiler_params=_params(),
        name="norm_matmul",
    )(x, g.reshape(1, d), w_arg)
    return (out[0], out[1]) if emit else (out[0], w)


def _even_in_proj_kernel(x_ref, g_ref, w_ref, cos_ref, sin_ref, o32_ref, o16_ref, *rest,
                         emit, kinds, dk, k_scale):
    copy_ref, h_ref = rest if emit else (None, rest[0])
    j = pl.program_id(1)
    tn = w_ref.shape[1]

    @pl.when(j == 0)
    def _():
        def start(rows):
            h_ref[rows, :] = _rms(x_ref[rows, :], g_ref[...]).astype(BF16)

        _for_row_chunks(x_ref.shape[0], start)

    def project():
        return jnp.dot(h_ref[...], _load_weight(w_ref, copy_ref), preferred_element_type=F32)

    def rope(acc, scale):
        cosb, sinb = cos_ref[...], sin_ref[...]
        for hh in range(tn // dk):
            seg = acc[:, hh * dk:(hh + 1) * dk]
            out = seg * cosb + pltpu.roll(seg, dk // 2, axis=1) * sinb
            if scale != 1.0:
                out = out * scale
            o16_ref[:, hh * dk:(hh + 1) * dk] = out.astype(BF16)

    def finish(kind):
        acc = project()
        if kind == 'f32':
            o32_ref[...] = acc
        elif kind == 'rope':
            rope(acc, 1.0)
        elif kind == 'rope_scaled':
            rope(acc, k_scale)
        elif kind == 'bf16':
            o16_ref[...] = acc.astype(BF16)
        else:
            assert kind == 'silu'
            o16_ref[...] = (acc * jax.nn.sigmoid(acc)).astype(BF16)

    for kind in dict.fromkeys(kinds):
        tiles = [t for t, k in enumerate(kinds) if k == kind]
        here = functools.reduce(jnp.logical_or, [j == t for t in tiles])
        pl.when(here)(functools.partial(finish, kind))


def even_in_proj(x, g, w, cosb, sinb, *, tm, tn, d_a, n_heads, dk, dv):
    rows, d = x.shape
    n = w.shape[1]
    tm = _row_tile(rows, tm)
    hk, hv = n_heads * dk, n_heads * dv
    assert n == 3 * d_a + 2 * hk + 2 * hv and tn % dk == 0
    assert all(part % tn == 0 for part in (d_a, hk, hv))
    kinds = (('f32',) * (3 * d_a // tn) + ('rope',) * (hk // tn) + ('rope_scaled',) * (hk // tn)
             + ('bf16',) * (hv // tn) + ('silu',) * (hv // tn))
    n32 = 3 * d_a // tn
    w_arg, w_spec, copy_spec, copy_shape = _weight_in(w, (d, tn), lambda i, j: (0, j))
    emit = copy_spec is not None
    assert not emit or rows == tm, "a bf16 copy needs every weight tile visited exactly once"
    out_specs = [pl.BlockSpec((tm, tn), lambda i, j: (i, jnp.minimum(j, n32 - 1))),
                 pl.BlockSpec((tm, tn), lambda i, j: (i, jnp.maximum(j - n32, 0)))]
    out_shape = [jax.ShapeDtypeStruct((rows, 3 * d_a), F32), jax.ShapeDtypeStruct((rows, n - 3 * d_a), BF16)]
    if emit:
        out_specs.append(copy_spec)
        out_shape.append(copy_shape)
    kern = functools.partial(_even_in_proj_kernel, emit=emit, kinds=kinds, dk=dk, k_scale=dk ** -0.5)
    out = pl.pallas_call(
        kern,
        grid=(rows // tm, n // tn),
        in_specs=[
            pl.BlockSpec((tm, d), lambda i, j: (i, 0)),
            pl.BlockSpec((1, d), lambda i, j: (0, 0)),
            w_spec,
            pl.BlockSpec((tm, dk), lambda i, j: (i, 0)),
            pl.BlockSpec((tm, dk), lambda i, j: (i, 0)),
        ],
        out_specs=out_specs,
        out_shape=out_shape,
        scratch_shapes=[pltpu.VMEM((tm, d), BF16)],
        compiler_params=_params(),
        name="even_in_proj",
    )(x, g.reshape(1, d), w_arg, cosb, sinb)
    return out[0], out[1], (out[2] if emit else w)


def _even_mixer_kernel(p_ref, pb_ref, chist_ref, rstate_ref, wconv_ref, gn_ref,
                       mix_ref, cstate_ref, rout_ref, carry_ref, s_ref, dmat_ref, qdec_ref, kdec_ref,
                       *, n_heads, dk, dv, d_a):
    k0 = n_heads * dk
    v0 = k0 + n_heads * dk
    g0 = v0 + n_heads * dv
    c = pl.program_id(1)
    last = pl.num_programs(1) - 1
    blk = p_ref.shape[0]

    @pl.when(c == 0)
    def _():
        carry_ref[...] = chist_ref[...]
        s_ref[...] = rstate_ref[...]

    u = p_ref[:, 2 * d_a:3 * d_a] * p_ref[:, 0:d_a]
    row = lax.broadcasted_iota(jnp.int32, u.shape, 0)
    h0 = carry_ref[0:1, :]
    h1 = carry_ref[1:2, :]
    prev1 = jnp.where(row == 0, h1, pltpu.roll(u, 1, axis=0))
    prev2 = jnp.where(row == 0, h0, jnp.where(row == 1, h1, pltpu.roll(u, 2, axis=0)))
    w = wconv_ref[...]
    conv = prev2 * w[0:1] + prev1 * w[1:2] + u * w[2:3]
    mix_ref[:, 0:d_a] = (p_ref[:, d_a:2 * d_a] * conv).astype(BF16)
    tail = u[blk - 2:blk, :]
    carry_ref[...] = tail

    @pl.when(c == last)
    def _():
        cstate_ref[...] = tail

    log_gamma = [math.log1p(-(2.0 ** (-5.0 - h))) for h in range(n_heads)]

    @pl.when((pl.program_id(0) == 0) & (c == 0))
    def _():
        rel = (lax.broadcasted_iota(jnp.int32, (blk, blk), 0)
               - lax.broadcasted_iota(jnp.int32, (blk, blk), 1)).astype(F32)
        ii = lax.broadcasted_iota(jnp.int32, (blk, dk), 0).astype(F32)
        for h, lg in enumerate(log_gamma):
            dmat_ref[h] = jnp.where(rel >= 0, jnp.exp(jnp.maximum(rel, 0.0) * lg), 0.0)
            qdec_ref[h] = jnp.exp((ii + 1.0) * lg)
            kdec_ref[h] = jnp.exp((blk - 1.0 - ii) * lg)

    stage = []
    for h in range(n_heads):
        qr = pb_ref[:, h * dk:(h + 1) * dk]
        kr = pb_ref[:, k0 + h * dk:k0 + (h + 1) * dk]
        vb = pb_ref[:, v0 + h * dv:v0 + (h + 1) * dv]
        sc = lax.dot_general(qr, kr, (((1,), (1,)), ((), ())), preferred_element_type=F32)
        st = s_ref[h]
        inter = jnp.dot((qr.astype(F32) * qdec_ref[h]).astype(BF16), st.astype(BF16),
                        preferred_element_type=F32)
        kv = lax.dot_general((kr.astype(F32) * kdec_ref[h]).astype(BF16), vb, (((0,), (0,)), ((), ())),
                             preferred_element_type=F32)
        s_ref[h] = math.exp(blk * log_gamma[h]) * st + kv
        stage.append((sc, inter, vb))
    for h, (sc, inter, vb) in enumerate(stage):
        o = jnp.dot((sc * dmat_ref[h]).astype(BF16), vb, preferred_element_type=F32) + inter
        on = o * lax.rsqrt(jnp.mean(o * o, axis=-1, keepdims=True) + EPS) * gn_ref[h:h + 1, :]
        gate = pb_ref[:, g0 + h * dv:g0 + (h + 1) * dv].astype(F32)
        mix_ref[:, d_a + h * dv:d_a + (h + 1) * dv] = (on * gate).astype(BF16)

    @pl.when(c == last)
    def _():
        rout_ref[...] = s_ref[...]


def even_mixer(p32, p16, conv_hist, ret_state, w_conv, gn, *, blk):
    b, t, w32 = p32.shape
    w16 = p16.shape[-1]
    n_heads, dk, dv = ret_state.shape[1:]
    d_a = w_conv.shape[1]
    hv = n_heads * dv
    assert w32 == 3 * d_a and w16 == 2 * n_heads * dk + 2 * hv
    blk = _row_tile(t, blk)
    nc = t // blk
    kern = functools.partial(_even_mixer_kernel, n_heads=n_heads, dk=dk, dv=dv, d_a=d_a)
    return pl.pallas_call(
        kern,
        grid=(b, nc),
        in_specs=[
            pl.BlockSpec((None, blk, w32), lambda bi, ci: (bi, ci, 0)),
            pl.BlockSpec((None, blk, w16), lambda bi, ci: (bi, ci, 0)),
            pl.BlockSpec((None, 2, d_a), lambda bi, ci: (bi, 0, 0)),
            pl.BlockSpec((None, n_heads, dk, dv), lambda bi, ci: (bi, 0, 0, 0)),
            pl.BlockSpec((3, d_a), lambda bi, ci: (0, 0)),
            pl.BlockSpec((n_heads, dv), lambda bi, ci: (0, 0)),
        ],
        out_specs=[
            pl.BlockSpec((None, blk, d_a + hv), lambda bi, ci: (bi, ci, 0)),
            pl.BlockSpec((None, 2, d_a), lambda bi, ci: (bi, 0, 0)),
            pl.BlockSpec((None, n_heads, dk, dv), lambda bi, ci: (bi, 0, 0, 0)),
        ],
        out_shape=[
            jax.ShapeDtypeStruct((b, t, d_a + hv), BF16),
            jax.ShapeDtypeStruct((b, 2, d_a), F32),
            jax.ShapeDtypeStruct((b, n_heads, dk, dv), F32),
        ],
        scratch_shapes=[pltpu.VMEM((2, d_a), F32), pltpu.VMEM((n_heads, dk, dv), F32),
                        pltpu.VMEM((n_heads, blk, blk), F32), pltpu.VMEM((n_heads, blk, dk), F32),
                        pltpu.VMEM((n_heads, blk, dk), F32)],
        compiler_params=_params(),
        name="even_mixer",
    )(p32, p16, conv_hist, ret_state, w_conv, gn)


def _odd_mixer_kernel(sinks_ref, p_ref, cos_ref, sin_ref,
                      phist_ref, ck_ref, cv_ref, wpool_ref, pscale_ref,
                      mix_ref, pstate_ref, kstate_ref, vstate_ref,
                      ext_ref, kbuf_ref, vbuf_ref,
                      *, d_c, n_q_heads, n_kv_heads, hd, pos0, has_cache, n_state_rows):
    c = pl.program_id(1)
    last = pl.num_programs(1) - 1
    blk = p_ref.shape[0]
    hist_rows = POOL_HIST + 1
    d_cg = d_c // len(POOL_WINDOWS)
    lanes = 2 * hd
    q0 = d_c
    k0 = q0 + n_q_heads * hd
    v0 = k0 + lanes

    @pl.when(c == 0)
    def _():
        ext_ref[0:1, :] = jnp.zeros((1, d_c), F32)
        ext_ref[1:hist_rows, :] = phist_ref[...]
        kbuf_ref[0:WINDOW, :] = ck_ref[...]
        vbuf_ref[0:WINDOW, :] = cv_ref[...]

    u = p_ref[:, 0:d_c]
    ext_ref[hist_rows:hist_rows + blk, :] = u
    pos = pos0 + c * blk + lax.broadcasted_iota(jnp.int32, (blk, 1), 0)
    for gi, win in enumerate(POOL_WINDOWS):
        cols = slice(gi * d_cg, (gi + 1) * d_cg)
        s = ext_ref[:, cols]
        span = 1
        while span < win:
            s = s + pltpu.roll(s, span, axis=0)
            span *= 2
        cnt = jnp.minimum(pos + 1, win).astype(F32)
        d = s[hist_rows:, :] / cnt - u[:, cols]
        y = jnp.dot(d.astype(BF16), wpool_ref[gi], preferred_element_type=F32)
        mix_ref[:, cols] = (y * pscale_ref[:, cols]).astype(BF16)
    tail = ext_ref[blk:blk + hist_rows, :]
    ext_ref[0:hist_rows, :] = tail

    @pl.when(c == last)
    def _():
        pstate_ref[...] = ext_ref[1:hist_rows, :]

    cos4 = cos_ref[...]
    sin4 = sin_ref[...]
    lane = lax.broadcasted_iota(jnp.int32, (1, lanes), 1)
    first_half = jnp.bitwise_and(lane, hd - 1) < (hd // 2)
    left = lane < hd

    def rope(x):
        rot = jnp.where(first_half, pltpu.roll(x, lanes - hd // 2, axis=1), pltpu.roll(x, hd // 2, axis=1))
        return x * cos4 + rot * sin4

    kr = rope(p_ref[:, k0:k0 + lanes])
    vv = p_ref[:, v0:v0 + lanes]
    kbuf_ref[WINDOW:WINDOW + blk, :] = kr
    vbuf_ref[WINDOW:WINDOW + blk, :] = vv
    kall = kbuf_ref[...]
    vall = vbuf_ref[...]
    kswap = pltpu.roll(kall, hd, axis=1)
    vswap = pltpu.roll(vall, hd, axis=1)
    nk = WINDOW + blk
    chunk_shift = CHUNK.bit_length() - 1
    qi = jnp.right_shift(lax.broadcasted_iota(jnp.int32, (blk, 1), 0), chunk_shift)
    kj = lax.broadcasted_iota(jnp.int32, (1, nk), 1)
    kc = jnp.right_shift(kj, chunk_shift) - WINDOW // CHUNK
    ok = (kc <= qi) & (kc >= qi - WINDOW // CHUNK)
    if not has_cache:
        ok = ok & ((kj >= WINDOW) | (c > 0))
    scale = hd ** -0.5
    group = n_q_heads // n_kv_heads
    zeros = jnp.zeros_like(vall)
    kv_ops = [
        (jnp.where(left, kall, kswap).astype(BF16), jnp.where(left, vall, zeros).astype(BF16),
         jnp.where(left, zeros, vswap).astype(BF16)),
        (jnp.where(left, kswap, kall).astype(BF16), jnp.where(left, vswap, zeros).astype(BF16),
         jnp.where(left, zeros, vall).astype(BF16)),
    ]
    scores = []
    for head in range(n_q_heads):
        kvh = head // group
        if head % 2 == 0:
            qr = rope(p_ref[:, q0 + (head // 2) * lanes:q0 + (head // 2 + 1) * lanes])
        qh = jnp.where(left if head % 2 == 0 else jnp.logical_not(left), qr, 0.0).astype(BF16)
        scores.append(lax.dot_general(qh, kv_ops[kvh][0], (((1,), (1,)), ((), ())),
                                      preferred_element_type=F32))
    probs = []
    for head, sc in enumerate(scores):
        sc = jnp.where(ok, sc * scale, NEG_INF)
        sink = sinks_ref[head]
        m = jnp.maximum(jnp.max(sc, axis=-1, keepdims=True), sink)
        e = jnp.exp(sc - m)
        den = jnp.sum(e, axis=-1, keepdims=True) + jnp.exp(sink - m)
        probs.append((e / den).astype(BF16))
    for pair in range(n_q_heads // 2):
        kvh = (2 * pair) // group
        assert (2 * pair + 1) // group == kvh
        out = (jnp.dot(probs[2 * pair], kv_ops[kvh][1], preferred_element_type=F32)
               + jnp.dot(probs[2 * pair + 1], kv_ops[kvh][2], preferred_element_type=F32))
        mix_ref[:, d_c + pair * lanes:d_c + (pair + 1) * lanes] = out.astype(BF16)
    ktail = kbuf_ref[blk:blk + WINDOW, :]
    vtail = vbuf_ref[blk:blk + WINDOW, :]
    kbuf_ref[0:WINDOW, :] = ktail
    vbuf_ref[0:WINDOW, :] = vtail

    @pl.when(c == last)
    def _():
        kstate_ref[...] = kbuf_ref[nk - n_state_rows:nk, :]
        vstate_ref[...] = vbuf_ref[nk - n_state_rows:nk, :]


def odd_mixer(p, cos4, sin4, pool_hist, cache_k, cache_v, w_pool, pool_scale, sinks, *,
              blk, pos0, has_cache, n_state_rows):
    b, t, width = p.shape
    d_c = pool_scale.shape[-1]
    n_q_heads = sinks.shape[-1]
    lanes = cache_k.shape[-1]
    n_kv_heads = 2
    hd = lanes // n_kv_heads
    d_q = n_q_heads * hd
    assert d_q == d_c and d_c % lanes == 0 and width == d_c + d_q + 2 * lanes
    blk = _row_tile(t, blk)
    nc = t // blk
    assert blk % CHUNK == 0 or nc == 1
    kern = functools.partial(_odd_mixer_kernel, d_c=d_c, n_q_heads=n_q_heads, n_kv_heads=n_kv_heads,
                             hd=hd, pos0=pos0, has_cache=has_cache, n_state_rows=n_state_rows)
    return pl.pallas_call(
        kern,
        grid=(b, nc),
        in_specs=[
            pl.BlockSpec(memory_space=pltpu.SMEM),
            pl.BlockSpec((None, blk, width), lambda bi, ci: (bi, ci, 0)),
            pl.BlockSpec((blk, lanes), lambda bi, ci: (ci, 0)),
            pl.BlockSpec((blk, lanes), lambda bi, ci: (ci, 0)),
            pl.BlockSpec((None, POOL_HIST, d_c), lambda bi, ci: (bi, 0, 0)),
            pl.BlockSpec((None, WINDOW, lanes), lambda bi, ci: (bi, 0, 0)),
            pl.BlockSpec((None, WINDOW, lanes), lambda bi, ci: (bi, 0, 0)),
            pl.BlockSpec(w_pool.shape, lambda bi, ci: (0, 0, 0)),
            pl.BlockSpec((1, d_c), lambda bi, ci: (0, 0)),
        ],
        out_specs=[
            pl.BlockSpec((None, blk, d_c + d_q), lambda bi, ci: (bi, ci, 0)),
            pl.BlockSpec((None, POOL_HIST, d_c), lambda bi, ci: (bi, 0, 0)),
            pl.BlockSpec((None, n_state_rows, lanes), lambda bi, ci: (bi, 0, 0)),
            pl.BlockSpec((None, n_state_rows, lanes), lambda bi, ci: (bi, 0, 0)),
        ],
        out_shape=[
            jax.ShapeDtypeStruct((b, t, d_c + d_q), BF16),
            jax.ShapeDtypeStruct((b, POOL_HIST, d_c), F32),
            jax.ShapeDtypeStruct((b, n_state_rows, lanes), F32),
            jax.ShapeDtypeStruct((b, n_state_rows, lanes), F32),
        ],
        scratch_shapes=[
            pltpu.VMEM((POOL_HIST + 1 + blk, d_c), F32),
            pltpu.VMEM((WINDOW + blk, lanes), F32),
            pltpu.VMEM((WINDOW + blk, lanes), F32),
        ],
        compiler_params=_params(),
        name="odd_mixer",
    )(sinks, p, cos4, sin4, pool_hist, cache_k, cache_v, w_pool, pool_scale.reshape(1, d_c))


def _accumulate(o_ref, part, k):
    @pl.when(k == 0)
    def _():
        o_ref[...] = part

    @pl.when(k > 0)
    def _():
        o_ref[...] += part


def _out_proj_kernel(mix_ref, w_ref, x_ref, g_ref, o_ref, *rest, n_k):
    copy_ref = rest[0] if rest else None
    k = pl.program_id(1)
    part = jnp.dot(mix_ref[...], _load_weight(w_ref, copy_ref), preferred_element_type=F32)
    if n_k == 1:
        o_ref[...] = x_ref[...] + _rms(part, g_ref[...])
    else:
        _accumulate(o_ref, part, k)

        @pl.when(k == n_k - 1)
        def _():
            o_ref[...] = x_ref[...] + _rms(o_ref[...], g_ref[...])


def out_proj(mix, w, x, g, *, tm, tk=None):
    rows, kdim = mix.shape
    d = x.shape[1]
    tm = _row_tile(rows, tm)
    tk = kdim if tk is None else tk
    assert kdim % tk == 0
    w_arg, w_spec, copy_spec, copy_shape = _weight_in(w, (tk, d), lambda i, k: (k, 0))
    emit = copy_spec is not None
    assert not emit or rows == tm, "a bf16 copy needs every weight tile visited exactly once"
    if not emit and tk == kdim:
        w_spec = _resident_spec((tk, d), lambda i, k: (0, 0))
    out_specs = [pl.BlockSpec((tm, d), lambda i, k: (i, 0))]
    out_shape = [jax.ShapeDtypeStruct((rows, d), F32)]
    if emit:
        out_specs.append(copy_spec)
        out_shape.append(copy_shape)
    out = pl.pallas_call(
        functools.partial(_out_proj_kernel, n_k=kdim // tk),
        grid=(rows // tm, kdim // tk),
        in_specs=[
            pl.BlockSpec((tm, tk), lambda i, k: (i, k)),
            w_spec,
            pl.BlockSpec((tm, d), lambda i, k: (i, 0)),
            pl.BlockSpec((1, d), lambda i, k: (0, 0)),
        ],
        out_specs=out_specs,
        out_shape=out_shape,
        compiler_params=_params(),
        name="out_proj",
    )(mix, w_arg, x, g.reshape(1, d))
    return (out[0], out[1]) if emit else (out[0], w)


def _ffn_kernel(x_ref, gpre_ref, wa_ref, wg_ref, wconv_ref, wd_ref, gpost_ref, hist_ref,
                o_ref, state_ref, *rest, n_seq, blocks_per_seq, sub, emit):
    if emit:
        wa_copy_ref, wg_copy_ref, wd_copy_ref, h_ref, carry_ref = rest
    else:
        wa_copy_ref = wg_copy_ref = wd_copy_ref = None
        h_ref, carry_ref = rest
    i = pl.program_id(0)
    j = pl.program_id(1)
    last_j = pl.num_programs(1) - 1
    tm = x_ref.shape[0]
    tf = wa_ref.shape[1]
    rows_per_seq = tm // n_seq

    @pl.when(j == 0)
    def _():
        def start(rows):
            x = x_ref[rows, :]
            h_ref[rows, :] = _rms(x, gpre_ref[...]).astype(BF16)
            o_ref[rows, :] = jnp.zeros_like(x)

        _for_row_chunks(tm, start)

    if blocks_per_seq > 1:
        @pl.when(i % blocks_per_seq == 0)
        def _():
            carry_ref[j] = hist_ref[0]

    h = h_ref[...]
    assert rows_per_seq & (rows_per_seq - 1) == 0
    t = jnp.bitwise_and(lax.broadcasted_iota(jnp.int32, (tm, 1), 0), rows_per_seq - 1)
    groups = [slice(s * sub, (s + 1) * sub) for s in range(tf // sub)]
    if emit:
        wa, wg, wd = (_load_weight(wa_ref, wa_copy_ref), _load_weight(wg_ref, wg_copy_ref),
                      _load_weight(wd_ref, wd_copy_ref))
    else:
        wa, wg, wd = wa_ref, wg_ref, wd_ref
    pending = None
    for cols in groups:
        a = jnp.dot(h, wa[:, cols], preferred_element_type=F32)
        val = jnp.dot(h, wg[:, cols], preferred_element_type=F32)
        if pending is not None:
            o_ref[...] += jnp.dot(pending[0], wd[pending[1], :], preferred_element_type=F32)
        if blocks_per_seq > 1:
            hist0 = carry_ref[j, 0:1, cols]
            hist1 = carry_ref[j, 1:2, cols]
        elif n_seq == 1:
            hist0 = hist_ref[0, 0:1, cols]
            hist1 = hist_ref[0, 1:2, cols]
        else:
            hist = hist_ref[:, :, cols]
            hist0 = jnp.broadcast_to(hist[:, 0:1, :], (n_seq, rows_per_seq, sub)).reshape(tm, sub)
            hist1 = jnp.broadcast_to(hist[:, 1:2, :], (n_seq, rows_per_seq, sub)).reshape(tm, sub)
        prev1, prev2 = pltpu.roll(a, 1, axis=0), pltpu.roll(a, 2, axis=0)
        if n_seq == 1:
            t8 = t[0:SUBLANES]
            top1 = jnp.where(t8 == 0, hist1, prev1[0:SUBLANES])
            top2 = jnp.where(t8 == 0, hist0, jnp.where(t8 == 1, hist1, prev2[0:SUBLANES]))
            prev1 = jnp.concatenate([top1, prev1[SUBLANES:]], axis=0)
            prev2 = jnp.concatenate([top2, prev2[SUBLANES:]], axis=0)
        else:
            prev1 = jnp.where(t == 0, hist1, prev1)
            prev2 = jnp.where(t == 0, hist0, jnp.where(t == 1, hist1, prev2))
        w = wconv_ref[:, cols]
        conv = prev2 * w[0:1] + prev1 * w[1:2] + a * w[2:3]
        pending = ((_gelu_tanh(conv) * val).astype(BF16), cols)
        tail = a.reshape(n_seq, rows_per_seq, sub)[:, rows_per_seq - 2:, :]
        state_ref[j, :, :, cols] = tail
        if blocks_per_seq > 1:
            carry_ref[j, :, cols] = tail[0]
    o_ref[...] += jnp.dot(pending[0], wd[pending[1], :], preferred_element_type=F32)

    @pl.when(j == last_j)
    def _():
        def finish(rows):
            o_ref[rows, :] = x_ref[rows, :] + _rms(o_ref[rows, :], gpost_ref[...])

        _for_row_chunks(tm, finish)


def conv_ffn(x, g_pre, w_up, w_conv, w_down, g_post, hist, *, seq_len, tm, tf):
    rows, d = x.shape
    d_ff = w_down.shape[0]
    tm = _row_tile(rows, tm)
    assert d_ff % tf == 0 and tf % MXU_COLS == 0
    n_ff = d_ff // tf
    emit = isinstance(w_up, LayerWeight)
    assert emit == isinstance(w_down, LayerWeight)
    if emit:
        assert rows == tm, "a bf16 copy needs every weight tile visited exactly once"
        half = jax.ShapeDtypeStruct((d, d_ff), BF16)
        up_args = (w_up.stack, w_up.stack)
        up_specs = [w_up.spec((d, tf), lambda i, j: (0, j)), w_up.spec((d, tf), lambda i, j: (0, j + n_ff))]
        copy_specs = [pl.BlockSpec((d, tf), lambda i, j: (0, j)), pl.BlockSpec((d, tf), lambda i, j: (0, j))]
        copy_shapes = [half, half]
    else:
        up_args = tuple(w_up)
        up_specs = [pl.BlockSpec((d, tf), lambda i, j: (0, j)), pl.BlockSpec((d, tf), lambda i, j: (0, j))]
        copy_specs, copy_shapes = [], []
    wd_arg, wd_spec, wd_copy_spec, wd_copy_shape = _weight_in(w_down, (tf, d), lambda i, j: (j, 0))
    if emit:
        copy_specs.append(wd_copy_spec)
        copy_shapes.append(wd_copy_shape)
    if seq_len >= tm:
        assert seq_len % tm == 0
        n_seq, blocks_per_seq = 1, seq_len // tm
        seq_of = lambda i: i // blocks_per_seq
    else:
        assert tm % seq_len == 0
        n_seq, blocks_per_seq = tm // seq_len, 1
        seq_of = lambda i: i
    kern = functools.partial(_ffn_kernel, n_seq=n_seq, blocks_per_seq=blocks_per_seq, sub=MXU_COLS, emit=emit)
    out = pl.pallas_call(
        kern,
        grid=(rows // tm, n_ff),
        in_specs=[
            (_resident_spec if rows == tm else pl.BlockSpec)((tm, d), lambda i, j: (i, 0)),
            pl.BlockSpec((1, d), lambda i, j: (0, 0)),
            up_specs[0],
            up_specs[1],
            pl.BlockSpec((3, tf), lambda i, j: (0, j)),
            wd_spec,
            pl.BlockSpec((1, d), lambda i, j: (0, 0)),
            pl.BlockSpec((n_seq, 2, tf), lambda i, j: (seq_of(i), 0, j)),
        ],
        out_specs=[
            pl.BlockSpec((tm, d), lambda i, j: (i, 0)),
            pl.BlockSpec((n_ff, n_seq, 2, tf), lambda i, j: (0, seq_of(i), 0, 0)),
        ] + copy_specs,
        out_shape=[
            jax.ShapeDtypeStruct((rows, d), F32),
            jax.ShapeDtypeStruct((n_ff, rows // seq_len, 2, tf), F32),
        ] + copy_shapes,
        scratch_shapes=[pltpu.VMEM((tm, d), BF16), pltpu.VMEM((n_ff, 2, tf), F32)],
        compiler_params=_params(),
        name="conv_ffn",
    )(x, g_pre.reshape(1, d), *up_args, w_conv, wd_arg, g_post.reshape(1, d), hist)
    state = jnp.moveaxis(out[1], 0, 2).reshape(rows // seq_len, 2, d_ff)
    weights = ((out[2], out[3]), out[4]) if emit else (tuple(w_up), w_down)
    return out[0], state, weights


def _ple_kernel(x_ref, p_ref, gpre_ref, wgate_ref, wproj_ref, gpost_ref, o_ref, *rest, n_k, emit):
    wgate_copy_ref, wproj_copy_ref = rest[:2] if emit else (None, None)
    h_ref = rest[-1] if n_k > 1 else None
    k = pl.program_id(1)
    tk = wgate_ref.shape[0]

    def finish(pre):
        emb = jnp.dot(p_ref[...].astype(BF16), _load_weight(wproj_ref, wproj_copy_ref),
                      preferred_element_type=F32)
        o_ref[...] = x_ref[...] + _rms(jax.nn.sigmoid(pre) * emb, gpost_ref[...])

    if n_k == 1:
        h = _rms(x_ref[...], gpre_ref[...]).astype(BF16)
        finish(jnp.dot(h, _load_weight(wgate_ref, wgate_copy_ref), preferred_element_type=F32))
    else:
        @pl.when(k == 0)
        def _():
            h = _rms(x_ref[...], gpre_ref[...]).astype(BF16)
            for kk in range(n_k):
                h_ref[kk] = h[:, kk * tk:(kk + 1) * tk]

        _accumulate(o_ref, jnp.dot(h_ref[k], _load_weight(wgate_ref, wgate_copy_ref),
                                   preferred_element_type=F32), k)

        @pl.when(k == n_k - 1)
        def _():
            finish(o_ref[...])


def per_layer_embedding(x, p, layer, g_pre, w_gate, w_proj, g_post, *, tm, tk=None):
    rows, d = x.shape
    pd = p.shape[-1]
    tm = _row_tile(rows, tm)
    tk = d if tk is None else tk
    assert d % tk == 0
    n_k = d // tk
    wg_arg, wg_spec, wg_copy_spec, wg_copy_shape = _weight_in(w_gate, (tk, d), lambda i, k: (k, 0))
    wp_arg, wp_spec, wp_copy_spec, wp_copy_shape = _weight_in(w_proj, (pd, d), lambda i, k: (0, 0))
    emit = wg_copy_spec is not None
    assert emit == (wp_copy_spec is not None)
    assert not emit or rows == tm, "a bf16 copy needs every weight tile visited exactly once"
    if not emit:
        wp_spec = _resident_spec((pd, d), lambda i, k: (0, 0))
        if n_k == 1:
            wg_spec = _resident_spec((tk, d), lambda i, k: (0, 0))
    out_specs = [pl.BlockSpec((tm, d), lambda i, k: (i, 0))]
    out_shape = [jax.ShapeDtypeStruct((rows, d), F32)]
    if emit:
        out_specs += [wg_copy_spec, wp_copy_spec]
        out_shape += [wg_copy_shape, wp_copy_shape]
    out = pl.pallas_call(
        functools.partial(_ple_kernel, n_k=n_k, emit=emit),
        grid=(rows // tm, n_k),
        in_specs=[
            pl.BlockSpec((tm, d), lambda i, k: (i, 0)),
            pl.BlockSpec((None, tm, pd), lambda i, k: (layer, i, 0)),
            pl.BlockSpec((1, d), lambda i, k: (0, 0)),
            wg_spec,
            wp_spec,
            pl.BlockSpec((1, d), lambda i, k: (0, 0)),
        ],
        out_specs=out_specs,
        out_shape=out_shape,
        scratch_shapes=[pltpu.VMEM((n_k, tm, tk), BF16)] if n_k > 1 else [],
        compiler_params=_params(),
        name="per_layer_embedding",
    )(x, p, g_pre.reshape(1, d), wg_arg, wp_arg, g_post.reshape(1, d))
    return (out[0], (out[1], out[2])) if emit else (out[0], (w_gate, w_proj))


def _rope_tables(pos, dim, reps):
    inv = 1.0 / (ROPE_THETA ** (jnp.arange(0, dim, 2, dtype=F32) / dim))
    ang = pos.astype(F32)[:, None] * inv[None, :]
    cos, sin = jnp.cos(ang), jnp.sin(ang)
    return (jnp.tile(jnp.concatenate([cos, cos], axis=-1), (1, reps)),
            jnp.tile(jnp.concatenate([-sin, sin], axis=-1), (1, reps)))


def _layer(i, x, p, grp, states, wts, prm, cfg):
    b, t = grp['b'], grp['t']
    rows = b * t
    j = i // 2
    wq = {}
    if i % 2 == 0:
        n_heads, dk, dv = states['ret'].shape[2:]
        p32, p16, wq['w_in'] = even_in_proj(
            x, prm['norm_mix_pre'][i], wts['w_in'], *grp['even_tabs'], tm=cfg['tm_in'], tn=cfg['tn_even'],
            d_a=prm['w_conv_a'].shape[-1], n_heads=n_heads, dk=dk, dv=dv)
        mix, c_s, r_s = even_mixer(p32.reshape(b, t, -1), p16.reshape(b, t, -1), states['conv'][j], states['ret'][j],
                                   prm['w_conv_a'][j], prm['ret_gn'][j], blk=cfg['blk_even'])
        new = dict(conv=c_s, ret=r_s)
    else:
        proj, wq['w_in'] = norm_matmul(x, prm['norm_mix_pre'][i], wts['w_in'], tm=cfg['tm_in'], tn=cfg['tn_odd'])
        lanes = grp['kv_lanes']
        has_cache = states['cache_k'] is not None
        if has_cache:
            ck = states['cache_k'][j].reshape(b, WINDOW, lanes)
            cv = states['cache_v'][j].reshape(b, WINDOW, lanes)
            n_state_rows = t
        else:
            ck = cv = jnp.zeros((b, WINDOW, lanes), F32)
            n_state_rows = cfg['kv_rows']
        mix, p_s, k_s, v_s = odd_mixer(proj.reshape(b, t, -1), *grp['odd_tabs'], states['pool'][j], ck, cv,
                                       prm['w_pool'][j], prm['pool_scale'][j], prm['sinks'][j],
                                       blk=cfg['blk_odd'], pos0=grp['pos0'], has_cache=has_cache,
                                       n_state_rows=n_state_rows)
        kv_shape = (b, n_state_rows, 2, lanes // 2)
        new = dict(pool=p_s, k=k_s.reshape(kv_shape), v=v_s.reshape(kv_shape))
    x, wq['w_out'] = out_proj(mix.reshape(rows, -1), wts['w_out'], x, prm['norm_mix_post'][i],
                              tm=cfg['tm_out'], tk=cfg['tk_out'])
    x, new['ffn'], (wq['w_up'], wq['w_down']) = conv_ffn(
        x, prm['norm_ffn_pre'][i], wts['w_up'], prm['w_conv_ffn'][i], wts['w_down'], prm['norm_ffn_post'][i],
        states['ffn'][i], seq_len=t, tm=cfg['tm_ffn'], tf=cfg['tf'])
    x, (wq['w_gate'], wq['w_proj']) = per_layer_embedding(
        x, p, i, prm['norm_ple_pre'][i], wts['w_gate'], wts['w_proj'], prm['norm_ple_post'][i],
        tm=cfg['tm_ple'], tk=cfg['tk_ple'])
    return x, new, wq


def kernel(x_prompt, x_sample, state_conv, state_ret, state_pool, cache_k, cache_v, state_ffn, p_prompt, p_sample, norm_mix_pre, norm_mix_post, norm_ffn_pre, norm_ffn_post, norm_ple_pre, norm_ple_post, w_in_even, w_conv_a, ret_gn, w_out_even, w_in_odd, w_pool, pool_scale, sinks, w_out_odd, w_up, w_conv_ffn, w_down, w_ple_gate, w_ple_proj):
    depth = norm_mix_pre.shape[0]
    n_even, n_odd = w_in_even.shape[0], w_in_odd.shape[0]
    d = x_prompt.shape[-1]
    d_a = w_conv_a.shape[-1]
    n_heads, dk, dv = state_ret.shape[2:]
    d_c = pool_scale.shape[-1]
    kv_rows, n_kv, hd = cache_k.shape[2:]
    d_ff = w_conv_ffn.shape[-1]
    lanes = n_kv * hd
    dt = x_prompt.dtype

    prm = dict(norm_mix_pre=norm_mix_pre, norm_mix_post=norm_mix_post, norm_ffn_pre=norm_ffn_pre,
               norm_ffn_post=norm_ffn_post, norm_ple_pre=norm_ple_pre, norm_ple_post=norm_ple_post,
               w_conv_a=w_conv_a, ret_gn=ret_gn, w_pool=w_pool.astype(BF16), pool_scale=pool_scale, sinks=sinks,
               w_conv_ffn=w_conv_ffn)

    def group(x, pos0):
        b, t = x.shape[:2]
        pos = pos0 + jnp.arange(t, dtype=jnp.int32)
        return dict(b=b, t=t, pos0=pos0, kv_lanes=lanes,
                    even_tabs=tuple(jnp.tile(tab, (b, 1)) for tab in _rope_tables(pos, dk, 1)),
                    odd_tabs=_rope_tables(pos, hd, lanes // hd))

    bp, sp = x_prompt.shape[:2]
    bs, ts = x_sample.shape[:2]
    grp_p, grp_s = group(x_prompt, 0), group(x_sample, PAST_LEN)
    st_p = dict(conv=jnp.zeros((n_even, bp, 2, d_a), dt), ret=jnp.zeros((n_even, bp, n_heads, dk, dv), dt),
                pool=jnp.zeros((n_odd, bp, POOL_HIST, d_c), dt), cache_k=None, cache_v=None,
                ffn=jnp.zeros((depth, bp, 2, d_ff), dt))
    st_s = dict(conv=state_conv, ret=state_ret, pool=state_pool, cache_k=cache_k, cache_v=cache_v, ffn=state_ffn)
    cfg_s = dict(tm_in=bs * ts, tn_even=1024, tn_odd=w_in_odd.shape[-1] // 2, blk_even=ts, blk_odd=ts,
                 tm_out=bs * ts, tk_out=512, tm_ffn=bs * ts, tf=512, tm_ple=bs * ts, tk_ple=512, kv_rows=kv_rows)
    cfg_p = dict(tm_in=1024, tn_even=1024, tn_odd=w_in_odd.shape[-1] // 2, blk_even=256, blk_odd=128,
                 tm_out=512, tk_out=None, tm_ffn=1024, tf=512, tm_ple=1024, tk_ple=None, kv_rows=kv_rows)
    xp = x_prompt.reshape(bp * sp, d)
    xs = x_sample.reshape(bs * ts, d)
    pp = p_prompt.reshape(depth, bp * sp, -1)
    ps = p_sample.reshape(depth, bs * ts, -1)
    new_p, new_s = [], []
    for i in range(depth):
        j = i // 2
        raw = dict(w_in=LayerWeight(w_in_even if i % 2 == 0 else w_in_odd, j),
                   w_out=LayerWeight(w_out_even if i % 2 == 0 else w_out_odd, j),
                   w_up=LayerWeight(w_up, i), w_down=LayerWeight(w_down, i),
                   w_gate=LayerWeight(w_ple_gate, i), w_proj=LayerWeight(w_ple_proj, i))
        xs, st, wq = _layer(i, xs, ps, grp_s, st_s, raw, prm, cfg_s)
        new_s.append(st)
        xp, st, _ = _layer(i, xp, pp, grp_p, st_p, wq, prm, cfg_p)
        new_p.append(st)

    def stacked(new, key):
        parts = [st[key] for st in new if key in st]
        return parts[0][None] if len(parts) == 1 else jnp.stack(parts)

    outs = [xp.reshape(bp, sp, d), xs.reshape(bs, ts, d)]
    for key in ('conv', 'ret', 'pool', 'k', 'v', 'ffn'):
        outs += [stacked(new_p, key), stacked(new_s, key)]
    return tuple(outs)
```

```python
import functools
import math

import jax
import jax.numpy as jnp
from jax import lax
from jax.experimental import pallas as pl
from jax.experimental.pallas import tpu as pltpu

CHUNK = 64
WINDOW = 128
PAST_LEN = 4096
EPS = 1e-6
ROPE_THETA = 10000.0
NEG_INF = -1e30
POOL_WINDOWS = (2, 4, 8, 16)
POOL_HIST = max(POOL_WINDOWS) - 1

V7X_VMEM_BYTES = 64 * 1024 * 1024
VMEM_LIMIT_BYTES = 58 * 1024 * 1024
MXU_COLS = 256
SUBLANES = 8

F32 = jnp.float32
BF16 = jnp.bfloat16


def _params():
    return pltpu.CompilerParams(vmem_limit_bytes=VMEM_LIMIT_BYTES)


def _rms(x, g):
    return x * lax.rsqrt(jnp.mean(x * x, axis=-1, keepdims=True) + EPS) * g


def _gelu_tanh(x):
    c1 = math.sqrt(2.0 / math.pi)
    half = 0.5 * x
    return half + half * jnp.tanh(x * (c1 + (c1 * 0.044715) * (x * x)))


ROW_CHUNK = 256


def _for_row_chunks(n_rows, fn):
    if n_rows <= ROW_CHUNK or n_rows % ROW_CHUNK:
        fn(slice(0, n_rows))
        return

    def body(r, carry):
        fn(pl.ds(pl.multiple_of(r * ROW_CHUNK, ROW_CHUNK), ROW_CHUNK))
        return carry

    lax.fori_loop(0, n_rows // ROW_CHUNK, body, 0)


def _resident_spec(block, index_map):
    return pl.BlockSpec(block, index_map, pipeline_mode=pl.Buffered(1))


def _row_tile(rows, want):
    t = min(rows, want)
    assert rows % t == 0, (rows, t)
    return t


class LayerWeight:
    def __init__(self, stack, layer):
        self.stack, self.layer = stack, layer
        self.shape = stack.shape[1:]

    def spec(self, block, index_map):
        layer = self.layer
        return pl.BlockSpec((None,) + block, lambda *g: (layer,) + index_map(*g))


def _weight_in(w, block, index_map):
    if isinstance(w, LayerWeight):
        return (w.stack, w.spec(block, index_map), pl.BlockSpec(block, index_map),
                jax.ShapeDtypeStruct(w.shape, BF16))
    return w, pl.BlockSpec(block, index_map), None, None


def _load_weight(w_ref, copy_ref):
    w = w_ref[...]
    if copy_ref is not None:
        w = w.astype(BF16)
        copy_ref[...] = w
    return w


def _norm_matmul_kernel(x_ref, g_ref, w_ref, *rest, emit):
    o_ref, copy_ref, h_ref = rest if emit else (rest[0], None, rest[1])

    @pl.when(pl.program_id(1) == 0)
    def _():
        def start(rows):
            h_ref[rows, :] = _rms(x_ref[rows, :], g_ref[...]).astype(BF16)

        _for_row_chunks(x_ref.shape[0], start)

    w = _load_weight(w_ref, copy_ref)
    o_ref[...] = jnp.dot(h_ref[...], w, preferred_element_type=F32)


def norm_matmul(x, g, w, *, tm, tn):
    rows, d = x.shape
    n = w.shape[1]
    tm = _row_tile(rows, tm)
    assert n % tn == 0
    w_arg, w_spec, copy_spec, copy_shape = _weight_in(w, (d, tn), lambda i, j: (0, j))
    emit = copy_spec is not None
    assert not emit or rows == tm, "a bf16 copy needs every weight tile visited exactly once"
    out_specs = [pl.BlockSpec((tm, tn), lambda i, j: (i, j))]
    out_shape = [jax.ShapeDtypeStruct((rows, n), F32)]
    if emit:
        out_specs.append(copy_spec)
        out_shape.append(copy_shape)
    out = pl.pallas_call(
        functools.partial(_norm_matmul_kernel, emit=emit),
        grid=(rows // tm, n // tn),
        in_specs=[
            pl.BlockSpec((tm, d), lambda i, j: (i, 0)),
            pl.BlockSpec((1, d), lambda i, j: (0, 0)),
            w_spec,
        ],
        out_specs=out_specs,
        out_shape=out_shape,
        scratch_shapes=[pltpu.VMEM((tm, d), BF16)],
        compiler_params=_params(),
        name="norm_matmul",
    )(x, g.reshape(1, d), w_arg)
    return (out[0], out[1]) if emit else (out[0], w)


def _even_in_proj_kernel(x_ref, g_ref, w_ref, cos_ref, sin_ref, o_ref, *rest, emit, kinds, dk, k_scale):
    copy_ref, h_ref = rest if emit else (None, rest[0])
    j = pl.program_id(1)
    tn = w_ref.shape[1]

    @pl.when(j == 0)
    def _():
        def start(rows):
            h_ref[rows, :] = _rms(x_ref[rows, :], g_ref[...]).astype(BF16)

        _for_row_chunks(x_ref.shape[0], start)

    def project():
        return jnp.dot(h_ref[...], _load_weight(w_ref, copy_ref), preferred_element_type=F32)

    def rope(acc, scale):
        cosb, sinb = cos_ref[...], sin_ref[...]
        for hh in range(tn // dk):
            seg = acc[:, hh * dk:(hh + 1) * dk]
            out = seg * cosb + pltpu.roll(seg, dk // 2, axis=1) * sinb
            if scale != 1.0:
                out = out * scale
            o_ref[:, hh * dk:(hh + 1) * dk] = out.astype(BF16)

    def finish(kind):
        acc = project()
        if kind == 'rope':
            rope(acc, 1.0)
        elif kind == 'rope_scaled':
            rope(acc, k_scale)
        elif kind == 'plain':
            o_ref[...] = acc.astype(BF16)
        else:
            assert kind == 'silu'
            o_ref[...] = (acc * jax.nn.sigmoid(acc)).astype(BF16)

    for kind in dict.fromkeys(kinds):
        tiles = [t for t, k in enumerate(kinds) if k == kind]
        here = functools.reduce(jnp.logical_or, [j == t for t in tiles])
        pl.when(here)(functools.partial(finish, kind))


def even_in_proj(x, g, w, cosb, sinb, *, tm, tn, d_a, n_heads, dk, dv):
    rows, d = x.shape
    n = w.shape[1]
    tm = _row_tile(rows, tm)
    hk, hv = n_heads * dk, n_heads * dv
    assert n == 3 * d_a + 2 * hk + 2 * hv and tn % dk == 0
    assert all(part % tn == 0 for part in (d_a, hk, hv))
    kinds = (('plain',) * (3 * d_a // tn) + ('rope',) * (hk // tn) + ('rope_scaled',) * (hk // tn)
             + ('plain',) * (hv // tn) + ('silu',) * (hv // tn))
    w_arg, w_spec, copy_spec, copy_shape = _weight_in(w, (d, tn), lambda i, j: (0, j))
    emit = copy_spec is not None
    assert not emit or rows == tm, "a bf16 copy needs every weight tile visited exactly once"
    out_specs = [pl.BlockSpec((tm, tn), lambda i, j: (i, j))]
    out_shape = [jax.ShapeDtypeStruct((rows, n), BF16)]
    if emit:
        out_specs.append(copy_spec)
        out_shape.append(copy_shape)
    kern = functools.partial(_even_in_proj_kernel, emit=emit, kinds=kinds, dk=dk, k_scale=dk ** -0.5)
    out = pl.pallas_call(
        kern,
        grid=(rows // tm, n // tn),
        in_specs=[
            pl.BlockSpec((tm, d), lambda i, j: (i, 0)),
            pl.BlockSpec((1, d), lambda i, j: (0, 0)),
            w_spec,
            pl.BlockSpec((tm, dk), lambda i, j: (i, 0)),
            pl.BlockSpec((tm, dk), lambda i, j: (i, 0)),
        ],
        out_specs=out_specs,
        out_shape=out_shape,
        scratch_shapes=[pltpu.VMEM((tm, d), BF16)],
        compiler_params=_params(),
        name="even_in_proj",
    )(x, g.reshape(1, d), w_arg, cosb, sinb)
    return out[0], (out[1] if emit else w)


def _even_mixer_kernel(p_ref, chist_ref, rstate_ref, wconv_ref, gn_ref,
                       mix_ref, cstate_ref, rout_ref, carry_ref, s_ref, dmat_ref, qdec_ref, kdec_ref,
                       *, n_heads, dk, dv, d_a):
    q0 = 3 * d_a
    k0 = q0 + n_heads * dk
    v0 = k0 + n_heads * dk
    g0 = v0 + n_heads * dv
    c = pl.program_id(1)
    last = pl.num_programs(1) - 1
    blk = p_ref.shape[0]

    @pl.when(c == 0)
    def _():
        carry_ref[...] = chist_ref[...]
        s_ref[...] = rstate_ref[...]

    u = p_ref[:, 2 * d_a:3 * d_a].astype(F32) * p_ref[:, 0:d_a].astype(F32)
    row = lax.broadcasted_iota(jnp.int32, u.shape, 0)
    h0 = carry_ref[0:1, :]
    h1 = carry_ref[1:2, :]
    prev1 = jnp.where(row == 0, h1, pltpu.roll(u, 1, axis=0))
    prev2 = jnp.where(row == 0, h0, jnp.where(row == 1, h1, pltpu.roll(u, 2, axis=0)))
    w = wconv_ref[...]
    conv = prev2 * w[0:1] + prev1 * w[1:2] + u * w[2:3]
    mix_ref[:, 0:d_a] = (p_ref[:, d_a:2 * d_a].astype(F32) * conv).astype(BF16)
    tail = u[blk - 2:blk, :]
    carry_ref[...] = tail

    @pl.when(c == last)
    def _():
        cstate_ref[...] = tail

    log_gamma = [math.log1p(-(2.0 ** (-5.0 - h))) for h in range(n_heads)]

    @pl.when((pl.program_id(0) == 0) & (c == 0))
    def _():
        rel = (lax.broadcasted_iota(jnp.int32, (blk, blk), 0)
               - lax.broadcasted_iota(jnp.int32, (blk, blk), 1)).astype(F32)
        ii = lax.broadcasted_iota(jnp.int32, (blk, dk), 0).astype(F32)
        for h, lg in enumerate(log_gamma):
            dmat_ref[h] = jnp.where(rel >= 0, jnp.exp(jnp.maximum(rel, 0.0) * lg), 0.0)
            qdec_ref[h] = jnp.exp((ii + 1.0) * lg)
            kdec_ref[h] = jnp.exp((blk - 1.0 - ii) * lg)

    stage = []
    for h in range(n_heads):
        qr = p_ref[:, q0 + h * dk:q0 + (h + 1) * dk]
        kr = p_ref[:, k0 + h * dk:k0 + (h + 1) * dk]
        vb = p_ref[:, v0 + h * dv:v0 + (h + 1) * dv]
        sc = lax.dot_general(qr, kr, (((1,), (1,)), ((), ())), preferred_element_type=F32)
        st = s_ref[h]
        inter = jnp.dot((qr.astype(F32) * qdec_ref[h]).astype(BF16), st.astype(BF16),
                        preferred_element_type=F32)
        kv = lax.dot_general((kr.astype(F32) * kdec_ref[h]).astype(BF16), vb, (((0,), (0,)), ((), ())),
                             preferred_element_type=F32)
        s_ref[h] = math.exp(blk * log_gamma[h]) * st + kv
        stage.append((sc, inter, vb))
    for h, (sc, inter, vb) in enumerate(stage):
        o = jnp.dot((sc * dmat_ref[h]).astype(BF16), vb, preferred_element_type=F32) + inter
        on = o * lax.rsqrt(jnp.mean(o * o, axis=-1, keepdims=True) + EPS) * gn_ref[h:h + 1, :]
        gate = p_ref[:, g0 + h * dv:g0 + (h + 1) * dv].astype(F32)
        mix_ref[:, d_a + h * dv:d_a + (h + 1) * dv] = (on * gate).astype(BF16)

    @pl.when(c == last)
    def _():
        rout_ref[...] = s_ref[...]


def even_mixer(p, conv_hist, ret_state, w_conv, gn, *, blk):
    b, t, width = p.shape
    n_heads, dk, dv = ret_state.shape[1:]
    d_a = w_conv.shape[1]
    hv = n_heads * dv
    assert width == 3 * d_a + 2 * n_heads * dk + 2 * hv
    blk = _row_tile(t, blk)
    nc = t // blk
    kern = functools.partial(_even_mixer_kernel, n_heads=n_heads, dk=dk, dv=dv, d_a=d_a)
    return pl.pallas_call(
        kern,
        grid=(b, nc),
        in_specs=[
            pl.BlockSpec((None, blk, width), lambda bi, ci: (bi, ci, 0)),
            pl.BlockSpec((None, 2, d_a), lambda bi, ci: (bi, 0, 0)),
            pl.BlockSpec((None, n_heads, dk, dv), lambda bi, ci: (bi, 0, 0, 0)),
            pl.BlockSpec((3, d_a), lambda bi, ci: (0, 0)),
            pl.BlockSpec((n_heads, dv), lambda bi, ci: (0, 0)),
        ],
        out_specs=[
            pl.BlockSpec((None, blk, d_a + hv), lambda bi, ci: (bi, ci, 0)),
            pl.BlockSpec((None, 2, d_a), lambda bi, ci: (bi, 0, 0)),
            pl.BlockSpec((None, n_heads, dk, dv), lambda bi, ci: (bi, 0, 0, 0)),
        ],
        out_shape=[
            jax.ShapeDtypeStruct((b, t, d_a + hv), BF16),
            jax.ShapeDtypeStruct((b, 2, d_a), F32),
            jax.ShapeDtypeStruct((b, n_heads, dk, dv), F32),
        ],
        scratch_shapes=[pltpu.VMEM((2, d_a), F32), pltpu.VMEM((n_heads, dk, dv), F32),
                        pltpu.VMEM((n_heads, blk, blk), F32), pltpu.VMEM((n_heads, blk, dk), F32),
                        pltpu.VMEM((n_heads, blk, dk), F32)],
        compiler_params=_params(),
        name="even_mixer",
    )(p, conv_hist, ret_state, w_conv, gn)


def _odd_mixer_kernel(sinks_ref, p_ref, cos_ref, sin_ref,
                      phist_ref, ck_ref, cv_ref, wpool_ref, pscale_ref,
                      mix_ref, pstate_ref, kstate_ref, vstate_ref,
                      ext_ref, kbuf_ref, vbuf_ref,
                      *, d_c, n_q_heads, n_kv_heads, hd, pos0, has_cache, n_state_rows):
    c = pl.program_id(1)
    last = pl.num_programs(1) - 1
    blk = p_ref.shape[0]
    hist_rows = POOL_HIST + 1
    d_cg = d_c // len(POOL_WINDOWS)
    lanes = 2 * hd
    q0 = d_c
    k0 = q0 + n_q_heads * hd
    v0 = k0 + lanes

    @pl.when(c == 0)
    def _():
        ext_ref[0:1, :] = jnp.zeros((1, d_c), F32)
        ext_ref[1:hist_rows, :] = phist_ref[...]
        kbuf_ref[0:WINDOW, :] = ck_ref[...]
        vbuf_ref[0:WINDOW, :] = cv_ref[...]

    u = p_ref[:, 0:d_c]
    ext_ref[hist_rows:hist_rows + blk, :] = u
    pos = pos0 + c * blk + lax.broadcasted_iota(jnp.int32, (blk, 1), 0)
    for gi, win in enumerate(POOL_WINDOWS):
        cols = slice(gi * d_cg, (gi + 1) * d_cg)
        s = ext_ref[:, cols]
        span = 1
        while span < win:
            s = s + pltpu.roll(s, span, axis=0)
            span *= 2
        cnt = jnp.minimum(pos + 1, win).astype(F32)
        d = s[hist_rows:, :] / cnt - u[:, cols]
        y = jnp.dot(d.astype(BF16), wpool_ref[gi], preferred_element_type=F32)
        mix_ref[:, cols] = (y * pscale_ref[:, cols]).astype(BF16)
    tail = ext_ref[blk:blk + hist_rows, :]
    ext_ref[0:hist_rows, :] = tail

    @pl.when(c == last)
    def _():
        pstate_ref[...] = ext_ref[1:hist_rows, :]

    cos4 = cos_ref[...]
    sin4 = sin_ref[...]
    lane = lax.broadcasted_iota(jnp.int32, (1, lanes), 1)
    first_half = jnp.bitwise_and(lane, hd - 1) < (hd // 2)
    left = lane < hd

    def rope(x):
        rot = jnp.where(first_half, pltpu.roll(x, lanes - hd // 2, axis=1), pltpu.roll(x, hd // 2, axis=1))
        return x * cos4 + rot * sin4

    kr = rope(p_ref[:, k0:k0 + lanes])
    vv = p_ref[:, v0:v0 + lanes]
    kbuf_ref[WINDOW:WINDOW + blk, :] = kr
    vbuf_ref[WINDOW:WINDOW + blk, :] = vv
    kall = kbuf_ref[...]
    vall = vbuf_ref[...]
    kswap = pltpu.roll(kall, hd, axis=1)
    vswap = pltpu.roll(vall, hd, axis=1)
    nk = WINDOW + blk
    chunk_shift = CHUNK.bit_length() - 1
    qi = jnp.right_shift(lax.broadcasted_iota(jnp.int32, (blk, 1), 0), chunk_shift)
    kj = lax.broadcasted_iota(jnp.int32, (1, nk), 1)
    kc = jnp.right_shift(kj, chunk_shift) - WINDOW // CHUNK
    ok = (kc <= qi) & (kc >= qi - WINDOW // CHUNK)
    if not has_cache:
        ok = ok & ((kj >= WINDOW) | (c > 0))
    scale = hd ** -0.5
    group = n_q_heads // n_kv_heads
    zeros = jnp.zeros_like(vall)
    kv_ops = [
        (jnp.where(left, kall, kswap).astype(BF16), jnp.where(left, vall, zeros).astype(BF16),
         jnp.where(left, zeros, vswap).astype(BF16)),
        (jnp.where(left, kswap, kall).astype(BF16), jnp.where(left, vswap, zeros).astype(BF16),
         jnp.where(left, zeros, vall).astype(BF16)),
    ]
    scores = []
    for head in range(n_q_heads):
        kvh = head // group
        if head % 2 == 0:
            qr = rope(p_ref[:, q0 + (head // 2) * lanes:q0 + (head // 2 + 1) * lanes])
        qh = jnp.where(left if head % 2 == 0 else jnp.logical_not(left), qr, 0.0).astype(BF16)
        scores.append(lax.dot_general(qh, kv_ops[kvh][0], (((1,), (1,)), ((), ())),
                                      preferred_element_type=F32))
    probs = []
    for head, sc in enumerate(scores):
        sc = jnp.where(ok, sc * scale, NEG_INF)
        sink = sinks_ref[head]
        m = jnp.maximum(jnp.max(sc, axis=-1, keepdims=True), sink)
        e = jnp.exp(sc - m)
        den = jnp.sum(e, axis=-1, keepdims=True) + jnp.exp(sink - m)
        probs.append((e / den).astype(BF16))
    for pair in range(n_q_heads // 2):
        kvh = (2 * pair) // group
        assert (2 * pair + 1) // group == kvh
        out = (jnp.dot(probs[2 * pair], kv_ops[kvh][1], preferred_element_type=F32)
               + jnp.dot(probs[2 * pair + 1], kv_ops[kvh][2], preferred_element_type=F32))
        mix_ref[:, d_c + pair * lanes:d_c + (pair + 1) * lanes] = out.astype(BF16)
    ktail = kbuf_ref[blk:blk + WINDOW, :]
    vtail = vbuf_ref[blk:blk + WINDOW, :]
    kbuf_ref[0:WINDOW, :] = ktail
    vbuf_ref[0:WINDOW, :] = vtail

    @pl.when(c == last)
    def _():
        kstate_ref[...] = kbuf_ref[nk - n_state_rows:nk, :]
        vstate_ref[...] = vbuf_ref[nk - n_state_rows:nk, :]


def odd_mixer(p, cos4, sin4, pool_hist, cache_k, cache_v, w_pool, pool_scale, sinks, *,
              blk, pos0, has_cache, n_state_rows):
    b, t, width = p.shape
    d_c = pool_scale.shape[-1]
    n_q_heads = sinks.shape[-1]
    lanes = cache_k.shape[-1]
    n_kv_heads = 2
    hd = lanes // n_kv_heads
    d_q = n_q_heads * hd
    assert d_q == d_c and d_c % lanes == 0 and width == d_c + d_q + 2 * lanes
    blk = _row_tile(t, blk)
    nc = t // blk
    assert blk % CHUNK == 0 or nc == 1
    kern = functools.partial(_odd_mixer_kernel, d_c=d_c, n_q_heads=n_q_heads, n_kv_heads=n_kv_heads,
                             hd=hd, pos0=pos0, has_cache=has_cache, n_state_rows=n_state_rows)
    return pl.pallas_call(
        kern,
        grid=(b, nc),
        in_specs=[
            pl.BlockSpec(memory_space=pltpu.SMEM),
            pl.BlockSpec((None, blk, width), lambda bi, ci: (bi, ci, 0)),
            pl.BlockSpec((blk, lanes), lambda bi, ci: (ci, 0)),
            pl.BlockSpec((blk, lanes), lambda bi, ci: (ci, 0)),
            pl.BlockSpec((None, POOL_HIST, d_c), lambda bi, ci: (bi, 0, 0)),
            pl.BlockSpec((None, WINDOW, lanes), lambda bi, ci: (bi, 0, 0)),
            pl.BlockSpec((None, WINDOW, lanes), lambda bi, ci: (bi, 0, 0)),
            pl.BlockSpec(w_pool.shape, lambda bi, ci: (0, 0, 0)),
            pl.BlockSpec((1, d_c), lambda bi, ci: (0, 0)),
        ],
        out_specs=[
            pl.BlockSpec((None, blk, d_c + d_q), lambda bi, ci: (bi, ci, 0)),
            pl.BlockSpec((None, POOL_HIST, d_c), lambda bi, ci: (bi, 0, 0)),
            pl.BlockSpec((None, n_state_rows, lanes), lambda bi, ci: (bi, 0, 0)),
            pl.BlockSpec((None, n_state_rows, lanes), lambda bi, ci: (bi, 0, 0)),
        ],
        out_shape=[
            jax.ShapeDtypeStruct((b, t, d_c + d_q), BF16),
            jax.ShapeDtypeStruct((b, POOL_HIST, d_c), F32),
            jax.ShapeDtypeStruct((b, n_state_rows, lanes), F32),
            jax.ShapeDtypeStruct((b, n_state_rows, lanes), F32),
        ],
        scratch_shapes=[
            pltpu.VMEM((POOL_HIST + 1 + blk, d_c), F32),
            pltpu.VMEM((WINDOW + blk, lanes), F32),
            pltpu.VMEM((WINDOW + blk, lanes), F32),
        ],
        compiler_params=_params(),
        name="odd_mixer",
    )(sinks, p, cos4, sin4, pool_hist, cache_k, cache_v, w_pool, pool_scale.reshape(1, d_c))


def _accumulate(o_ref, part, k):
    @pl.when(k == 0)
    def _():
        o_ref[...] = part

    @pl.when(k > 0)
    def _():
        o_ref[...] += part


def _out_proj_kernel(mix_ref, w_ref, x_ref, g_ref, o_ref, *rest, n_k):
    copy_ref = rest[0] if rest else None
    k = pl.program_id(1)
    part = jnp.dot(mix_ref[...], _load_weight(w_ref, copy_ref), preferred_element_type=F32)
    if n_k == 1:
        o_ref[...] = x_ref[...] + _rms(part, g_ref[...])
    else:
        _accumulate(o_ref, part, k)

        @pl.when(k == n_k - 1)
        def _():
            o_ref[...] = x_ref[...] + _rms(o_ref[...], g_ref[...])


def out_proj(mix, w, x, g, *, tm, tk=None):
    rows, kdim = mix.shape
    d = x.shape[1]
    tm = _row_tile(rows, tm)
    tk = kdim if tk is None else tk
    assert kdim % tk == 0
    w_arg, w_spec, copy_spec, copy_shape = _weight_in(w, (tk, d), lambda i, k: (k, 0))
    emit = copy_spec is not None
    assert not emit or rows == tm, "a bf16 copy needs every weight tile visited exactly once"
    if not emit and tk == kdim:
        w_spec = _resident_spec((tk, d), lambda i, k: (0, 0))
    out_specs = [pl.BlockSpec((tm, d), lambda i, k: (i, 0))]
    out_shape = [jax.ShapeDtypeStruct((rows, d), F32)]
    if emit:
        out_specs.append(copy_spec)
        out_shape.append(copy_shape)
    out = pl.pallas_call(
        functools.partial(_out_proj_kernel, n_k=kdim // tk),
        grid=(rows // tm, kdim // tk),
        in_specs=[
            pl.BlockSpec((tm, tk), lambda i, k: (i, k)),
            w_spec,
            pl.BlockSpec((tm, d), lambda i, k: (i, 0)),
            pl.BlockSpec((1, d), lambda i, k: (0, 0)),
        ],
        out_specs=out_specs,
        out_shape=out_shape,
        compiler_params=_params(),
        name="out_proj",
    )(mix, w_arg, x, g.reshape(1, d))
    return (out[0], out[1]) if emit else (out[0], w)


def _ffn_kernel(x_ref, gpre_ref, wa_ref, wg_ref, wconv_ref, wd_ref, gpost_ref, hist_ref,
                o_ref, state_ref, *rest, n_seq, blocks_per_seq, sub, emit):
    if emit:
        wa_copy_ref, wg_copy_ref, wd_copy_ref, h_ref, carry_ref = rest
    else:
        wa_copy_ref = wg_copy_ref = wd_copy_ref = None
        h_ref, carry_ref = rest
    i = pl.program_id(0)
    j = pl.program_id(1)
    last_j = pl.num_programs(1) - 1
    tm = x_ref.shape[0]
    tf = wa_ref.shape[1]
    rows_per_seq = tm // n_seq

    @pl.when(j == 0)
    def _():
        def start(rows):
            x = x_ref[rows, :]
            h_ref[rows, :] = _rms(x, gpre_ref[...]).astype(BF16)
            o_ref[rows, :] = jnp.zeros_like(x)

        _for_row_chunks(tm, start)

    if blocks_per_seq > 1:
        @pl.when(i % blocks_per_seq == 0)
        def _():
            carry_ref[j] = hist_ref[0]

    h = h_ref[...]
    assert rows_per_seq & (rows_per_seq - 1) == 0
    t = jnp.bitwise_and(lax.broadcasted_iota(jnp.int32, (tm, 1), 0), rows_per_seq - 1)
    groups = [slice(s * sub, (s + 1) * sub) for s in range(tf // sub)]
    if emit:
        wa, wg, wd = (_load_weight(wa_ref, wa_copy_ref), _load_weight(wg_ref, wg_copy_ref),
                      _load_weight(wd_ref, wd_copy_ref))
    else:
        wa, wg, wd = wa_ref, wg_ref, wd_ref
    pending = None
    for cols in groups:
        a = jnp.dot(h, wa[:, cols], preferred_element_type=F32)
        val = jnp.dot(h, wg[:, cols], preferred_element_type=F32)
        if pending is not None:
            o_ref[...] += jnp.dot(pending[0], wd[pending[1], :], preferred_element_type=F32)
        if blocks_per_seq > 1:
            hist0 = carry_ref[j, 0:1, cols]
            hist1 = carry_ref[j, 1:2, cols]
        elif n_seq == 1:
            hist0 = hist_ref[0, 0:1, cols]
            hist1 = hist_ref[0, 1:2, cols]
        else:
            hist = hist_ref[:, :, cols]
            hist0 = jnp.broadcast_to(hist[:, 0:1, :], (n_seq, rows_per_seq, sub)).reshape(tm, sub)
            hist1 = jnp.broadcast_to(hist[:, 1:2, :], (n_seq, rows_per_seq, sub)).reshape(tm, sub)
        prev1, prev2 = pltpu.roll(a, 1, axis=0), pltpu.roll(a, 2, axis=0)
        if n_seq == 1:
            t8 = t[0:SUBLANES]
            top1 = jnp.where(t8 == 0, hist1, prev1[0:SUBLANES])
            top2 = jnp.where(t8 == 0, hist0, jnp.where(t8 == 1, hist1, prev2[0:SUBLANES]))
            prev1 = jnp.concatenate([top1, prev1[SUBLANES:]], axis=0)
            prev2 = jnp.concatenate([top2, prev2[SUBLANES:]], axis=0)
        else:
            prev1 = jnp.where(t == 0, hist1, prev1)
            prev2 = jnp.where(t == 0, hist0, jnp.where(t == 1, hist1, prev2))
        w = wconv_ref[:, cols]
        conv = prev2 * w[0:1] + prev1 * w[1:2] + a * w[2:3]
        pending = ((_gelu_tanh(conv) * val).astype(BF16), cols)
        tail = a.reshape(n_seq, rows_per_seq, sub)[:, rows_per_seq - 2:, :]
        state_ref[j, :, :, cols] = tail
        if blocks_per_seq > 1:
            carry_ref[j, :, cols] = tail[0]
    o_ref[...] += jnp.dot(pending[0], wd[pending[1], :], preferred_element_type=F32)

    @pl.when(j == last_j)
    def _():
        def finish(rows):
            o_ref[rows, :] = x_ref[rows, :] + _rms(o_ref[rows, :], gpost_ref[...])

        _for_row_chunks(tm, finish)


def conv_ffn(x, g_pre, w_up, w_conv, w_down, g_post, hist, *, seq_len, tm, tf):
    rows, d = x.shape
    d_ff = w_down.shape[0]
    tm = _row_tile(rows, tm)
    assert d_ff % tf == 0 and tf % MXU_COLS == 0
    n_ff = d_ff // tf
    emit = isinstance(w_up, LayerWeight)
    assert emit == isinstance(w_down, LayerWeight)
    if emit:
        assert rows == tm, "a bf16 copy needs every weight tile visited exactly once"
        half = jax.ShapeDtypeStruct((d, d_ff), BF16)
        up_args = (w_up.stack, w_up.stack)
        up_specs = [w_up.spec((d, tf), lambda i, j: (0, j)), w_up.spec((d, tf), lambda i, j: (0, j + n_ff))]
        copy_specs = [pl.BlockSpec((d, tf), lambda i, j: (0, j)), pl.BlockSpec((d, tf), lambda i, j: (0, j))]
        copy_shapes = [half, half]
    else:
        up_args = tuple(w_up)
        up_specs = [pl.BlockSpec((d, tf), lambda i, j: (0, j)), pl.BlockSpec((d, tf), lambda i, j: (0, j))]
        copy_specs, copy_shapes = [], []
    wd_arg, wd_spec, wd_copy_spec, wd_copy_shape = _weight_in(w_down, (tf, d), lambda i, j: (j, 0))
    if emit:
        copy_specs.append(wd_copy_spec)
        copy_shapes.append(wd_copy_shape)
    if seq_len >= tm:
        assert seq_len % tm == 0
        n_seq, blocks_per_seq = 1, seq_len // tm
        seq_of = lambda i: i // blocks_per_seq
    else:
        assert tm % seq_len == 0
        n_seq, blocks_per_seq = tm // seq_len, 1
        seq_of = lambda i: i
    kern = functools.partial(_ffn_kernel, n_seq=n_seq, blocks_per_seq=blocks_per_seq, sub=MXU_COLS, emit=emit)
    out = pl.pallas_call(
        kern,
        grid=(rows // tm, n_ff),
        in_specs=[
            (_resident_spec if rows == tm else pl.BlockSpec)((tm, d), lambda i, j: (i, 0)),
            pl.BlockSpec((1, d), lambda i, j: (0, 0)),
            up_specs[0],
            up_specs[1],
            pl.BlockSpec((3, tf), lambda i, j: (0, j)),
            wd_spec,
            pl.BlockSpec((1, d), lambda i, j: (0, 0)),
            pl.BlockSpec((n_seq, 2, tf), lambda i, j: (seq_of(i), 0, j)),
        ],
        out_specs=[
            pl.BlockSpec((tm, d), lambda i, j: (i, 0)),
            pl.BlockSpec((n_ff, n_seq, 2, tf), lambda i, j: (0, seq_of(i), 0, 0)),
        ] + copy_specs,
        out_shape=[
            jax.ShapeDtypeStruct((rows, d), F32),
            jax.ShapeDtypeStruct((n_ff, rows // seq_len, 2, tf), F32),
        ] + copy_shapes,
        scratch_shapes=[pltpu.VMEM((tm, d), BF16), pltpu.VMEM((n_ff, 2, tf), F32)],
        compiler_params=_params(),
        name="conv_ffn",
    )(x, g_pre.reshape(1, d), *up_args, w_conv, wd_arg, g_post.reshape(1, d), hist)
    state = jnp.moveaxis(out[1], 0, 2).reshape(rows // seq_len, 2, d_ff)
    weights = ((out[2], out[3]), out[4]) if emit else (tuple(w_up), w_down)
    return out[0], state, weights


def _ple_kernel(x_ref, p_ref, gpre_ref, wgate_ref, wproj_ref, gpost_ref, o_ref, *rest, n_k, emit):
    wgate_copy_ref, wproj_copy_ref = rest[:2] if emit else (None, None)
    h_ref = rest[-1] if n_k > 1 else None
    k = pl.program_id(1)
    tk = wgate_ref.shape[0]

    def finish(pre):
        emb = jnp.dot(p_ref[...].astype(BF16), _load_weight(wproj_ref, wproj_copy_ref),
                      preferred_element_type=F32)
        o_ref[...] = x_ref[...] + _rms(jax.nn.sigmoid(pre) * emb, gpost_ref[...])

    if n_k == 1:
        h = _rms(x_ref[...], gpre_ref[...]).astype(BF16)
        finish(jnp.dot(h, _load_weight(wgate_ref, wgate_copy_ref), preferred_element_type=F32))
    else:
        @pl.when(k == 0)
        def _():
            h = _rms(x_ref[...], gpre_ref[...]).astype(BF16)
            for kk in range(n_k):
                h_ref[kk] = h[:, kk * tk:(kk + 1) * tk]

        _accumulate(o_ref, jnp.dot(h_ref[k], _load_weight(wgate_ref, wgate_copy_ref),
                                   preferred_element_type=F32), k)

        @pl.when(k == n_k - 1)
        def _():
            finish(o_ref[...])


def per_layer_embedding(x, p, layer, g_pre, w_gate, w_proj, g_post, *, tm, tk=None):
    rows, d = x.shape
    pd = p.shape[-1]
    tm = _row_tile(rows, tm)
    tk = d if tk is None else tk
    assert d % tk == 0
    n_k = d // tk
    wg_arg, wg_spec, wg_copy_spec, wg_copy_shape = _weight_in(w_gate, (tk, d), lambda i, k: (k, 0))
    wp_arg, wp_spec, wp_copy_spec, wp_copy_shape = _weight_in(w_proj, (pd, d), lambda i, k: (0, 0))
    emit = wg_copy_spec is not None
    assert emit == (wp_copy_spec is not None)
    assert not emit or rows == tm, "a bf16 copy needs every weight tile visited exactly once"
    if not emit:
        wp_spec = _resident_spec((pd, d), lambda i, k: (0, 0))
        if n_k == 1:
            wg_spec = _resident_spec((tk, d), lambda i, k: (0, 0))
    out_specs = [pl.BlockSpec((tm, d), lambda i, k: (i, 0))]
    out_shape = [jax.ShapeDtypeStruct((rows, d), F32)]
    if emit:
        out_specs += [wg_copy_spec, wp_copy_spec]
        out_shape += [wg_copy_shape, wp_copy_shape]
    out = pl.pallas_call(
        functools.partial(_ple_kernel, n_k=n_k, emit=emit),
        grid=(rows // tm, n_k),
        in_specs=[
            pl.BlockSpec((tm, d), lambda i, k: (i, 0)),
            pl.BlockSpec((None, tm, pd), lambda i, k: (layer, i, 0)),
            pl.BlockSpec((1, d), lambda i, k: (0, 0)),
            wg_spec,
            wp_spec,
            pl.BlockSpec((1, d), lambda i, k: (0, 0)),
        ],
        out_specs=out_specs,
        out_shape=out_shape,
        scratch_shapes=[pltpu.VMEM((n_k, tm, tk), BF16)] if n_k > 1 else [],
        compiler_params=_params(),
        name="per_layer_embedding",
    )(x, p, g_pre.reshape(1, d), wg_arg, wp_arg, g_post.reshape(1, d))
    return (out[0], (out[1], out[2])) if emit else (out[0], (w_gate, w_proj))


def _rope_tables(pos, dim, reps):
    inv = 1.0 / (ROPE_THETA ** (jnp.arange(0, dim, 2, dtype=F32) / dim))
    ang = pos.astype(F32)[:, None] * inv[None, :]
    cos, sin = jnp.cos(ang), jnp.sin(ang)
    return (jnp.tile(jnp.concatenate([cos, cos], axis=-1), (1, reps)),
            jnp.tile(jnp.concatenate([-sin, sin], axis=-1), (1, reps)))


def _layer(i, x, p, grp, states, wts, prm, cfg):
    b, t = grp['b'], grp['t']
    rows = b * t
    j = i // 2
    wq = {}
    if i % 2 == 0:
        n_heads, dk, dv = states['ret'].shape[2:]
        proj, wq['w_in'] = even_in_proj(
            x, prm['norm_mix_pre'][i], wts['w_in'], *grp['even_tabs'], tm=cfg['tm_in'], tn=cfg['tn_even'],
            d_a=prm['w_conv_a'].shape[-1], n_heads=n_heads, dk=dk, dv=dv)
        mix, c_s, r_s = even_mixer(proj.reshape(b, t, -1), states['conv'][j], states['ret'][j],
                                   prm['w_conv_a'][j], prm['ret_gn'][j], blk=cfg['blk_even'])
        new = dict(conv=c_s, ret=r_s)
    else:
        proj, wq['w_in'] = norm_matmul(x, prm['norm_mix_pre'][i], wts['w_in'], tm=cfg['tm_in'], tn=cfg['tn_odd'])
        lanes = grp['kv_lanes']
        has_cache = states['cache_k'] is not None
        if has_cache:
            ck = states['cache_k'][j].reshape(b, WINDOW, lanes)
            cv = states['cache_v'][j].reshape(b, WINDOW, lanes)
            n_state_rows = t
        else:
            ck = cv = jnp.zeros((b, WINDOW, lanes), F32)
            n_state_rows = cfg['kv_rows']
        mix, p_s, k_s, v_s = odd_mixer(proj.reshape(b, t, -1), *grp['odd_tabs'], states['pool'][j], ck, cv,
                                       prm['w_pool'][j], prm['pool_scale'][j], prm['sinks'][j],
                                       blk=cfg['blk_odd'], pos0=grp['pos0'], has_cache=has_cache,
                                       n_state_rows=n_state_rows)
        kv_shape = (b, n_state_rows, 2, lanes // 2)
        new = dict(pool=p_s, k=k_s.reshape(kv_shape), v=v_s.reshape(kv_shape))
    x, wq['w_out'] = out_proj(mix.reshape(rows, -1), wts['w_out'], x, prm['norm_mix_post'][i],
                              tm=cfg['tm_out'], tk=cfg['tk_out'])
    x, new['ffn'], (wq['w_up'], wq['w_down']) = conv_ffn(
        x, prm['norm_ffn_pre'][i], wts['w_up'], prm['w_conv_ffn'][i], wts['w_down'], prm['norm_ffn_post'][i],
        states['ffn'][i], seq_len=t, tm=cfg['tm_ffn'], tf=cfg['tf'])
    x, (wq['w_gate'], wq['w_proj']) = per_layer_embedding(
        x, p, i, prm['norm_ple_pre'][i], wts['w_gate'], wts['w_proj'], prm['norm_ple_post'][i],
        tm=cfg['tm_ple'], tk=cfg['tk_ple'])
    return x, new, wq


def kernel(x_prompt, x_sample, state_conv, state_ret, state_pool, cache_k, cache_v, state_ffn, p_prompt, p_sample, norm_mix_pre, norm_mix_post, norm_ffn_pre, norm_ffn_post, norm_ple_pre, norm_ple_post, w_in_even, w_conv_a, ret_gn, w_out_even, w_in_odd, w_pool, pool_scale, sinks, w_out_odd, w_up, w_conv_ffn, w_down, w_ple_gate, w_ple_proj):
    depth = norm_mix_pre.shape[0]
    n_even, n_odd = w_in_even.shape[0], w_in_odd.shape[0]
    d = x_prompt.shape[-1]
    d_a = w_conv_a.shape[-1]
    n_heads, dk, dv = state_ret.shape[2:]
    d_c = pool_scale.shape[-1]
    kv_rows, n_kv, hd = cache_k.shape[2:]
    d_ff = w_conv_ffn.shape[-1]
    lanes = n_kv * hd
    dt = x_prompt.dtype

    prm = dict(norm_mix_pre=norm_mix_pre, norm_mix_post=norm_mix_post, norm_ffn_pre=norm_ffn_pre,
               norm_ffn_post=norm_ffn_post, norm_ple_pre=norm_ple_pre, norm_ple_post=norm_ple_post,
               w_conv_a=w_conv_a, ret_gn=ret_gn, w_pool=w_pool.astype(BF16), pool_scale=pool_scale, sinks=sinks,
               w_conv_ffn=w_conv_ffn)

    def group(x, pos0):
        b, t = x.shape[:2]
        pos = pos0 + jnp.arange(t, dtype=jnp.int32)
        return dict(b=b, t=t, pos0=pos0, kv_lanes=lanes,
                    even_tabs=tuple(jnp.tile(tab, (b, 1)) for tab in _rope_tables(pos, dk, 1)),
                    odd_tabs=_rope_tables(pos, hd, lanes // hd))

    bp, sp = x_prompt.shape[:2]
    bs, ts = x_sample.shape[:2]
    grp_p, grp_s = group(x_prompt, 0), group(x_sample, PAST_LEN)
    st_p = dict(conv=jnp.zeros((n_even, bp, 2, d_a), dt), ret=jnp.zeros((n_even, bp, n_heads, dk, dv), dt),
                pool=jnp.zeros((n_odd, bp, POOL_HIST, d_c), dt), cache_k=None, cache_v=None,
                ffn=jnp.zeros((depth, bp, 2, d_ff), dt))
    st_s = dict(conv=state_conv, ret=state_ret, pool=state_pool, cache_k=cache_k, cache_v=cache_v, ffn=state_ffn)
    cfg_s = dict(tm_in=bs * ts, tn_even=1024, tn_odd=w_in_odd.shape[-1] // 2, blk_even=ts, blk_odd=ts,
                 tm_out=bs * ts, tk_out=512, tm_ffn=bs * ts, tf=512, tm_ple=bs * ts, tk_ple=512, kv_rows=kv_rows)
    cfg_p = dict(tm_in=1024, tn_even=1024, tn_odd=w_in_odd.shape[-1] // 2, blk_even=256, blk_odd=128,
                 tm_out=512, tk_out=None, tm_ffn=1024, tf=512, tm_ple=1024, tk_ple=None, kv_rows=kv_rows)
    xp = x_prompt.reshape(bp * sp, d)
    xs = x_sample.reshape(bs * ts, d)
    pp = p_prompt.reshape(depth, bp * sp, -1)
    ps = p_sample.reshape(depth, bs * ts, -1)
    new_p, new_s = [], []
    for i in range(depth):
        j = i // 2
        raw = dict(w_in=LayerWeight(w_in_even if i % 2 == 0 else w_in_odd, j),
                   w_out=LayerWeight(w_out_even if i % 2 == 0 else w_out_odd, j),
                   w_up=LayerWeight(w_up, i), w_down=LayerWeight(w_down, i),
                   w_gate=LayerWeight(w_ple_gate, i), w_proj=LayerWeight(w_ple_proj, i))
        xs, st, wq = _layer(i, xs, ps, grp_s, st_s, raw, prm, cfg_s)
        new_s.append(st)
        xp, st, _ = _layer(i, xp, pp, grp_p, st_p, wq, prm, cfg_p)
        new_p.append(st)

    def stacked(new, key):
        parts = [st[key] for st in new if key in st]
        return parts[0][None] if len(parts) == 1 else jnp.stack(parts)

    outs = [xp.reshape(bp, sp, d), xs.reshape(bs, ts, d)]
    for key in ('conv', 'ret', 'pool', 'k', 'v', 'ffn'):
        outs += [stacked(new_p, key), stacked(new_s, key)]
    return tuple(outs)
```

```python
import functools
import math

import jax
import jax.numpy as jnp
from jax import lax
from jax.experimental import pallas as pl
from jax.experimental.pallas import tpu as pltpu

CHUNK = 64
WINDOW = 128
PAST_LEN = 4096
EPS = 1e-6
ROPE_THETA = 10000.0
NEG_INF = -1e30
POOL_WINDOWS = (2, 4, 8, 16)
POOL_HIST = max(POOL_WINDOWS) - 1

V7X_VMEM_BYTES = 64 * 1024 * 1024
VMEM_LIMIT_BYTES = V7X_VMEM_BYTES - 6 * 1024 * 1024
MXU_COLS = 256
SUBLANES = 8

F32 = jnp.float32
BF16 = jnp.bfloat16


def _params():
    return pltpu.CompilerParams(vmem_limit_bytes=VMEM_LIMIT_BYTES)


def _rms(x, g):
    return x * lax.rsqrt(jnp.mean(x * x, axis=-1, keepdims=True) + EPS) * g


def _gelu_tanh(x):
    c1 = math.sqrt(2.0 / math.pi)
    half = 0.5 * x
    return half + half * jnp.tanh(x * (c1 + (c1 * 0.044715) * (x * x)))


ROW_CHUNK = 256


def _for_row_chunks(n_rows, fn):
    if n_rows <= ROW_CHUNK or n_rows % ROW_CHUNK:
        fn(slice(0, n_rows))
        return

    def body(r, carry):
        fn(pl.ds(pl.multiple_of(r * ROW_CHUNK, ROW_CHUNK), ROW_CHUNK))
        return carry

    lax.fori_loop(0, n_rows // ROW_CHUNK, body, 0)


def _resident_spec(block, index_map):
    return pl.BlockSpec(block, index_map, pipeline_mode=pl.Buffered(1))


def _row_tile(rows, want):
    t = min(rows, want)
    assert rows % t == 0, (rows, t)
    return t


class LayerWeight:
    def __init__(self, stack, layer):
        self.stack, self.layer = stack, layer
        self.shape = stack.shape[1:]

    def spec(self, block, index_map):
        layer = self.layer
        return pl.BlockSpec((None,) + block, lambda *g: (layer,) + index_map(*g))


def _weight_in(w, block, index_map):
    if isinstance(w, LayerWeight):
        return (w.stack, w.spec(block, index_map), pl.BlockSpec(block, index_map),
                jax.ShapeDtypeStruct(w.shape, BF16))
    return w, pl.BlockSpec(block, index_map), None, None


def _load_weight(w_ref, copy_ref):
    w = w_ref[...]
    if copy_ref is not None:
        w = w.astype(BF16)
        copy_ref[...] = w
    return w


BF16_SUBLANES = 16


def _side_casts(sides, n_steps, step_of):
    args, in_specs, out_specs, out_shapes = [], [], [], []
    for side in sides:
        rows, cols = side.shape
        n_slabs = max(n for n in range(1, n_steps + 1) if rows % n == 0 and (rows // n) % BF16_SUBLANES == 0)
        slab = rows // n_slabs
        index_map = lambda *g, last=n_slabs - 1: (jnp.minimum(step_of(*g), last), 0)
        args.append(side.stack)
        in_specs.append(side.spec((slab, cols), index_map))
        out_specs.append(pl.BlockSpec((slab, cols), index_map))
        out_shapes.append(jax.ShapeDtypeStruct((rows, cols), BF16))
    return args, in_specs, out_specs, out_shapes


def _run_side_casts(side_refs):
    n = len(side_refs) // 2
    for src, dst in zip(side_refs[:n], side_refs[n:]):
        dst[...] = src[...].astype(BF16)


def _norm_matmul_kernel(x_ref, g_ref, w_ref, *rest, emit):
    o_ref, copy_ref, h_ref = rest if emit else (rest[0], None, rest[1])

    @pl.when(pl.program_id(1) == 0)
    def _():
        def start(rows):
            h_ref[rows, :] = _rms(x_ref[rows, :], g_ref[...]).astype(BF16)

        _for_row_chunks(x_ref.shape[0], start)

    w = _load_weight(w_ref, copy_ref)
    o_ref[...] = jnp.dot(h_ref[...], w, preferred_element_type=F32)


def norm_matmul(x, g, w, *, tm, tn):
    rows, d = x.shape
    n = w.shape[1]
    tm = _row_tile(rows, tm)
    assert n % tn == 0
    w_arg, w_spec, copy_spec, copy_shape = _weight_in(w, (d, tn), lambda i, j: (0, j))
    emit = copy_spec is not None
    assert not emit or rows == tm, "a bf16 copy needs every weight tile visited exactly once"
    out_specs = [pl.BlockSpec((tm, tn), lambda i, j: (i, j))]
    out_shape = [jax.ShapeDtypeStruct((rows, n), F32)]
    if emit:
        out_specs.append(copy_spec)
        out_shape.append(copy_shape)
    out = pl.pallas_call(
        functools.partial(_norm_matmul_kernel, emit=emit),
        grid=(rows // tm, n // tn),
        in_specs=[
            pl.BlockSpec((tm, d), lambda i, j: (i, 0)),
            pl.BlockSpec((1, d), lambda i, j: (0, 0)),
            w_spec,
        ],
        out_specs=out_specs,
        out_shape=out_shape,
        scratch_shapes=[pltpu.VMEM((tm, d), BF16)],
        compiler_params=_params(),
        name="norm_matmul",
    )(x, g.reshape(1, d), w_arg)
    return (out[0], out[1]) if emit else (out[0], w)


def _even_in_proj_kernel(x_ref, g_ref, w_ref, cos_ref, sin_ref, *rest, emit, n_side, kinds, dk, k_scale):
    side_in, rest = rest[:n_side], rest[n_side:]
    o_ref, rest = rest[0], rest[1:]
    copy_ref, rest = (rest[0], rest[1:]) if emit else (None, rest)
    side_out, (h_ref,) = rest[:n_side], rest[n_side:]
    _run_side_casts(side_in + side_out)
    j = pl.program_id(1)
    tn = w_ref.shape[1]

    @pl.when(j == 0)
    def _():
        def start(rows):
            h_ref[rows, :] = _rms(x_ref[rows, :], g_ref[...]).astype(BF16)

        _for_row_chunks(x_ref.shape[0], start)

    def project():
        return jnp.dot(h_ref[...], _load_weight(w_ref, copy_ref), preferred_element_type=F32)

    def rope(acc, scale):
        cosb, sinb = cos_ref[...], sin_ref[...]
        for hh in range(tn // dk):
            seg = acc[:, hh * dk:(hh + 1) * dk]
            out = seg * cosb + pltpu.roll(seg, dk // 2, axis=1) * sinb
            if scale != 1.0:
                out = out * scale
            o_ref[:, hh * dk:(hh + 1) * dk] = out.astype(BF16)

    def finish(kind):
        acc = project()
        if kind == 'rope':
            rope(acc, 1.0)
        elif kind == 'rope_scaled':
            rope(acc, k_scale)
        elif kind == 'plain':
            o_ref[...] = acc.astype(BF16)
        else:
            assert kind == 'silu'
            o_ref[...] = (acc * jax.nn.sigmoid(acc)).astype(BF16)

    for kind in dict.fromkeys(kinds):
        tiles = [t for t, k in enumerate(kinds) if k == kind]
        here = functools.reduce(jnp.logical_or, [j == t for t in tiles])
        pl.when(here)(functools.partial(finish, kind))


def even_in_proj(x, g, w, cosb, sinb, *, tm, tn, d_a, n_heads, dk, dv, sides=()):
    rows, d = x.shape
    n = w.shape[1]
    tm = _row_tile(rows, tm)
    hk, hv = n_heads * dk, n_heads * dv
    assert n == 3 * d_a + 2 * hk + 2 * hv and tn % dk == 0
    assert all(part % tn == 0 for part in (d_a, hk, hv))
    kinds = (('plain',) * (3 * d_a // tn) + ('rope',) * (hk // tn) + ('rope_scaled',) * (hk // tn)
             + ('plain',) * (hv // tn) + ('silu',) * (hv // tn))
    w_arg, w_spec, copy_spec, copy_shape = _weight_in(w, (d, tn), lambda i, j: (0, j))
    emit = copy_spec is not None
    assert not emit or rows == tm, "a bf16 copy needs every weight tile visited exactly once"
    out_specs = [pl.BlockSpec((tm, tn), lambda i, j: (i, j))]
    out_shape = [jax.ShapeDtypeStruct((rows, n), BF16)]
    if emit:
        out_specs.append(copy_spec)
        out_shape.append(copy_shape)
    n_col = n // tn
    side_args, side_in, side_out, side_shapes = _side_casts(
        sides, (rows // tm) * n_col, lambda i, j: i * n_col + j)
    kern = functools.partial(_even_in_proj_kernel, emit=emit, n_side=len(sides), kinds=kinds, dk=dk,
                             k_scale=dk ** -0.5)
    out = pl.pallas_call(
        kern,
        grid=(rows // tm, n_col),
        in_specs=[
            pl.BlockSpec((tm, d), lambda i, j: (i, 0)),
            pl.BlockSpec((1, d), lambda i, j: (0, 0)),
            w_spec,
            pl.BlockSpec((tm, dk), lambda i, j: (i, 0)),
            pl.BlockSpec((tm, dk), lambda i, j: (i, 0)),
        ] + side_in,
        out_specs=out_specs + side_out,
        out_shape=out_shape + side_shapes,
        scratch_shapes=[pltpu.VMEM((tm, d), BF16)],
        compiler_params=_params(),
        name="even_in_proj",
    )(x, g.reshape(1, d), w_arg, cosb, sinb, *side_args)
    n_main = 2 if emit else 1
    return out[0], (out[1] if emit else w), list(out[n_main:])


def _even_mixer_kernel(p_ref, chist_ref, rstate_ref, wconv_ref, gn_ref,
                       mix_ref, cstate_ref, rout_ref, carry_ref, s_ref, dmat_ref, qdec_ref, kdec_ref,
                       *, n_heads, dk, dv, d_a):
    q0 = 3 * d_a
    k0 = q0 + n_heads * dk
    v0 = k0 + n_heads * dk
    g0 = v0 + n_heads * dv
    c = pl.program_id(1)
    last = pl.num_programs(1) - 1
    blk = p_ref.shape[0]

    @pl.when(c == 0)
    def _():
        carry_ref[...] = chist_ref[...]
        s_ref[...] = rstate_ref[...]

    u = p_ref[:, 2 * d_a:3 * d_a].astype(F32) * p_ref[:, 0:d_a].astype(F32)
    row = lax.broadcasted_iota(jnp.int32, u.shape, 0)
    h0 = carry_ref[0:1, :]
    h1 = carry_ref[1:2, :]
    prev1 = jnp.where(row == 0, h1, pltpu.roll(u, 1, axis=0))
    prev2 = jnp.where(row == 0, h0, jnp.where(row == 1, h1, pltpu.roll(u, 2, axis=0)))
    w = wconv_ref[...]
    conv = prev2 * w[0:1] + prev1 * w[1:2] + u * w[2:3]
    mix_ref[:, 0:d_a] = (p_ref[:, d_a:2 * d_a].astype(F32) * conv).astype(BF16)
    tail = u[blk - 2:blk, :]
    carry_ref[...] = tail

    @pl.when(c == last)
    def _():
        cstate_ref[...] = tail

    log_gamma = [math.log1p(-(2.0 ** (-5.0 - h))) for h in range(n_heads)]

    @pl.when((pl.program_id(0) == 0) & (c == 0))
    def _():
        rel = (lax.broadcasted_iota(jnp.int32, (blk, blk), 0)
               - lax.broadcasted_iota(jnp.int32, (blk, blk), 1)).astype(F32)
        ii = lax.broadcasted_iota(jnp.int32, (blk, dk), 0).astype(F32)
        for h, lg in enumerate(log_gamma):
            dmat_ref[h] = jnp.where(rel >= 0, jnp.exp(jnp.maximum(rel, 0.0) * lg), 0.0)
            qdec_ref[h] = jnp.exp((ii + 1.0) * lg)
            kdec_ref[h] = jnp.exp((blk - 1.0 - ii) * lg)

    stage = []
    for h in range(n_heads):
        qr = p_ref[:, q0 + h * dk:q0 + (h + 1) * dk]
        kr = p_ref[:, k0 + h * dk:k0 + (h + 1) * dk]
        vb = p_ref[:, v0 + h * dv:v0 + (h + 1) * dv]
        sc = lax.dot_general(qr, kr, (((1,), (1,)), ((), ())), preferred_element_type=F32)
        st = s_ref[h]
        inter = jnp.dot((qr.astype(F32) * qdec_ref[h]).astype(BF16), st.astype(BF16),
                        preferred_element_type=F32)
        kv = lax.dot_general((kr.astype(F32) * kdec_ref[h]).astype(BF16), vb, (((0,), (0,)), ((), ())),
                             preferred_element_type=F32)
        s_ref[h] = math.exp(blk * log_gamma[h]) * st + kv
        stage.append((sc, inter, vb))
    for h, (sc, inter, vb) in enumerate(stage):
        o = jnp.dot((sc * dmat_ref[h]).astype(BF16), vb, preferred_element_type=F32) + inter
        on = o * lax.rsqrt(jnp.mean(o * o, axis=-1, keepdims=True) + EPS) * gn_ref[h:h + 1, :]
        gate = p_ref[:, g0 + h * dv:g0 + (h + 1) * dv].astype(F32)
        mix_ref[:, d_a + h * dv:d_a + (h + 1) * dv] = (on * gate).astype(BF16)

    @pl.when(c == last)
    def _():
        rout_ref[...] = s_ref[...]


def even_mixer(p, conv_hist, ret_state, w_conv, gn, *, blk):
    b, t, width = p.shape
    n_heads, dk, dv = ret_state.shape[1:]
    d_a = w_conv.shape[1]
    hv = n_heads * dv
    assert width == 3 * d_a + 2 * n_heads * dk + 2 * hv
    blk = _row_tile(t, blk)
    nc = t // blk
    kern = functools.partial(_even_mixer_kernel, n_heads=n_heads, dk=dk, dv=dv, d_a=d_a)
    return pl.pallas_call(
        kern,
        grid=(b, nc),
        in_specs=[
            pl.BlockSpec((None, blk, width), lambda bi, ci: (bi, ci, 0)),
            pl.BlockSpec((None, 2, d_a), lambda bi, ci: (bi, 0, 0)),
            pl.BlockSpec((None, n_heads, dk, dv), lambda bi, ci: (bi, 0, 0, 0)),
            pl.BlockSpec((3, d_a), lambda bi, ci: (0, 0)),
            pl.BlockSpec((n_heads, dv), lambda bi, ci: (0, 0)),
        ],
        out_specs=[
            pl.BlockSpec((None, blk, d_a + hv), lambda bi, ci: (bi, ci, 0)),
            pl.BlockSpec((None, 2, d_a), lambda bi, ci: (bi, 0, 0)),
            pl.BlockSpec((None, n_heads, dk, dv), lambda bi, ci: (bi, 0, 0, 0)),
        ],
        out_shape=[
            jax.ShapeDtypeStruct((b, t, d_a + hv), BF16),
            jax.ShapeDtypeStruct((b, 2, d_a), F32),
            jax.ShapeDtypeStruct((b, n_heads, dk, dv), F32),
        ],
        scratch_shapes=[pltpu.VMEM((2, d_a), F32), pltpu.VMEM((n_heads, dk, dv), F32),
                        pltpu.VMEM((n_heads, blk, blk), F32), pltpu.VMEM((n_heads, blk, dk), F32),
                        pltpu.VMEM((n_heads, blk, dk), F32)],
        compiler_params=_params(),
        name="even_mixer",
    )(p, conv_hist, ret_state, w_conv, gn)


def _odd_mixer_kernel(sinks_ref, p_ref, cos_ref, sin_ref,
                      phist_ref, ck_ref, cv_ref, wpool_ref, pscale_ref,
                      mix_ref, pstate_ref, kstate_ref, vstate_ref,
                      ext_ref, kbuf_ref, vbuf_ref,
                      *, d_c, n_q_heads, n_kv_heads, hd, pos0, has_cache, n_state_rows):
    c = pl.program_id(1)
    last = pl.num_programs(1) - 1
    blk = p_ref.shape[0]
    hist_rows = POOL_HIST + 1
    d_cg = d_c // len(POOL_WINDOWS)
    lanes = 2 * hd
    q0 = d_c
    k0 = q0 + n_q_heads * hd
    v0 = k0 + lanes

    @pl.when(c == 0)
    def _():
        ext_ref[0:1, :] = jnp.zeros((1, d_c), F32)
        ext_ref[1:hist_rows, :] = phist_ref[...]
        kbuf_ref[0:WINDOW, :] = ck_ref[...]
        vbuf_ref[0:WINDOW, :] = cv_ref[...]

    u = p_ref[:, 0:d_c]
    ext_ref[hist_rows:hist_rows + blk, :] = u
    pos = pos0 + c * blk + lax.broadcasted_iota(jnp.int32, (blk, 1), 0)
    for gi, win in enumerate(POOL_WINDOWS):
        cols = slice(gi * d_cg, (gi + 1) * d_cg)
        s = ext_ref[:, cols]
        span = 1
        while span < win:
            s = s + pltpu.roll(s, span, axis=0)
            span *= 2
        cnt = jnp.minimum(pos + 1, win).astype(F32)
        d = s[hist_rows:, :] / cnt - u[:, cols]
        y = jnp.dot(d.astype(BF16), wpool_ref[gi], preferred_element_type=F32)
        mix_ref[:, cols] = (y * pscale_ref[:, cols]).astype(BF16)
    tail = ext_ref[blk:blk + hist_rows, :]
    ext_ref[0:hist_rows, :] = tail

    @pl.when(c == last)
    def _():
        pstate_ref[...] = ext_ref[1:hist_rows, :]

    cos4 = cos_ref[...]
    sin4 = sin_ref[...]
    lane = lax.broadcasted_iota(jnp.int32, (1, lanes), 1)
    first_half = jnp.bitwise_and(lane, hd - 1) < (hd // 2)
    left = lane < hd

    def rope(x):
        rot = jnp.where(first_half, pltpu.roll(x, lanes - hd // 2, axis=1), pltpu.roll(x, hd // 2, axis=1))
        return x * cos4 + rot * sin4

    kr = rope(p_ref[:, k0:k0 + lanes])
    vv = p_ref[:, v0:v0 + lanes]
    kbuf_ref[WINDOW:WINDOW + blk, :] = kr
    vbuf_ref[WINDOW:WINDOW + blk, :] = vv
    kall = kbuf_ref[...]
    vall = vbuf_ref[...]
    kswap = pltpu.roll(kall, hd, axis=1)
    vswap = pltpu.roll(vall, hd, axis=1)
    nk = WINDOW + blk
    chunk_shift = CHUNK.bit_length() - 1
    qi = jnp.right_shift(lax.broadcasted_iota(jnp.int32, (blk, 1), 0), chunk_shift)
    kj = lax.broadcasted_iota(jnp.int32, (1, nk), 1)
    kc = jnp.right_shift(kj, chunk_shift) - WINDOW // CHUNK
    ok = (kc <= qi) & (kc >= qi - WINDOW // CHUNK)
    if not has_cache:
        ok = ok & ((kj >= WINDOW) | (c > 0))
    scale = hd ** -0.5
    group = n_q_heads // n_kv_heads
    zeros = jnp.zeros_like(vall)
    kv_ops = [
        (jnp.where(left, kall, kswap).astype(BF16), jnp.where(left, vall, zeros).astype(BF16),
         jnp.where(left, zeros, vswap).astype(BF16)),
        (jnp.where(left, kswap, kall).astype(BF16), jnp.where(left, vswap, zeros).astype(BF16),
         jnp.where(left, zeros, vall).astype(BF16)),
    ]
    scores = []
    for head in range(n_q_heads):
        kvh = head // group
        if head % 2 == 0:
            qr = rope(p_ref[:, q0 + (head // 2) * lanes:q0 + (head // 2 + 1) * lanes])
        qh = jnp.where(left if head % 2 == 0 else jnp.logical_not(left), qr, 0.0).astype(BF16)
        scores.append(lax.dot_general(qh, kv_ops[kvh][0], (((1,), (1,)), ((), ())),
                                      preferred_element_type=F32))
    probs = []
    for head, sc in enumerate(scores):
        sc = jnp.where(ok, sc * scale, NEG_INF)
        sink = sinks_ref[head]
        m = jnp.maximum(jnp.max(sc, axis=-1, keepdims=True), sink)
        e = jnp.exp(sc - m)
        den = jnp.sum(e, axis=-1, keepdims=True) + jnp.exp(sink - m)
        probs.append((e / den).astype(BF16))
    for pair in range(n_q_heads // 2):
        kvh = (2 * pair) // group
        assert (2 * pair + 1) // group == kvh
        out = (jnp.dot(probs[2 * pair], kv_ops[kvh][1], preferred_element_type=F32)
               + jnp.dot(probs[2 * pair + 1], kv_ops[kvh][2], preferred_element_type=F32))
        mix_ref[:, d_c + pair * lanes:d_c + (pair + 1) * lanes] = out.astype(BF16)
    ktail = kbuf_ref[blk:blk + WINDOW, :]
    vtail = vbuf_ref[blk:blk + WINDOW, :]
    kbuf_ref[0:WINDOW, :] = ktail
    vbuf_ref[0:WINDOW, :] = vtail

    @pl.when(c == last)
    def _():
        kstate_ref[...] = kbuf_ref[nk - n_state_rows:nk, :]
        vstate_ref[...] = vbuf_ref[nk - n_state_rows:nk, :]


def odd_mixer(p, cos4, sin4, pool_hist, cache_k, cache_v, w_pool, pool_scale, sinks, *,
              blk, pos0, has_cache, n_state_rows):
    b, t, width = p.shape
    d_c = pool_scale.shape[-1]
    n_q_heads = sinks.shape[-1]
    lanes = cache_k.shape[-1]
    n_kv_heads = 2
    hd = lanes // n_kv_heads
    d_q = n_q_heads * hd
    assert d_q == d_c and d_c % lanes == 0 and width == d_c + d_q + 2 * lanes
    blk = _row_tile(t, blk)
    nc = t // blk
    assert blk % CHUNK == 0 or nc == 1
    kern = functools.partial(_odd_mixer_kernel, d_c=d_c, n_q_heads=n_q_heads, n_kv_heads=n_kv_heads,
                             hd=hd, pos0=pos0, has_cache=has_cache, n_state_rows=n_state_rows)
    return pl.pallas_call(
        kern,
        grid=(b, nc),
        in_specs=[
            pl.BlockSpec(memory_space=pltpu.SMEM),
            pl.BlockSpec((None, blk, width), lambda bi, ci: (bi, ci, 0)),
            pl.BlockSpec((blk, lanes), lambda bi, ci: (ci, 0)),
            pl.BlockSpec((blk, lanes), lambda bi, ci: (ci, 0)),
            pl.BlockSpec((None, POOL_HIST, d_c), lambda bi, ci: (bi, 0, 0)),
            pl.BlockSpec((None, WINDOW, lanes), lambda bi, ci: (bi, 0, 0)),
            pl.BlockSpec((None, WINDOW, lanes), lambda bi, ci: (bi, 0, 0)),
            pl.BlockSpec(w_pool.shape, lambda bi, ci: (0, 0, 0)),
            pl.BlockSpec((1, d_c), lambda bi, ci: (0, 0)),
        ],
        out_specs=[
            pl.BlockSpec((None, blk, d_c + d_q), lambda bi, ci: (bi, ci, 0)),
            pl.BlockSpec((None, POOL_HIST, d_c), lambda bi, ci: (bi, 0, 0)),
            pl.BlockSpec((None, n_state_rows, lanes), lambda bi, ci: (bi, 0, 0)),
            pl.BlockSpec((None, n_state_rows, lanes), lambda bi, ci: (bi, 0, 0)),
        ],
        out_shape=[
            jax.ShapeDtypeStruct((b, t, d_c + d_q), BF16),
            jax.ShapeDtypeStruct((b, POOL_HIST, d_c), F32),
            jax.ShapeDtypeStruct((b, n_state_rows, lanes), F32),
            jax.ShapeDtypeStruct((b, n_state_rows, lanes), F32),
        ],
        scratch_shapes=[
            pltpu.VMEM((POOL_HIST + 1 + blk, d_c), F32),
            pltpu.VMEM((WINDOW + blk, lanes), F32),
            pltpu.VMEM((WINDOW + blk, lanes), F32),
        ],
        compiler_params=_params(),
        name="odd_mixer",
    )(sinks, p, cos4, sin4, pool_hist, cache_k, cache_v, w_pool, pool_scale.reshape(1, d_c))


def _accumulate(o_ref, part, k):
    @pl.when(k == 0)
    def _():
        o_ref[...] = part

    @pl.when(k > 0)
    def _():
        o_ref[...] += part


def _out_proj_kernel(mix_ref, w_ref, x_ref, g_ref, *rest, n_k, emit, n_side):
    side_in, rest = rest[:n_side], rest[n_side:]
    o_ref, rest = rest[0], rest[1:]
    copy_ref, side_out = (rest[0], rest[1:]) if emit else (None, rest)
    _run_side_casts(side_in + side_out)
    k = pl.program_id(1)
    part = jnp.dot(mix_ref[...], _load_weight(w_ref, copy_ref), preferred_element_type=F32)
    if n_k == 1:
        o_ref[...] = x_ref[...] + _rms(part, g_ref[...])
    else:
        _accumulate(o_ref, part, k)

        @pl.when(k == n_k - 1)
        def _():
            o_ref[...] = x_ref[...] + _rms(o_ref[...], g_ref[...])


def out_proj(mix, w, x, g, *, tm, tk=None, sides=()):
    rows, kdim = mix.shape
    d = x.shape[1]
    tm = _row_tile(rows, tm)
    tk = kdim if tk is None else tk
    assert kdim % tk == 0
    w_arg, w_spec, copy_spec, copy_shape = _weight_in(w, (tk, d), lambda i, k: (k, 0))
    emit = copy_spec is not None
    assert not emit or rows == tm, "a bf16 copy needs every weight tile visited exactly once"
    if not emit and tk == kdim:
        w_spec = _resident_spec((tk, d), lambda i, k: (0, 0))
    out_specs = [pl.BlockSpec((tm, d), lambda i, k: (i, 0))]
    out_shape = [jax.ShapeDtypeStruct((rows, d), F32)]
    if emit:
        out_specs.append(copy_spec)
        out_shape.append(copy_shape)
    n_k = kdim // tk
    side_args, side_in, side_out, side_shapes = _side_casts(
        sides, (rows // tm) * n_k, lambda i, k: i * n_k + k)
    out = pl.pallas_call(
        functools.partial(_out_proj_kernel, n_k=n_k, emit=emit, n_side=len(sides)),
        grid=(rows // tm, n_k),
        in_specs=[
            pl.BlockSpec((tm, tk), lambda i, k: (i, k)),
            w_spec,
            pl.BlockSpec((tm, d), lambda i, k: (i, 0)),
            pl.BlockSpec((1, d), lambda i, k: (0, 0)),
        ] + side_in,
        out_specs=out_specs + side_out,
        out_shape=out_shape + side_shapes,
        compiler_params=_params(),
        name="out_proj",
    )(mix, w_arg, x, g.reshape(1, d), *side_args)
    n_main = 2 if emit else 1
    return out[0], (out[1] if emit else w), list(out[n_main:])


def _ffn_kernel(x_ref, gpre_ref, wa_ref, wg_ref, wconv_ref, wd_ref, gpost_ref, hist_ref,
                o_ref, state_ref, *rest, n_seq, blocks_per_seq, sub, emit_up, emit_down):
    (wa_copy_ref, wg_copy_ref), rest = (rest[:2], rest[2:]) if emit_up else ((None, None), rest)
    wd_copy_ref, rest = (rest[0], rest[1:]) if emit_down else (None, rest)
    h_ref, carry_ref = rest
    i = pl.program_id(0)
    j = pl.program_id(1)
    last_j = pl.num_programs(1) - 1
    tm = x_ref.shape[0]
    tf = wa_ref.shape[1]
    rows_per_seq = tm // n_seq

    @pl.when(j == 0)
    def _():
        def start(rows):
            x = x_ref[rows, :]
            h_ref[rows, :] = _rms(x, gpre_ref[...]).astype(BF16)
            o_ref[rows, :] = jnp.zeros_like(x)

        _for_row_chunks(tm, start)

    if blocks_per_seq > 1:
        @pl.when(i % blocks_per_seq == 0)
        def _():
            carry_ref[j] = hist_ref[0]

    h = h_ref[...]
    assert rows_per_seq & (rows_per_seq - 1) == 0
    t = jnp.bitwise_and(lax.broadcasted_iota(jnp.int32, (tm, 1), 0), rows_per_seq - 1)
    groups = [slice(s * sub, (s + 1) * sub) for s in range(tf // sub)]
    wa, wg = ((_load_weight(wa_ref, wa_copy_ref), _load_weight(wg_ref, wg_copy_ref)) if emit_up
              else (wa_ref, wg_ref))
    wd = _load_weight(wd_ref, wd_copy_ref) if emit_down else wd_ref
    pending = None
    for cols in groups:
        a = jnp.dot(h, wa[:, cols], preferred_element_type=F32)
        val = jnp.dot(h, wg[:, cols], preferred_element_type=F32)
        if pending is not None:
            o_ref[...] += jnp.dot(pending[0], wd[pending[1], :], preferred_element_type=F32)
        if blocks_per_seq > 1:
            hist0 = carry_ref[j, 0:1, cols]
            hist1 = carry_ref[j, 1:2, cols]
        elif n_seq == 1:
            hist0 = hist_ref[0, 0:1, cols]
            hist1 = hist_ref[0, 1:2, cols]
        else:
            hist = hist_ref[:, :, cols]
            hist0 = jnp.broadcast_to(hist[:, 0:1, :], (n_seq, rows_per_seq, sub)).reshape(tm, sub)
            hist1 = jnp.broadcast_to(hist[:, 1:2, :], (n_seq, rows_per_seq, sub)).reshape(tm, sub)
        prev1, prev2 = pltpu.roll(a, 1, axis=0), pltpu.roll(a, 2, axis=0)
        if n_seq == 1:
            t8 = t[0:SUBLANES]
            top1 = jnp.where(t8 == 0, hist1, prev1[0:SUBLANES])
            top2 = jnp.where(t8 == 0, hist0, jnp.where(t8 == 1, hist1, prev2[0:SUBLANES]))
            prev1 = jnp.concatenate([top1, prev1[SUBLANES:]], axis=0)
            prev2 = jnp.concatenate([top2, prev2[SUBLANES:]], axis=0)
        else:
            prev1 = jnp.where(t == 0, hist1, prev1)
            prev2 = jnp.where(t == 0, hist0, jnp.where(t == 1, hist1, prev2))
        w = wconv_ref[:, cols]
        conv = prev2 * w[0:1] + prev1 * w[1:2] + a * w[2:3]
        pending = ((_gelu_tanh(conv) * val).astype(BF16), cols)
        tail = a.reshape(n_seq, rows_per_seq, sub)[:, rows_per_seq - 2:, :]
        state_ref[j, :, :, cols] = tail
        if blocks_per_seq > 1:
            carry_ref[j, :, cols] = tail[0]
    o_ref[...] += jnp.dot(pending[0], wd[pending[1], :], preferred_element_type=F32)

    @pl.when(j == last_j)
    def _():
        def finish(rows):
            o_ref[rows, :] = x_ref[rows, :] + _rms(o_ref[rows, :], gpost_ref[...])

        _for_row_chunks(tm, finish)


def conv_ffn(x, g_pre, w_up, w_conv, w_down, g_post, hist, *, seq_len, tm, tf):
    rows, d = x.shape
    d_ff = w_down.shape[0]
    tm = _row_tile(rows, tm)
    assert d_ff % tf == 0 and tf % MXU_COLS == 0
    n_ff = d_ff // tf
    emit_up = isinstance(w_up, LayerWeight)
    emit_down = isinstance(w_down, LayerWeight)
    assert not (emit_up or emit_down) or rows == tm, "a bf16 copy needs every weight tile visited exactly once"
    gate_tile, value_tile = (lambda i, j: (0, j)), (lambda i, j: (0, j + n_ff))
    copy_specs, copy_shapes = [], []
    if emit_up:
        up_args = (w_up.stack, w_up.stack)
        up_specs = [w_up.spec((d, tf), gate_tile), w_up.spec((d, tf), value_tile)]
        copy_specs += [pl.BlockSpec((d, tf), gate_tile)] * 2
        copy_shapes += [jax.ShapeDtypeStruct((d, d_ff), BF16)] * 2
    elif isinstance(w_up, tuple):
        up_args = w_up
        up_specs = [pl.BlockSpec((d, tf), gate_tile)] * 2
    else:
        up_args = (w_up, w_up)
        up_specs = [pl.BlockSpec((d, tf), gate_tile), pl.BlockSpec((d, tf), value_tile)]
    wd_arg, wd_spec, wd_copy_spec, wd_copy_shape = _weight_in(w_down, (tf, d), lambda i, j: (j, 0))
    if emit_down:
        copy_specs.append(wd_copy_spec)
        copy_shapes.append(wd_copy_shape)
    if seq_len >= tm:
        assert seq_len % tm == 0
        n_seq, blocks_per_seq = 1, seq_len // tm
        seq_of = lambda i: i // blocks_per_seq
    else:
        assert tm % seq_len == 0
        n_seq, blocks_per_seq = tm // seq_len, 1
        seq_of = lambda i: i
    kern = functools.partial(_ffn_kernel, n_seq=n_seq, blocks_per_seq=blocks_per_seq, sub=MXU_COLS,
                             emit_up=emit_up, emit_down=emit_down)
    out = pl.pallas_call(
        kern,
        grid=(rows // tm, n_ff),
        in_specs=[
            (_resident_spec if rows == tm else pl.BlockSpec)((tm, d), lambda i, j: (i, 0)),
            pl.BlockSpec((1, d), lambda i, j: (0, 0)),
            up_specs[0],
            up_specs[1],
            pl.BlockSpec((3, tf), lambda i, j: (0, j)),
            wd_spec,
            pl.BlockSpec((1, d), lambda i, j: (0, 0)),
            pl.BlockSpec((n_seq, 2, tf), lambda i, j: (seq_of(i), 0, j)),
        ],
        out_specs=[
            pl.BlockSpec((tm, d), lambda i, j: (i, 0)),
            pl.BlockSpec((n_ff, n_seq, 2, tf), lambda i, j: (0, seq_of(i), 0, 0)),
        ] + copy_specs,
        out_shape=[
            jax.ShapeDtypeStruct((rows, d), F32),
            jax.ShapeDtypeStruct((n_ff, rows // seq_len, 2, tf), F32),
        ] + copy_shapes,
        scratch_shapes=[pltpu.VMEM((tm, d), BF16), pltpu.VMEM((n_ff, 2, tf), F32)],
        compiler_params=_params(),
        name="conv_ffn",
    )(x, g_pre.reshape(1, d), *up_args, w_conv, wd_arg, g_post.reshape(1, d), hist)
    state = jnp.moveaxis(out[1], 0, 2).reshape(rows // seq_len, 2, d_ff)
    copies = list(out[2:])
    w_up_q = (copies.pop(0), copies.pop(0)) if emit_up else w_up
    w_down_q = copies.pop(0) if emit_down else w_down
    return out[0], state, (w_up_q, w_down_q)


def _ple_kernel(x_ref, p_ref, gpre_ref, wgate_ref, wproj_ref, gpost_ref, o_ref, *rest, n_k, emit):
    wgate_copy_ref, wproj_copy_ref = rest[:2] if emit else (None, None)
    h_ref = rest[-1] if n_k > 1 else None
    k = pl.program_id(1)
    tk = wgate_ref.shape[0]

    def finish(pre):
        emb = jnp.dot(p_ref[...].astype(BF16), _load_weight(wproj_ref, wproj_copy_ref),
                      preferred_element_type=F32)
        o_ref[...] = x_ref[...] + _rms(jax.nn.sigmoid(pre) * emb, gpost_ref[...])

    if n_k == 1:
        h = _rms(x_ref[...], gpre_ref[...]).astype(BF16)
        finish(jnp.dot(h, _load_weight(wgate_ref, wgate_copy_ref), preferred_element_type=F32))
    else:
        @pl.when(k == 0)
        def _():
            h = _rms(x_ref[...], gpre_ref[...]).astype(BF16)
            for kk in range(n_k):
                h_ref[kk] = h[:, kk * tk:(kk + 1) * tk]

        _accumulate(o_ref, jnp.dot(h_ref[k], _load_weight(wgate_ref, wgate_copy_ref),
                                   preferred_element_type=F32), k)

        @pl.when(k == n_k - 1)
        def _():
            finish(o_ref[...])


def per_layer_embedding(x, p, layer, g_pre, w_gate, w_proj, g_post, *, tm, tk=None):
    rows, d = x.shape
    pd = p.shape[-1]
    tm = _row_tile(rows, tm)
    tk = d if tk is None else tk
    assert d % tk == 0
    n_k = d // tk
    wg_arg, wg_spec, wg_copy_spec, wg_copy_shape = _weight_in(w_gate, (tk, d), lambda i, k: (k, 0))
    wp_arg, wp_spec, wp_copy_spec, wp_copy_shape = _weight_in(w_proj, (pd, d), lambda i, k: (0, 0))
    emit = wg_copy_spec is not None
    assert emit == (wp_copy_spec is not None)
    assert not emit or rows == tm, "a bf16 copy needs every weight tile visited exactly once"
    if not emit:
        wp_spec = _resident_spec((pd, d), lambda i, k: (0, 0))
        if n_k == 1:
            wg_spec = _resident_spec((tk, d), lambda i, k: (0, 0))
    out_specs = [pl.BlockSpec((tm, d), lambda i, k: (i, 0))]
    out_shape = [jax.ShapeDtypeStruct((rows, d), F32)]
    if emit:
        out_specs += [wg_copy_spec, wp_copy_spec]
        out_shape += [wg_copy_shape, wp_copy_shape]
    out = pl.pallas_call(
        functools.partial(_ple_kernel, n_k=n_k, emit=emit),
        grid=(rows // tm, n_k),
        in_specs=[
            pl.BlockSpec((tm, d), lambda i, k: (i, 0)),
            pl.BlockSpec((None, tm, pd), lambda i, k: (layer, i, 0)),
            pl.BlockSpec((1, d), lambda i, k: (0, 0)),
            wg_spec,
            wp_spec,
            pl.BlockSpec((1, d), lambda i, k: (0, 0)),
        ],
        out_specs=out_specs,
        out_shape=out_shape,
        scratch_shapes=[pltpu.VMEM((n_k, tm, tk), BF16)] if n_k > 1 else [],
        compiler_params=_params(),
        name="per_layer_embedding",
    )(x, p, g_pre.reshape(1, d), wg_arg, wp_arg, g_post.reshape(1, d))
    return (out[0], (out[1], out[2])) if emit else (out[0], (w_gate, w_proj))


def _rope_tables(pos, dim, reps):
    inv = 1.0 / (ROPE_THETA ** (jnp.arange(0, dim, 2, dtype=F32) / dim))
    ang = pos.astype(F32)[:, None] * inv[None, :]
    cos, sin = jnp.cos(ang), jnp.sin(ang)
    return (jnp.tile(jnp.concatenate([cos, cos], axis=-1), (1, reps)),
            jnp.tile(jnp.concatenate([-sin, sin], axis=-1), (1, reps)))


def _mix_stage(i, x, grp, states, wts, prm, cfg, sides=None):
    b, t = grp['b'], grp['t']
    rows = b * t
    j = i // 2
    wq = {}
    sides = dict(sides or {})
    cast = {}
    if i % 2 == 0:
        n_heads, dk, dv = states['ret'].shape[2:]
        names = list(sides)
        proj, wq['w_in'], done = even_in_proj(
            x, prm['norm_mix_pre'][i], wts['w_in'], *grp['even_tabs'], tm=cfg['tm_in'], tn=cfg['tn_even'],
            d_a=prm['w_conv_a'].shape[-1], n_heads=n_heads, dk=dk, dv=dv, sides=[sides.pop(n) for n in names])
        cast.update(zip(names, done))
        mix, c_s, r_s = even_mixer(proj.reshape(b, t, -1), states['conv'][j], states['ret'][j],
                                   prm['w_conv_a'][j], prm['ret_gn'][j], blk=cfg['blk_even'])
        new = dict(conv=c_s, ret=r_s)
    else:
        proj, wq['w_in'] = norm_matmul(x, prm['norm_mix_pre'][i], wts['w_in'], tm=cfg['tm_in'], tn=cfg['tn_odd'])
        lanes = grp['kv_lanes']
        has_cache = states['cache_k'] is not None
        if has_cache:
            ck = states['cache_k'][j].reshape(b, WINDOW, lanes)
            cv = states['cache_v'][j].reshape(b, WINDOW, lanes)
            n_state_rows = t
        else:
            ck = cv = jnp.zeros((b, WINDOW, lanes), F32)
            n_state_rows = cfg['kv_rows']
        mix, p_s, k_s, v_s = odd_mixer(proj.reshape(b, t, -1), *grp['odd_tabs'], states['pool'][j], ck, cv,
                                       prm['w_pool'][j], prm['pool_scale'][j], prm['sinks'][j],
                                       blk=cfg['blk_odd'], pos0=grp['pos0'], has_cache=has_cache,
                                       n_state_rows=n_state_rows)
        kv_shape = (b, n_state_rows, 2, lanes // 2)
        new = dict(pool=p_s, k=k_s.reshape(kv_shape), v=v_s.reshape(kv_shape))
    names = list(sides)
    x, wq['w_out'], done = out_proj(mix.reshape(rows, -1), wts['w_out'], x, prm['norm_mix_post'][i],
                                    tm=cfg['tm_out'], tk=cfg['tk_out'], sides=[sides[n] for n in names])
    cast.update(zip(names, done))
    return x, new, wq, cast


def _ffn_stage(i, x, p, grp, states, wts, prm, cfg):
    wq = {}
    x, ffn_state, (wq['w_up'], wq['w_down']) = conv_ffn(
        x, prm['norm_ffn_pre'][i], wts['w_up'], prm['w_conv_ffn'][i], wts['w_down'], prm['norm_ffn_post'][i],
        states['ffn'][i], seq_len=grp['t'], tm=cfg['tm_ffn'], tf=cfg['tf'])
    x, (wq['w_gate'], wq['w_proj']) = per_layer_embedding(
        x, p, i, prm['norm_ple_pre'][i], wts['w_gate'], wts['w_proj'], prm['norm_ple_post'][i],
        tm=cfg['tm_ple'], tk=cfg['tk_ple'])
    return x, ffn_state, wq


def kernel(x_prompt, x_sample, state_conv, state_ret, state_pool, cache_k, cache_v, state_ffn, p_prompt, p_sample, norm_mix_pre, norm_mix_post, norm_ffn_pre, norm_ffn_post, norm_ple_pre, norm_ple_post, w_in_even, w_conv_a, ret_gn, w_out_even, w_in_odd, w_pool, pool_scale, sinks, w_out_odd, w_up, w_conv_ffn, w_down, w_ple_gate, w_ple_proj):
    depth = norm_mix_pre.shape[0]
    n_even, n_odd = w_in_even.shape[0], w_in_odd.shape[0]
    d = x_prompt.shape[-1]
    d_a = w_conv_a.shape[-1]
    n_heads, dk, dv = state_ret.shape[2:]
    d_c = pool_scale.shape[-1]
    kv_rows, n_kv, hd = cache_k.shape[2:]
    d_ff = w_conv_ffn.shape[-1]
    lanes = n_kv * hd
    dt = x_prompt.dtype

    prm = dict(norm_mix_pre=norm_mix_pre, norm_mix_post=norm_mix_post, norm_ffn_pre=norm_ffn_pre,
               norm_ffn_post=norm_ffn_post, norm_ple_pre=norm_ple_pre, norm_ple_post=norm_ple_post,
               w_conv_a=w_conv_a, ret_gn=ret_gn, w_pool=w_pool.astype(BF16), pool_scale=pool_scale, sinks=sinks,
               w_conv_ffn=w_conv_ffn)

    def group(x, pos0):
        b, t = x.shape[:2]
        pos = pos0 + jnp.arange(t, dtype=jnp.int32)
        return dict(b=b, t=t, pos0=pos0, kv_lanes=lanes,
                    even_tabs=tuple(jnp.tile(tab, (b, 1)) for tab in _rope_tables(pos, dk, 1)),
                    odd_tabs=_rope_tables(pos, hd, lanes // hd))

    bp, sp = x_prompt.shape[:2]
    bs, ts = x_sample.shape[:2]
    grp_p, grp_s = group(x_prompt, 0), group(x_sample, PAST_LEN)
    st_p = dict(conv=jnp.zeros((n_even, bp, 2, d_a), dt), ret=jnp.zeros((n_even, bp, n_heads, dk, dv), dt),
                pool=jnp.zeros((n_odd, bp, POOL_HIST, d_c), dt), cache_k=None, cache_v=None,
                ffn=jnp.zeros((depth, bp, 2, d_ff), dt))
    st_s = dict(conv=state_conv, ret=state_ret, pool=state_pool, cache_k=cache_k, cache_v=cache_v, ffn=state_ffn)
    cfg_s = dict(tm_in=bs * ts, tn_even=1024, tn_odd=w_in_odd.shape[-1] // 2, blk_even=ts, blk_odd=ts,
                 tm_out=bs * ts, tk_out=512, tm_ffn=bs * ts, tf=512, tm_ple=bs * ts, tk_ple=512, kv_rows=kv_rows)
    cfg_p = dict(tm_in=1024, tn_even=1024, tn_odd=w_in_odd.shape[-1] // 2, blk_even=256, blk_odd=128,
                 tm_out=512, tk_out=None, tm_ffn=1024, tf=512, tm_ple=1024, tk_ple=None, kv_rows=kv_rows)
    xp = x_prompt.reshape(bp * sp, d)
    xs = x_sample.reshape(bs * ts, d)
    pp = p_prompt.reshape(depth, bp * sp, -1)
    ps = p_sample.reshape(depth, bs * ts, -1)
    new_p, new_s = [], []
    for i in range(depth):
        j = i // 2
        raw = dict(w_in=LayerWeight(w_in_even if i % 2 == 0 else w_in_odd, j),
                   w_out=LayerWeight(w_out_even if i % 2 == 0 else w_out_odd, j),
                   w_up=LayerWeight(w_up, i), w_down=LayerWeight(w_down, i),
                   w_gate=LayerWeight(w_ple_gate, i), w_proj=LayerWeight(w_ple_proj, i))
        sides = {name: raw[name] for name in (('w_up', 'w_down') if i % 2 == 0 else ('w_down',))}
        xs, st_mix_s, wq, _ = _mix_stage(i, xs, grp_s, st_s, raw, prm, cfg_s)
        xp, st_mix_p, _, cast = _mix_stage(i, xp, grp_p, st_p, wq, prm, cfg_p, sides)
        xs, ffn_s, wq = _ffn_stage(i, xs, ps, grp_s, st_s, {**raw, **cast}, prm, cfg_s)
        xp, ffn_p, _ = _ffn_stage(i, xp, pp, grp_p, st_p, wq, prm, cfg_p)
        new_s.append(dict(st_mix_s, ffn=ffn_s))
        new_p.append(dict(st_mix_p, ffn=ffn_p))

    def stacked(new, key):
        parts = [st[key] for st in new if key in st]
        return parts[0][None] if len(parts) == 1 else jnp.stack(parts)

    outs = [xp.reshape(bp, sp, d), xs.reshape(bs, ts, d)]
    for key in ('conv', 'ret', 'pool', 'k', 'v', 'ffn'):
        outs += [stacked(new_p, key), stacked(new_s, key)]
    return tuple(outs)
```

```python
import functools
import math

import jax
import jax.numpy as jnp
from jax import lax
from jax.experimental import pallas as pl
from jax.experimental.pallas import tpu as pltpu

CHUNK = 64
WINDOW = 128
PAST_LEN = 4096
EPS = 1e-6
ROPE_THETA = 10000.0
NEG_INF = -1e30
POOL_WINDOWS = (2, 4, 8, 16)
POOL_HIST = max(POOL_WINDOWS) - 1

V7X_VMEM_BYTES = 64 * 1024 * 1024
VMEM_LIMIT_BYTES = V7X_VMEM_BYTES - 6 * 1024 * 1024
MXU_COLS = 256
SUBLANES = 8

F32 = jnp.float32
BF16 = jnp.bfloat16


def _params():
    return pltpu.CompilerParams(vmem_limit_bytes=VMEM_LIMIT_BYTES)


def _rms(x, g):
    return x * lax.rsqrt(jnp.mean(x * x, axis=-1, keepdims=True) + EPS) * g


def _gelu_tanh(x):
    c1 = math.sqrt(2.0 / math.pi)
    half = 0.5 * x
    return half + half * jnp.tanh(x * (c1 + (c1 * 0.044715) * (x * x)))


ROW_CHUNK = 256


def _for_row_chunks(n_rows, fn):
    if n_rows <= ROW_CHUNK or n_rows % ROW_CHUNK:
        fn(slice(0, n_rows))
        return

    def body(r, carry):
        fn(pl.ds(pl.multiple_of(r * ROW_CHUNK, ROW_CHUNK), ROW_CHUNK))
        return carry

    lax.fori_loop(0, n_rows // ROW_CHUNK, body, 0)


def _resident_spec(block, index_map):
    return pl.BlockSpec(block, index_map, pipeline_mode=pl.Buffered(1))


def _row_tile(rows, want):
    t = min(rows, want)
    assert rows % t == 0, (rows, t)
    return t


class LayerWeight:
    def __init__(self, stack, layer):
        self.stack, self.layer = stack, layer
        self.shape = stack.shape[1:]

    def spec(self, block, index_map):
        layer = self.layer
        return pl.BlockSpec((None,) + block, lambda *g: (layer,) + index_map(*g))


def _weight_in(w, block, index_map):
    if isinstance(w, LayerWeight):
        return (w.stack, w.spec(block, index_map), pl.BlockSpec(block, index_map),
                jax.ShapeDtypeStruct(w.shape, BF16))
    return w, pl.BlockSpec(block, index_map), None, None


def _load_weight(w_ref, copy_ref):
    w = w_ref[...]
    if copy_ref is not None:
        w = w.astype(BF16)
        copy_ref[...] = w
    return w


BF16_SUBLANES = 16


def _side_casts(sides, n_steps, step_of):
    args, in_specs, out_specs, out_shapes = [], [], [], []
    for side in sides:
        rows, cols = side.shape
        n_slabs = max(n for n in range(1, n_steps + 1) if rows % n == 0 and (rows // n) % BF16_SUBLANES == 0)
        slab = rows // n_slabs
        index_map = lambda *g, last=n_slabs - 1: (jnp.minimum(step_of(*g), last), 0)
        args.append(side.stack)
        in_specs.append(side.spec((slab, cols), index_map))
        out_specs.append(pl.BlockSpec((slab, cols), index_map))
        out_shapes.append(jax.ShapeDtypeStruct((rows, cols), BF16))
    return args, in_specs, out_specs, out_shapes


def _run_side_casts(side_refs):
    n = len(side_refs) // 2
    for src, dst in zip(side_refs[:n], side_refs[n:]):
        dst[...] = src[...].astype(BF16)


def _norm_matmul_kernel(x_ref, g_ref, w_ref, *rest, emit):
    o_ref, copy_ref, h_ref = rest if emit else (rest[0], None, rest[1])

    @pl.when(pl.program_id(1) == 0)
    def _():
        def start(rows):
            h_ref[rows, :] = _rms(x_ref[rows, :], g_ref[...]).astype(BF16)

        _for_row_chunks(x_ref.shape[0], start)

    w = _load_weight(w_ref, copy_ref)
    o_ref[...] = jnp.dot(h_ref[...], w, preferred_element_type=F32)


def norm_matmul(x, g, w, *, tm, tn):
    rows, d = x.shape
    n = w.shape[1]
    tm = _row_tile(rows, tm)
    assert n % tn == 0
    w_arg, w_spec, copy_spec, copy_shape = _weight_in(w, (d, tn), lambda i, j: (0, j))
    emit = copy_spec is not None
    assert not emit or rows == tm, "a bf16 copy needs every weight tile visited exactly once"
    out_specs = [pl.BlockSpec((tm, tn), lambda i, j: (i, j))]
    out_shape = [jax.ShapeDtypeStruct((rows, n), F32)]
    if emit:
        out_specs.append(copy_spec)
        out_shape.append(copy_shape)
    out = pl.pallas_call(
        functools.partial(_norm_matmul_kernel, emit=emit),
        grid=(rows // tm, n // tn),
        in_specs=[
            pl.BlockSpec((tm, d), lambda i, j: (i, 0)),
            pl.BlockSpec((1, d), lambda i, j: (0, 0)),
            w_spec,
        ],
        out_specs=out_specs,
        out_shape=out_shape,
        scratch_shapes=[pltpu.VMEM((tm, d), BF16)],
        compiler_params=_params(),
        name="norm_matmul",
    )(x, g.reshape(1, d), w_arg)
    return (out[0], out[1]) if emit else (out[0], w)


def _even_in_proj_kernel(x_ref, g_ref, w_ref, cos_ref, sin_ref, *rest, emit, n_side, kinds, dk, k_scale):
    side_in, rest = rest[:n_side], rest[n_side:]
    o_ref, rest = rest[0], rest[1:]
    copy_ref, rest = (rest[0], rest[1:]) if emit else (None, rest)
    side_out, (h_ref,) = rest[:n_side], rest[n_side:]
    _run_side_casts(side_in + side_out)
    j = pl.program_id(1)
    tn = w_ref.shape[1]

    @pl.when(j == 0)
    def _():
        def start(rows):
            h_ref[rows, :] = _rms(x_ref[rows, :], g_ref[...]).astype(BF16)

        _for_row_chunks(x_ref.shape[0], start)

    def project():
        return jnp.dot(h_ref[...], _load_weight(w_ref, copy_ref), preferred_element_type=F32)

    def rope(acc, scale):
        cosb, sinb = cos_ref[...], sin_ref[...]
        for hh in range(tn // dk):
            seg = acc[:, hh * dk:(hh + 1) * dk]
            out = seg * cosb + pltpu.roll(seg, dk // 2, axis=1) * sinb
            if scale != 1.0:
                out = out * scale
            o_ref[:, hh * dk:(hh + 1) * dk] = out.astype(BF16)

    def finish(kind):
        acc = project()
        if kind == 'rope':
            rope(acc, 1.0)
        elif kind == 'rope_scaled':
            rope(acc, k_scale)
        elif kind == 'plain':
            o_ref[...] = acc.astype(BF16)
        else:
            assert kind == 'silu'
            o_ref[...] = (acc * jax.nn.sigmoid(acc)).astype(BF16)

    for kind in dict.fromkeys(kinds):
        tiles = [t for t, k in enumerate(kinds) if k == kind]
        here = functools.reduce(jnp.logical_or, [j == t for t in tiles])
        pl.when(here)(functools.partial(finish, kind))


def even_in_proj(x, g, w, cosb, sinb, *, tm, tn, d_a, n_heads, dk, dv, sides=()):
    rows, d = x.shape
    n = w.shape[1]
    tm = _row_tile(rows, tm)
    hk, hv = n_heads * dk, n_heads * dv
    assert n == 3 * d_a + 2 * hk + 2 * hv and tn % dk == 0
    assert all(part % tn == 0 for part in (d_a, hk, hv))
    kinds = (('plain',) * (3 * d_a // tn) + ('rope',) * (hk // tn) + ('rope_scaled',) * (hk // tn)
             + ('plain',) * (hv // tn) + ('silu',) * (hv // tn))
    w_arg, w_spec, copy_spec, copy_shape = _weight_in(w, (d, tn), lambda i, j: (0, j))
    emit = copy_spec is not None
    assert not emit or rows == tm, "a bf16 copy needs every weight tile visited exactly once"
    out_specs = [pl.BlockSpec((tm, tn), lambda i, j: (i, j))]
    out_shape = [jax.ShapeDtypeStruct((rows, n), BF16)]
    if emit:
        out_specs.append(copy_spec)
        out_shape.append(copy_shape)
    n_col = n // tn
    side_args, side_in, side_out, side_shapes = _side_casts(
        sides, (rows // tm) * n_col, lambda i, j: i * n_col + j)
    kern = functools.partial(_even_in_proj_kernel, emit=emit, n_side=len(sides), kinds=kinds, dk=dk,
                             k_scale=dk ** -0.5)
    out = pl.pallas_call(
        kern,
        grid=(rows // tm, n_col),
        in_specs=[
            pl.BlockSpec((tm, d), lambda i, j: (i, 0)),
            pl.BlockSpec((1, d), lambda i, j: (0, 0)),
            w_spec,
            pl.BlockSpec((tm, dk), lambda i, j: (i, 0)),
            pl.BlockSpec((tm, dk), lambda i, j: (i, 0)),
        ] + side_in,
        out_specs=out_specs + side_out,
        out_shape=out_shape + side_shapes,
        scratch_shapes=[pltpu.VMEM((tm, d), BF16)],
        compiler_params=_params(),
        name="even_in_proj",
    )(x, g.reshape(1, d), w_arg, cosb, sinb, *side_args)
    n_main = 2 if emit else 1
    return out[0], (out[1] if emit else w), list(out[n_main:])


def _even_mixer_kernel(p_ref, chist_ref, rstate_ref, wconv_ref, gn_ref,
                       mix_ref, cstate_ref, rout_ref, carry_ref, s_ref, dmat_ref, qdec_ref, kdec_ref,
                       *, n_heads, dk, dv, d_a):
    q0 = 3 * d_a
    k0 = q0 + n_heads * dk
    v0 = k0 + n_heads * dk
    g0 = v0 + n_heads * dv
    c = pl.program_id(1)
    last = pl.num_programs(1) - 1
    blk = p_ref.shape[0]

    @pl.when(c == 0)
    def _():
        carry_ref[...] = chist_ref[...]
        s_ref[...] = rstate_ref[...]

    u = p_ref[:, 2 * d_a:3 * d_a].astype(F32) * p_ref[:, 0:d_a].astype(F32)
    row = lax.broadcasted_iota(jnp.int32, u.shape, 0)
    h0 = carry_ref[0:1, :]
    h1 = carry_ref[1:2, :]
    prev1 = jnp.where(row == 0, h1, pltpu.roll(u, 1, axis=0))
    prev2 = jnp.where(row == 0, h0, jnp.where(row == 1, h1, pltpu.roll(u, 2, axis=0)))
    w = wconv_ref[...]
    conv = prev2 * w[0:1] + prev1 * w[1:2] + u * w[2:3]
    mix_ref[:, 0:d_a] = (p_ref[:, d_a:2 * d_a].astype(F32) * conv).astype(BF16)
    tail = u[blk - 2:blk, :]
    carry_ref[...] = tail

    @pl.when(c == last)
    def _():
        cstate_ref[...] = tail

    log_gamma = [math.log1p(-(2.0 ** (-5.0 - h))) for h in range(n_heads)]

    @pl.when((pl.program_id(0) == 0) & (c == 0))
    def _():
        rel = (lax.broadcasted_iota(jnp.int32, (blk, blk), 0)
               - lax.broadcasted_iota(jnp.int32, (blk, blk), 1)).astype(F32)
        ii = lax.broadcasted_iota(jnp.int32, (blk, dk), 0).astype(F32)
        for h, lg in enumerate(log_gamma):
            dmat_ref[h] = jnp.where(rel >= 0, jnp.exp(jnp.maximum(rel, 0.0) * lg), 0.0)
            qdec_ref[h] = jnp.exp((ii + 1.0) * lg)
            kdec_ref[h] = jnp.exp((blk - 1.0 - ii) * lg)

    stage = []
    for h in range(n_heads):
        qr = p_ref[:, q0 + h * dk:q0 + (h + 1) * dk]
        kr = p_ref[:, k0 + h * dk:k0 + (h + 1) * dk]
        vb = p_ref[:, v0 + h * dv:v0 + (h + 1) * dv]
        sc = lax.dot_general(qr, kr, (((1,), (1,)), ((), ())), preferred_element_type=F32)
        st = s_ref[h]
        inter = jnp.dot((qr.astype(F32) * qdec_ref[h]).astype(BF16), st.astype(BF16),
                        preferred_element_type=F32)
        kv = lax.dot_general((kr.astype(F32) * kdec_ref[h]).astype(BF16), vb, (((0,), (0,)), ((), ())),
                             preferred_element_type=F32)
        s_ref[h] = math.exp(blk * log_gamma[h]) * st + kv
        stage.append((sc, inter, vb))
    for h, (sc, inter, vb) in enumerate(stage):
        o = jnp.dot((sc * dmat_ref[h]).astype(BF16), vb, preferred_element_type=F32) + inter
        on = o * lax.rsqrt(jnp.mean(o * o, axis=-1, keepdims=True) + EPS) * gn_ref[h:h + 1, :]
        gate = p_ref[:, g0 + h * dv:g0 + (h + 1) * dv].astype(F32)
        mix_ref[:, d_a + h * dv:d_a + (h + 1) * dv] = (on * gate).astype(BF16)

    @pl.when(c == last)
    def _():
        rout_ref[...] = s_ref[...]


def even_mixer(p, conv_hist, ret_state, w_conv, gn, *, blk):
    b, t, width = p.shape
    n_heads, dk, dv = ret_state.shape[1:]
    d_a = w_conv.shape[1]
    hv = n_heads * dv
    assert width == 3 * d_a + 2 * n_heads * dk + 2 * hv
    blk = _row_tile(t, blk)
    nc = t // blk
    kern = functools.partial(_even_mixer_kernel, n_heads=n_heads, dk=dk, dv=dv, d_a=d_a)
    return pl.pallas_call(
        kern,
        grid=(b, nc),
        in_specs=[
            pl.BlockSpec((None, blk, width), lambda bi, ci: (bi, ci, 0)),
            pl.BlockSpec((None, 2, d_a), lambda bi, ci: (bi, 0, 0)),
            pl.BlockSpec((None, n_heads, dk, dv), lambda bi, ci: (bi, 0, 0, 0)),
            pl.BlockSpec((3, d_a), lambda bi, ci: (0, 0)),
            pl.BlockSpec((n_heads, dv), lambda bi, ci: (0, 0)),
        ],
        out_specs=[
            pl.BlockSpec((None, blk, d_a + hv), lambda bi, ci: (bi, ci, 0)),
            pl.BlockSpec((None, 2, d_a), lambda bi, ci: (bi, 0, 0)),
            pl.BlockSpec((None, n_heads, dk, dv), lambda bi, ci: (bi, 0, 0, 0)),
        ],
        out_shape=[
            jax.ShapeDtypeStruct((b, t, d_a + hv), BF16),
            jax.ShapeDtypeStruct((b, 2, d_a), F32),
            jax.ShapeDtypeStruct((b, n_heads, dk, dv), F32),
        ],
        scratch_shapes=[pltpu.VMEM((2, d_a), F32), pltpu.VMEM((n_heads, dk, dv), F32),
                        pltpu.VMEM((n_heads, blk, blk), F32), pltpu.VMEM((n_heads, blk, dk), F32),
                        pltpu.VMEM((n_heads, blk, dk), F32)],
        compiler_params=_params(),
        name="even_mixer",
    )(p, conv_hist, ret_state, w_conv, gn)


def _odd_mixer_kernel(sinks_ref, p_ref, cos_ref, sin_ref,
                      phist_ref, ck_ref, cv_ref, wpool_ref, pscale_ref,
                      mix_ref, pstate_ref, kstate_ref, vstate_ref,
                      ext_ref, kbuf_ref, vbuf_ref,
                      *, d_c, n_q_heads, n_kv_heads, hd, pos0, has_cache, n_state_rows):
    c = pl.program_id(1)
    last = pl.num_programs(1) - 1
    blk = p_ref.shape[0]
    hist_rows = POOL_HIST + 1
    d_cg = d_c // len(POOL_WINDOWS)
    lanes = 2 * hd
    q0 = d_c
    k0 = q0 + n_q_heads * hd
    v0 = k0 + lanes

    @pl.when(c == 0)
    def _():
        ext_ref[0:1, :] = jnp.zeros((1, d_c), F32)
        ext_ref[1:hist_rows, :] = phist_ref[...]
        kbuf_ref[0:WINDOW, :] = ck_ref[...]
        vbuf_ref[0:WINDOW, :] = cv_ref[...]

    u = p_ref[:, 0:d_c]
    ext_ref[hist_rows:hist_rows + blk, :] = u
    pos = pos0 + c * blk + lax.broadcasted_iota(jnp.int32, (blk, 1), 0)
    for gi, win in enumerate(POOL_WINDOWS):
        cols = slice(gi * d_cg, (gi + 1) * d_cg)
        s = ext_ref[:, cols]
        span = 1
        while span < win:
            s = s + pltpu.roll(s, span, axis=0)
            span *= 2
        cnt = jnp.minimum(pos + 1, win).astype(F32)
        d = s[hist_rows:, :] / cnt - u[:, cols]
        y = jnp.dot(d.astype(BF16), wpool_ref[gi], preferred_element_type=F32)
        mix_ref[:, cols] = (y * pscale_ref[:, cols]).astype(BF16)
    tail = ext_ref[blk:blk + hist_rows, :]
    ext_ref[0:hist_rows, :] = tail

    cos4 = cos_ref[...]
    sin4 = sin_ref[...]
    lane = lax.broadcasted_iota(jnp.int32, (1, lanes), 1)
    first_half = jnp.bitwise_and(lane, hd - 1) < (hd // 2)
    left = lane < hd

    def rope(x):
        rot = jnp.where(first_half, pltpu.roll(x, lanes - hd // 2, axis=1), pltpu.roll(x, hd // 2, axis=1))
        return x * cos4 + rot * sin4

    kr = rope(p_ref[:, k0:k0 + lanes])
    vv = p_ref[:, v0:v0 + lanes]
    kbuf_ref[WINDOW:WINDOW + blk, :] = kr
    vbuf_ref[WINDOW:WINDOW + blk, :] = vv
    kall = kbuf_ref[...]
    vall = vbuf_ref[...]
    kswap = pltpu.roll(kall, hd, axis=1)
    vswap = pltpu.roll(vall, hd, axis=1)
    nk = WINDOW + blk
    chunk_shift = CHUNK.bit_length() - 1
    qi = jnp.right_shift(lax.broadcasted_iota(jnp.int32, (blk, 1), 0), chunk_shift)
    kj = lax.broadcasted_iota(jnp.int32, (1, nk), 1)
    kc = jnp.right_shift(kj, chunk_shift) - WINDOW // CHUNK
    ok = (kc <= qi) & (kc >= qi - WINDOW // CHUNK)
    if not has_cache:
        ok = ok & ((kj >= WINDOW) | (c > 0))
    scale = hd ** -0.5
    group = n_q_heads // n_kv_heads
    zeros = jnp.zeros_like(vall)
    kv_ops = [
        (jnp.where(left, kall, kswap).astype(BF16), jnp.where(left, vall, zeros).astype(BF16),
         jnp.where(left, zeros, vswap).astype(BF16)),
        (jnp.where(left, kswap, kall).astype(BF16), jnp.where(left, vswap, zeros).astype(BF16),
         jnp.where(left, zeros, vall).astype(BF16)),
    ]
    scores = []
    for head in range(n_q_heads):
        kvh = head // group
        if head % 2 == 0:
            qr = rope(p_ref[:, q0 + (head // 2) * lanes:q0 + (head // 2 + 1) * lanes])
        qh = jnp.where(left if head % 2 == 0 else jnp.logical_not(left), qr, 0.0).astype(BF16)
        scores.append(lax.dot_general(qh, kv_ops[kvh][0], (((1,), (1,)), ((), ())),
                                      preferred_element_type=F32))
    probs = []
    for head, sc in enumerate(scores):
        sc = jnp.where(ok, sc * scale, NEG_INF)
        sink = sinks_ref[head]
        m = jnp.maximum(jnp.max(sc, axis=-1, keepdims=True), sink)
        e = jnp.exp(sc - m)
        den = jnp.sum(e, axis=-1, keepdims=True) + jnp.exp(sink - m)
        probs.append((e / den).astype(BF16))
    for pair in range(n_q_heads // 2):
        kvh = (2 * pair) // group
        assert (2 * pair + 1) // group == kvh
        out = (jnp.dot(probs[2 * pair], kv_ops[kvh][1], preferred_element_type=F32)
               + jnp.dot(probs[2 * pair + 1], kv_ops[kvh][2], preferred_element_type=F32))
        mix_ref[:, d_c + pair * lanes:d_c + (pair + 1) * lanes] = out.astype(BF16)
    ktail = kbuf_ref[blk:blk + WINDOW, :]
    vtail = vbuf_ref[blk:blk + WINDOW, :]
    kbuf_ref[0:WINDOW, :] = ktail
    vbuf_ref[0:WINDOW, :] = vtail

    @pl.when(c == last)
    def _():
        pstate_ref[...] = ext_ref[1:hist_rows, :]
        kstate_ref[...] = kbuf_ref[nk - n_state_rows:nk, :]
        vstate_ref[...] = vbuf_ref[nk - n_state_rows:nk, :]


def odd_mixer(p, cos4, sin4, pool_hist, cache_k, cache_v, w_pool, pool_scale, sinks, *,
              blk, pos0, has_cache, n_state_rows):
    b, t, width = p.shape
    d_c = pool_scale.shape[-1]
    n_q_heads = sinks.shape[-1]
    lanes = cache_k.shape[-1]
    n_kv_heads = 2
    hd = lanes // n_kv_heads
    d_q = n_q_heads * hd
    assert d_q == d_c and d_c % lanes == 0 and width == d_c + d_q + 2 * lanes
    blk = _row_tile(t, blk)
    nc = t // blk
    assert blk % CHUNK == 0 or nc == 1
    kern = functools.partial(_odd_mixer_kernel, d_c=d_c, n_q_heads=n_q_heads, n_kv_heads=n_kv_heads,
                             hd=hd, pos0=pos0, has_cache=has_cache, n_state_rows=n_state_rows)
    return pl.pallas_call(
        kern,
        grid=(b, nc),
        in_specs=[
            pl.BlockSpec(memory_space=pltpu.SMEM),
            pl.BlockSpec((None, blk, width), lambda bi, ci: (bi, ci, 0)),
            pl.BlockSpec((blk, lanes), lambda bi, ci: (ci, 0)),
            pl.BlockSpec((blk, lanes), lambda bi, ci: (ci, 0)),
            pl.BlockSpec((None, POOL_HIST, d_c), lambda bi, ci: (bi, 0, 0)),
            pl.BlockSpec((None, WINDOW, lanes), lambda bi, ci: (bi, 0, 0)),
            pl.BlockSpec((None, WINDOW, lanes), lambda bi, ci: (bi, 0, 0)),
            pl.BlockSpec(w_pool.shape, lambda bi, ci: (0, 0, 0)),
            pl.BlockSpec((1, d_c), lambda bi, ci: (0, 0)),
        ],
        out_specs=[
            pl.BlockSpec((None, blk, d_c + d_q), lambda bi, ci: (bi, ci, 0)),
            pl.BlockSpec((None, POOL_HIST, d_c), lambda bi, ci: (bi, 0, 0)),
            pl.BlockSpec((None, n_state_rows, lanes), lambda bi, ci: (bi, 0, 0)),
            pl.BlockSpec((None, n_state_rows, lanes), lambda bi, ci: (bi, 0, 0)),
        ],
        out_shape=[
            jax.ShapeDtypeStruct((b, t, d_c + d_q), BF16),
            jax.ShapeDtypeStruct((b, POOL_HIST, d_c), F32),
            jax.ShapeDtypeStruct((b, n_state_rows, lanes), F32),
            jax.ShapeDtypeStruct((b, n_state_rows, lanes), F32),
        ],
        scratch_shapes=[
            pltpu.VMEM((POOL_HIST + 1 + blk, d_c), F32),
            pltpu.VMEM((WINDOW + blk, lanes), F32),
            pltpu.VMEM((WINDOW + blk, lanes), F32),
        ],
        compiler_params=_params(),
        name="odd_mixer",
    )(sinks, p, cos4, sin4, pool_hist, cache_k, cache_v, w_pool, pool_scale.reshape(1, d_c))


def _accumulate(o_ref, part, k):
    @pl.when(k == 0)
    def _():
        o_ref[...] = part

    @pl.when(k > 0)
    def _():
        o_ref[...] += part


def _out_proj_kernel(mix_ref, w_ref, x_ref, g_ref, *rest, n_k, emit, n_side):
    side_in, rest = rest[:n_side], rest[n_side:]
    o_ref, rest = rest[0], rest[1:]
    copy_ref, side_out = (rest[0], rest[1:]) if emit else (None, rest)
    _run_side_casts(side_in + side_out)
    k = pl.program_id(1)
    part = jnp.dot(mix_ref[...], _load_weight(w_ref, copy_ref), preferred_element_type=F32)
    if n_k == 1:
        o_ref[...] = x_ref[...] + _rms(part, g_ref[...])
    else:
        _accumulate(o_ref, part, k)

        @pl.when(k == n_k - 1)
        def _():
            o_ref[...] = x_ref[...] + _rms(o_ref[...], g_ref[...])


def out_proj(mix, w, x, g, *, tm, tk=None, sides=()):
    rows, kdim = mix.shape
    d = x.shape[1]
    tm = _row_tile(rows, tm)
    tk = kdim if tk is None else tk
    assert kdim % tk == 0
    w_arg, w_spec, copy_spec, copy_shape = _weight_in(w, (tk, d), lambda i, k: (k, 0))
    emit = copy_spec is not None
    assert not emit or rows == tm, "a bf16 copy needs every weight tile visited exactly once"
    if not emit and tk == kdim:
        w_spec = _resident_spec((tk, d), lambda i, k: (0, 0))
    out_specs = [pl.BlockSpec((tm, d), lambda i, k: (i, 0))]
    out_shape = [jax.ShapeDtypeStruct((rows, d), F32)]
    if emit:
        out_specs.append(copy_spec)
        out_shape.append(copy_shape)
    n_k = kdim // tk
    side_args, side_in, side_out, side_shapes = _side_casts(
        sides, (rows // tm) * n_k, lambda i, k: i * n_k + k)
    out = pl.pallas_call(
        functools.partial(_out_proj_kernel, n_k=n_k, emit=emit, n_side=len(sides)),
        grid=(rows // tm, n_k),
        in_specs=[
            pl.BlockSpec((tm, tk), lambda i, k: (i, k)),
            w_spec,
            pl.BlockSpec((tm, d), lambda i, k: (i, 0)),
            pl.BlockSpec((1, d), lambda i, k: (0, 0)),
        ] + side_in,
        out_specs=out_specs + side_out,
        out_shape=out_shape + side_shapes,
        compiler_params=_params(),
        name="out_proj",
    )(mix, w_arg, x, g.reshape(1, d), *side_args)
    n_main = 2 if emit else 1
    return out[0], (out[1] if emit else w), list(out[n_main:])


def _ffn_kernel(x_ref, gpre_ref, wa_ref, wg_ref, wconv_ref, wd_ref, gpost_ref, hist_ref,
                o_ref, state_ref, *rest, n_seq, blocks_per_seq, sub, emit_up, emit_down):
    (wa_copy_ref, wg_copy_ref), rest = (rest[:2], rest[2:]) if emit_up else ((None, None), rest)
    wd_copy_ref, rest = (rest[0], rest[1:]) if emit_down else (None, rest)
    h_ref, carry_ref = rest
    i = pl.program_id(0)
    j = pl.program_id(1)
    last_j = pl.num_programs(1) - 1
    tm = x_ref.shape[0]
    tf = wa_ref.shape[1]
    rows_per_seq = tm // n_seq

    @pl.when(j == 0)
    def _():
        def start(rows):
            x = x_ref[rows, :]
            h_ref[rows, :] = _rms(x, gpre_ref[...]).astype(BF16)
            o_ref[rows, :] = jnp.zeros_like(x)

        _for_row_chunks(tm, start)

    if blocks_per_seq > 1:
        @pl.when(i % blocks_per_seq == 0)
        def _():
            carry_ref[j] = hist_ref[0]

    h = h_ref[...]
    assert rows_per_seq & (rows_per_seq - 1) == 0
    t = jnp.bitwise_and(lax.broadcasted_iota(jnp.int32, (tm, 1), 0), rows_per_seq - 1)
    groups = [slice(s * sub, (s + 1) * sub) for s in range(tf // sub)]
    wa, wg = ((_load_weight(wa_ref, wa_copy_ref), _load_weight(wg_ref, wg_copy_ref)) if emit_up
              else (wa_ref, wg_ref))
    wd = _load_weight(wd_ref, wd_copy_ref) if emit_down else wd_ref
    pending = None
    for cols in groups:
        a = jnp.dot(h, wa[:, cols], preferred_element_type=F32)
        val = jnp.dot(h, wg[:, cols], preferred_element_type=F32)
        if pending is not None:
            o_ref[...] += jnp.dot(pending[0], wd[pending[1], :], preferred_element_type=F32)
        if blocks_per_seq > 1:
            hist0 = carry_ref[j, 0:1, cols]
            hist1 = carry_ref[j, 1:2, cols]
        elif n_seq == 1:
            hist0 = hist_ref[0, 0:1, cols]
            hist1 = hist_ref[0, 1:2, cols]
        else:
            hist = hist_ref[:, :, cols]
            hist0 = jnp.broadcast_to(hist[:, 0:1, :], (n_seq, rows_per_seq, sub)).reshape(tm, sub)
            hist1 = jnp.broadcast_to(hist[:, 1:2, :], (n_seq, rows_per_seq, sub)).reshape(tm, sub)
        prev1, prev2 = pltpu.roll(a, 1, axis=0), pltpu.roll(a, 2, axis=0)
        if n_seq == 1:
            t8 = t[0:SUBLANES]
            top1 = jnp.where(t8 == 0, hist1, prev1[0:SUBLANES])
            top2 = jnp.where(t8 == 0, hist0, jnp.where(t8 == 1, hist1, prev2[0:SUBLANES]))
            prev1 = jnp.concatenate([top1, prev1[SUBLANES:]], axis=0)
            prev2 = jnp.concatenate([top2, prev2[SUBLANES:]], axis=0)
        else:
            prev1 = jnp.where(t == 0, hist1, prev1)
            prev2 = jnp.where(t == 0, hist0, jnp.where(t == 1, hist1, prev2))
        w = wconv_ref[:, cols]
        conv = prev2 * w[0:1] + prev1 * w[1:2] + a * w[2:3]
        pending = ((_gelu_tanh(conv) * val).astype(BF16), cols)
        tail = a.reshape(n_seq, rows_per_seq, sub)[:, rows_per_seq - 2:, :]
        state_ref[j, :, :, cols] = tail
        if blocks_per_seq > 1:
            carry_ref[j, :, cols] = tail[0]
    o_ref[...] += jnp.dot(pending[0], wd[pending[1], :], preferred_element_type=F32)

    @pl.when(j == last_j)
    def _():
        def finish(rows):
            o_ref[rows, :] = x_ref[rows, :] + _rms(o_ref[rows, :], gpost_ref[...])

        _for_row_chunks(tm, finish)


def conv_ffn(x, g_pre, w_up, w_conv, w_down, g_post, hist, *, seq_len, tm, tf):
    rows, d = x.shape
    d_ff = w_down.shape[0]
    tm = _row_tile(rows, tm)
    assert d_ff % tf == 0 and tf % MXU_COLS == 0
    n_ff = d_ff // tf
    emit_up = isinstance(w_up, LayerWeight)
    emit_down = isinstance(w_down, LayerWeight)
    assert not (emit_up or emit_down) or rows == tm, "a bf16 copy needs every weight tile visited exactly once"
    gate_tile, value_tile = (lambda i, j: (0, j)), (lambda i, j: (0, j + n_ff))
    copy_specs, copy_shapes = [], []
    if emit_up:
        up_args = (w_up.stack, w_up.stack)
        up_specs = [w_up.spec((d, tf), gate_tile), w_up.spec((d, tf), value_tile)]
        copy_specs += [pl.BlockSpec((d, tf), gate_tile)] * 2
        copy_shapes += [jax.ShapeDtypeStruct((d, d_ff), BF16)] * 2
    elif isinstance(w_up, tuple):
        up_args = w_up
        up_specs = [pl.BlockSpec((d, tf), gate_tile)] * 2
    else:
        up_args = (w_up, w_up)
        up_specs = [pl.BlockSpec((d, tf), gate_tile), pl.BlockSpec((d, tf), value_tile)]
    wd_arg, wd_spec, wd_copy_spec, wd_copy_shape = _weight_in(w_down, (tf, d), lambda i, j: (j, 0))
    if emit_down:
        copy_specs.append(wd_copy_spec)
        copy_shapes.append(wd_copy_shape)
    if seq_len >= tm:
        assert seq_len % tm == 0
        n_seq, blocks_per_seq = 1, seq_len // tm
        seq_of = lambda i: i // blocks_per_seq
    else:
        assert tm % seq_len == 0
        n_seq, blocks_per_seq = tm // seq_len, 1
        seq_of = lambda i: i
    kern = functools.partial(_ffn_kernel, n_seq=n_seq, blocks_per_seq=blocks_per_seq, sub=MXU_COLS,
                             emit_up=emit_up, emit_down=emit_down)
    out = pl.pallas_call(
        kern,
        grid=(rows // tm, n_ff),
        in_specs=[
            (_resident_spec if rows == tm else pl.BlockSpec)((tm, d), lambda i, j: (i, 0)),
            pl.BlockSpec((1, d), lambda i, j: (0, 0)),
            up_specs[0],
            up_specs[1],
            pl.BlockSpec((3, tf), lambda i, j: (0, j)),
            wd_spec,
            pl.BlockSpec((1, d), lambda i, j: (0, 0)),
            pl.BlockSpec((n_seq, 2, tf), lambda i, j: (seq_of(i), 0, j)),
        ],
        out_specs=[
            pl.BlockSpec((tm, d), lambda i, j: (i, 0)),
            pl.BlockSpec((n_ff, n_seq, 2, tf), lambda i, j: (0, seq_of(i), 0, 0)),
        ] + copy_specs,
        out_shape=[
            jax.ShapeDtypeStruct((rows, d), F32),
            jax.ShapeDtypeStruct((n_ff, rows // seq_len, 2, tf), F32),
        ] + copy_shapes,
        scratch_shapes=[pltpu.VMEM((tm, d), BF16), pltpu.VMEM((n_ff, 2, tf), F32)],
        compiler_params=_params(),
        name="conv_ffn",
    )(x, g_pre.reshape(1, d), *up_args, w_conv, wd_arg, g_post.reshape(1, d), hist)
    state = jnp.moveaxis(out[1], 0, 2).reshape(rows // seq_len, 2, d_ff)
    copies = list(out[2:])
    w_up_q = (copies.pop(0), copies.pop(0)) if emit_up else w_up
    w_down_q = copies.pop(0) if emit_down else w_down
    return out[0], state, (w_up_q, w_down_q)


def _ple_kernel(x_ref, p_ref, gpre_ref, wgate_ref, wproj_ref, gpost_ref, o_ref, *rest, n_k, emit):
    wgate_copy_ref, wproj_copy_ref = rest[:2] if emit else (None, None)
    h_ref = rest[-1] if n_k > 1 else None
    k = pl.program_id(1)
    tk = wgate_ref.shape[0]

    def finish(pre):
        emb = jnp.dot(p_ref[...].astype(BF16), _load_weight(wproj_ref, wproj_copy_ref),
                      preferred_element_type=F32)
        o_ref[...] = x_ref[...] + _rms(jax.nn.sigmoid(pre) * emb, gpost_ref[...])

    if n_k == 1:
        h = _rms(x_ref[...], gpre_ref[...]).astype(BF16)
        finish(jnp.dot(h, _load_weight(wgate_ref, wgate_copy_ref), preferred_element_type=F32))
    else:
        @pl.when(k == 0)
        def _():
            h = _rms(x_ref[...], gpre_ref[...]).astype(BF16)
            for kk in range(n_k):
                h_ref[kk] = h[:, kk * tk:(kk + 1) * tk]

        _accumulate(o_ref, jnp.dot(h_ref[k], _load_weight(wgate_ref, wgate_copy_ref),
                                   preferred_element_type=F32), k)

        @pl.when(k == n_k - 1)
        def _():
            finish(o_ref[...])


def per_layer_embedding(x, p, layer, g_pre, w_gate, w_proj, g_post, *, tm, tk=None):
    rows, d = x.shape
    pd = p.shape[-1]
    tm = _row_tile(rows, tm)
    tk = d if tk is None else tk
    assert d % tk == 0
    n_k = d // tk
    wg_arg, wg_spec, wg_copy_spec, wg_copy_shape = _weight_in(w_gate, (tk, d), lambda i, k: (k, 0))
    wp_arg, wp_spec, wp_copy_spec, wp_copy_shape = _weight_in(w_proj, (pd, d), lambda i, k: (0, 0))
    emit = wg_copy_spec is not None
    assert emit == (wp_copy_spec is not None)
    assert not emit or rows == tm, "a bf16 copy needs every weight tile visited exactly once"
    if not emit:
        wp_spec = _resident_spec((pd, d), lambda i, k: (0, 0))
        if n_k == 1:
            wg_spec = _resident_spec((tk, d), lambda i, k: (0, 0))
    out_specs = [pl.BlockSpec((tm, d), lambda i, k: (i, 0))]
    out_shape = [jax.ShapeDtypeStruct((rows, d), F32)]
    if emit:
        out_specs += [wg_copy_spec, wp_copy_spec]
        out_shape += [wg_copy_shape, wp_copy_shape]
    out = pl.pallas_call(
        functools.partial(_ple_kernel, n_k=n_k, emit=emit),
        grid=(rows // tm, n_k),
        in_specs=[
            pl.BlockSpec((tm, d), lambda i, k: (i, 0)),
            pl.BlockSpec((None, tm, pd), lambda i, k: (layer, i, 0)),
            pl.BlockSpec((1, d), lambda i, k: (0, 0)),
            wg_spec,
            wp_spec,
            pl.BlockSpec((1, d), lambda i, k: (0, 0)),
        ],
        out_specs=out_specs,
        out_shape=out_shape,
        scratch_shapes=[pltpu.VMEM((n_k, tm, tk), BF16)] if n_k > 1 else [],
        compiler_params=_params(),
        name="per_layer_embedding",
    )(x, p, g_pre.reshape(1, d), wg_arg, wp_arg, g_post.reshape(1, d))
    return (out[0], (out[1], out[2])) if emit else (out[0], (w_gate, w_proj))


def _rope_cos_sin(pos, dim):
    inv = 1.0 / (ROPE_THETA ** (jnp.arange(0, dim, 2, dtype=F32) / dim))
    ang = pos.astype(F32)[:, None] * inv[None, :]
    return jnp.cos(ang), jnp.sin(ang)


def _rope_layout(cos, sin, reps):
    return (jnp.tile(jnp.concatenate([cos, cos], axis=-1), (1, reps)),
            jnp.tile(jnp.concatenate([-sin, sin], axis=-1), (1, reps)))


def _mix_stage(i, x, grp, states, wts, prm, cfg, sides=None):
    b, t = grp['b'], grp['t']
    rows = b * t
    j = i // 2
    wq = {}
    sides = dict(sides or {})
    cast = {}
    if i % 2 == 0:
        n_heads, dk, dv = states['ret'].shape[2:]
        names = list(sides)
        proj, wq['w_in'], done = even_in_proj(
            x, prm['norm_mix_pre'][i], wts['w_in'], *grp['even_tabs'], tm=cfg['tm_in'], tn=cfg['tn_even'],
            d_a=prm['w_conv_a'].shape[-1], n_heads=n_heads, dk=dk, dv=dv, sides=[sides.pop(n) for n in names])
        cast.update(zip(names, done))
        mix, c_s, r_s = even_mixer(proj.reshape(b, t, -1), states['conv'][j], states['ret'][j],
                                   prm['w_conv_a'][j], prm['ret_gn'][j], blk=cfg['blk_even'])
        new = dict(conv=c_s, ret=r_s)
    else:
        proj, wq['w_in'] = norm_matmul(x, prm['norm_mix_pre'][i], wts['w_in'], tm=cfg['tm_in'], tn=cfg['tn_odd'])
        lanes = grp['kv_lanes']
        has_cache = states['cache_k'] is not None
        if has_cache:
            ck = states['cache_k'][j].reshape(b, WINDOW, lanes)
            cv = states['cache_v'][j].reshape(b, WINDOW, lanes)
            n_state_rows = t
        else:
            ck = cv = jnp.zeros((b, WINDOW, lanes), F32)
            n_state_rows = cfg['kv_rows']
        mix, p_s, k_s, v_s = odd_mixer(proj.reshape(b, t, -1), *grp['odd_tabs'], states['pool'][j], ck, cv,
                                       prm['w_pool'][j], prm['pool_scale'][j], prm['sinks'][j],
                                       blk=cfg['blk_odd'], pos0=grp['pos0'], has_cache=has_cache,
                                       n_state_rows=n_state_rows)
        kv_shape = (b, n_state_rows, 2, lanes // 2)
        new = dict(pool=p_s, k=k_s.reshape(kv_shape), v=v_s.reshape(kv_shape))
    names = list(sides)
    x, wq['w_out'], done = out_proj(mix.reshape(rows, -1), wts['w_out'], x, prm['norm_mix_post'][i],
                                    tm=cfg['tm_out'], tk=cfg['tk_out'], sides=[sides[n] for n in names])
    cast.update(zip(names, done))
    return x, new, wq, cast


def _ffn_stage(i, x, p, grp, states, wts, prm, cfg):
    wq = {}
    x, ffn_state, (wq['w_up'], wq['w_down']) = conv_ffn(
        x, prm['norm_ffn_pre'][i], wts['w_up'], prm['w_conv_ffn'][i], wts['w_down'], prm['norm_ffn_post'][i],
        states['ffn'][i], seq_len=grp['t'], tm=cfg['tm_ffn'], tf=cfg['tf'])
    x, (wq['w_gate'], wq['w_proj']) = per_layer_embedding(
        x, p, i, prm['norm_ple_pre'][i], wts['w_gate'], wts['w_proj'], prm['norm_ple_post'][i],
        tm=cfg['tm_ple'], tk=cfg['tk_ple'])
    return x, ffn_state, wq


def kernel(x_prompt, x_sample, state_conv, state_ret, state_pool, cache_k, cache_v, state_ffn, p_prompt, p_sample, norm_mix_pre, norm_mix_post, norm_ffn_pre, norm_ffn_post, norm_ple_pre, norm_ple_post, w_in_even, w_conv_a, ret_gn, w_out_even, w_in_odd, w_pool, pool_scale, sinks, w_out_odd, w_up, w_conv_ffn, w_down, w_ple_gate, w_ple_proj):
    depth = norm_mix_pre.shape[0]
    n_even, n_odd = w_in_even.shape[0], w_in_odd.shape[0]
    d = x_prompt.shape[-1]
    d_a = w_conv_a.shape[-1]
    n_heads, dk, dv = state_ret.shape[2:]
    d_c = pool_scale.shape[-1]
    kv_rows, n_kv, hd = cache_k.shape[2:]
    d_ff = w_conv_ffn.shape[-1]
    lanes = n_kv * hd
    dt = x_prompt.dtype

    prm = dict(norm_mix_pre=norm_mix_pre, norm_mix_post=norm_mix_post, norm_ffn_pre=norm_ffn_pre,
               norm_ffn_post=norm_ffn_post, norm_ple_pre=norm_ple_pre, norm_ple_post=norm_ple_post,
               w_conv_a=w_conv_a, ret_gn=ret_gn, w_pool=w_pool.astype(BF16), pool_scale=pool_scale, sinks=sinks,
               w_conv_ffn=w_conv_ffn)

    def group(x, pos0):
        b, t = x.shape[:2]
        pos = pos0 + jnp.arange(t, dtype=jnp.int32)
        cos_e, sin_e = _rope_cos_sin(pos, dk)
        cos_o, sin_o = (cos_e[:, ::2], sin_e[:, ::2]) if 2 * hd == dk else _rope_cos_sin(pos, hd)
        return dict(b=b, t=t, pos0=pos0, kv_lanes=lanes,
                    even_tabs=tuple(jnp.tile(tab, (b, 1)) for tab in _rope_layout(cos_e, sin_e, 1)),
                    odd_tabs=_rope_layout(cos_o, sin_o, lanes // hd))

    bp, sp = x_prompt.shape[:2]
    bs, ts = x_sample.shape[:2]
    grp_p, grp_s = group(x_prompt, 0), group(x_sample, PAST_LEN)
    st_p = dict(conv=jnp.zeros((n_even, bp, 2, d_a), dt), ret=jnp.zeros((n_even, bp, n_heads, dk, dv), dt),
                pool=jnp.zeros((n_odd, bp, POOL_HIST, d_c), dt), cache_k=None, cache_v=None,
                ffn=jnp.zeros((depth, bp, 2, d_ff), dt))
    st_s = dict(conv=state_conv, ret=state_ret, pool=state_pool, cache_k=cache_k, cache_v=cache_v, ffn=state_ffn)
    cfg_s = dict(tm_in=bs * ts, tn_even=1024, tn_odd=w_in_odd.shape[-1] // 2, blk_even=ts, blk_odd=ts,
                 tm_out=bs * ts, tk_out=512, tm_ffn=bs * ts, tf=512, tm_ple=bs * ts, tk_ple=512, kv_rows=kv_rows)
    cfg_p = dict(tm_in=1024, tn_even=1024, tn_odd=w_in_odd.shape[-1] // 2, blk_even=256, blk_odd=128,
                 tm_out=512, tk_out=None, tm_ffn=1024, tf=512, tm_ple=1024, tk_ple=None, kv_rows=kv_rows)
    xp = x_prompt.reshape(bp * sp, d)
    xs = x_sample.reshape(bs * ts, d)
    pp = p_prompt.reshape(depth, bp * sp, -1)
    ps = p_sample.reshape(depth, bs * ts, -1)
    new_p, new_s = [], []
    for i in range(depth):
        j = i // 2
        raw = dict(w_in=LayerWeight(w_in_even if i % 2 == 0 else w_in_odd, j),
                   w_out=LayerWeight(w_out_even if i % 2 == 0 else w_out_odd, j),
                   w_up=LayerWeight(w_up, i), w_down=LayerWeight(w_down, i),
                   w_gate=LayerWeight(w_ple_gate, i), w_proj=LayerWeight(w_ple_proj, i))
        sides = {name: raw[name] for name in (('w_up', 'w_down') if i % 2 == 0 else ('w_down',))}
        xs, st_mix_s, wq, _ = _mix_stage(i, xs, grp_s, st_s, raw, prm, cfg_s)
        xp, st_mix_p, _, cast = _mix_stage(i, xp, grp_p, st_p, wq, prm, cfg_p, sides)
        xs, ffn_s, wq = _ffn_stage(i, xs, ps, grp_s, st_s, {**raw, **cast}, prm, cfg_s)
        xp, ffn_p, _ = _ffn_stage(i, xp, pp, grp_p, st_p, wq, prm, cfg_p)
        new_s.append(dict(st_mix_s, ffn=ffn_s))
        new_p.append(dict(st_mix_p, ffn=ffn_p))

    def stacked(new, key):
        parts = [st[key] for st in new if key in st]
        return parts[0][None] if len(parts) == 1 else jnp.stack(parts)

    outs = [xp.reshape(bp, sp, d), xs.reshape(bs, ts, d)]
    for key in ('conv', 'ret', 'pool', 'k', 'v', 'ffn'):
        outs += [stacked(new_p, key), stacked(new_s, key)]
    return tuple(outs)
```

```python
import functools
import math

import jax
import jax.numpy as jnp
from jax import lax
from jax.experimental import pallas as pl
from jax.experimental.pallas import tpu as pltpu

CHUNK = 64
WINDOW = 128
PAST_LEN = 4096
EPS = 1e-6
ROPE_THETA = 10000.0
NEG_INF = -1e30
POOL_WINDOWS = (2, 4, 8, 16)
POOL_HIST = max(POOL_WINDOWS) - 1

V7X_VMEM_BYTES = 64 * 1024 * 1024
VMEM_LIMIT_BYTES = V7X_VMEM_BYTES - 6 * 1024 * 1024
MXU_COLS = 256
SUBLANES = 8

F32 = jnp.float32
BF16 = jnp.bfloat16


def _params():
    return pltpu.CompilerParams(vmem_limit_bytes=VMEM_LIMIT_BYTES)


def _rms(x, g):
    return x * lax.rsqrt(jnp.mean(x * x, axis=-1, keepdims=True) + EPS) * g


def _gelu_tanh(x):
    c1 = math.sqrt(2.0 / math.pi)
    half = 0.5 * x
    return half + half * jnp.tanh(x * (c1 + (c1 * 0.044715) * (x * x)))


ROW_CHUNK = 256


def _for_row_chunks(n_rows, fn):
    if n_rows <= ROW_CHUNK or n_rows % ROW_CHUNK:
        fn(slice(0, n_rows))
        return

    def body(r, carry):
        fn(pl.ds(pl.multiple_of(r * ROW_CHUNK, ROW_CHUNK), ROW_CHUNK))
        return carry

    lax.fori_loop(0, n_rows // ROW_CHUNK, body, 0)


def _resident_spec(block, index_map):
    return pl.BlockSpec(block, index_map, pipeline_mode=pl.Buffered(1))


def _row_tile(rows, want):
    t = min(rows, want)
    assert rows % t == 0, (rows, t)
    return t


class LayerWeight:
    def __init__(self, stack, layer):
        self.stack, self.layer = stack, layer
        self.shape = stack.shape[1:]

    def spec(self, block, index_map):
        layer = self.layer
        return pl.BlockSpec((None,) + block, lambda *g: (layer,) + index_map(*g))


def _weight_in(w, block, index_map):
    if isinstance(w, LayerWeight):
        return (w.stack, w.spec(block, index_map), pl.BlockSpec(block, index_map),
                jax.ShapeDtypeStruct(w.shape, BF16))
    return w, pl.BlockSpec(block, index_map), None, None


def _load_weight(w_ref, copy_ref):
    w = w_ref[...]
    if copy_ref is not None:
        w = w.astype(BF16)
        copy_ref[...] = w
    return w


BF16_SUBLANES = 16


def _side_casts(sides, n_steps, step_of):
    args, in_specs, out_specs, out_shapes = [], [], [], []
    for side in sides:
        rows, cols = side.shape
        n_slabs = max(n for n in range(1, n_steps + 1) if rows % n == 0 and (rows // n) % BF16_SUBLANES == 0)
        slab = rows // n_slabs
        index_map = lambda *g, last=n_slabs - 1: (jnp.minimum(step_of(*g), last), 0)
        args.append(side.stack)
        in_specs.append(side.spec((slab, cols), index_map))
        out_specs.append(pl.BlockSpec((slab, cols), index_map))
        out_shapes.append(jax.ShapeDtypeStruct((rows, cols), BF16))
    return args, in_specs, out_specs, out_shapes


def _run_side_casts(side_refs):
    n = len(side_refs) // 2
    for src, dst in zip(side_refs[:n], side_refs[n:]):
        dst[...] = src[...].astype(BF16)


def _norm_matmul_kernel(x_ref, g_ref, w_ref, *rest, emit):
    o_ref, copy_ref, h_ref = rest if emit else (rest[0], None, rest[1])

    @pl.when(pl.program_id(1) == 0)
    def _():
        def start(rows):
            h_ref[rows, :] = _rms(x_ref[rows, :], g_ref[...]).astype(BF16)

        _for_row_chunks(x_ref.shape[0], start)

    w = _load_weight(w_ref, copy_ref)
    o_ref[...] = jnp.dot(h_ref[...], w, preferred_element_type=F32)


def norm_matmul(x, g, w, *, tm, tn):
    rows, d = x.shape
    n = w.shape[1]
    tm = _row_tile(rows, tm)
    assert n % tn == 0
    w_arg, w_spec, copy_spec, copy_shape = _weight_in(w, (d, tn), lambda i, j: (0, j))
    emit = copy_spec is not None
    assert not emit or rows == tm, "a bf16 copy needs every weight tile visited exactly once"
    out_specs = [pl.BlockSpec((tm, tn), lambda i, j: (i, j))]
    out_shape = [jax.ShapeDtypeStruct((rows, n), F32)]
    if emit:
        out_specs.append(copy_spec)
        out_shape.append(copy_shape)
    out = pl.pallas_call(
        functools.partial(_norm_matmul_kernel, emit=emit),
        grid=(rows // tm, n // tn),
        in_specs=[
            pl.BlockSpec((tm, d), lambda i, j: (i, 0)),
            pl.BlockSpec((1, d), lambda i, j: (0, 0)),
            w_spec,
        ],
        out_specs=out_specs,
        out_shape=out_shape,
        scratch_shapes=[pltpu.VMEM((tm, d), BF16)],
        compiler_params=_params(),
        name="norm_matmul",
    )(x, g.reshape(1, d), w_arg)
    return (out[0], out[1]) if emit else (out[0], w)


def _even_in_proj_kernel(x_ref, g_ref, w_ref, cos_ref, sin_ref, *rest, emit, n_side, kinds, dk, k_scale):
    side_in, rest = rest[:n_side], rest[n_side:]
    o_ref, rest = rest[0], rest[1:]
    copy_ref, rest = (rest[0], rest[1:]) if emit else (None, rest)
    side_out, (h_ref,) = rest[:n_side], rest[n_side:]
    _run_side_casts(side_in + side_out)
    j = pl.program_id(1)
    tn = w_ref.shape[1]

    @pl.when(j == 0)
    def _():
        def start(rows):
            h_ref[rows, :] = _rms(x_ref[rows, :], g_ref[...]).astype(BF16)

        _for_row_chunks(x_ref.shape[0], start)

    def project():
        return jnp.dot(h_ref[...], _load_weight(w_ref, copy_ref), preferred_element_type=F32)

    def rope(acc, scale):
        cosb, sinb = cos_ref[...], sin_ref[...]
        for hh in range(tn // dk):
            seg = acc[:, hh * dk:(hh + 1) * dk]
            out = seg * cosb + pltpu.roll(seg, dk // 2, axis=1) * sinb
            if scale != 1.0:
                out = out * scale
            o_ref[:, hh * dk:(hh + 1) * dk] = out.astype(BF16)

    def finish(kind):
        acc = project()
        if kind == 'rope':
            rope(acc, 1.0)
        elif kind == 'rope_scaled':
            rope(acc, k_scale)
        elif kind == 'plain':
            o_ref[...] = acc.astype(BF16)
        else:
            assert kind == 'silu'
            o_ref[...] = (acc * jax.nn.sigmoid(acc)).astype(BF16)

    for kind in dict.fromkeys(kinds):
        tiles = [t for t, k in enumerate(kinds) if k == kind]
        here = functools.reduce(jnp.logical_or, [j == t for t in tiles])
        pl.when(here)(functools.partial(finish, kind))


def even_in_proj(x, g, w, cosb, sinb, *, tm, tn, d_a, n_heads, dk, dv, sides=()):
    rows, d = x.shape
    n = w.shape[1]
    tm = _row_tile(rows, tm)
    hk, hv = n_heads * dk, n_heads * dv
    assert n == 3 * d_a + 2 * hk + 2 * hv and tn % dk == 0
    assert all(part % tn == 0 for part in (d_a, hk, hv))
    kinds = (('plain',) * (3 * d_a // tn) + ('rope',) * (hk // tn) + ('rope_scaled',) * (hk // tn)
             + ('plain',) * (hv // tn) + ('silu',) * (hv // tn))
    w_arg, w_spec, copy_spec, copy_shape = _weight_in(w, (d, tn), lambda i, j: (0, j))
    emit = copy_spec is not None
    assert not emit or rows == tm, "a bf16 copy needs every weight tile visited exactly once"
    out_specs = [pl.BlockSpec((tm, tn), lambda i, j: (i, j))]
    out_shape = [jax.ShapeDtypeStruct((rows, n), BF16)]
    if emit:
        out_specs.append(copy_spec)
        out_shape.append(copy_shape)
    n_col = n // tn
    side_args, side_in, side_out, side_shapes = _side_casts(
        sides, (rows // tm) * n_col, lambda i, j: i * n_col + j)
    kern = functools.partial(_even_in_proj_kernel, emit=emit, n_side=len(sides), kinds=kinds, dk=dk,
                             k_scale=dk ** -0.5)
    out = pl.pallas_call(
        kern,
        grid=(rows // tm, n_col),
        in_specs=[
            pl.BlockSpec((tm, d), lambda i, j: (i, 0)),
            pl.BlockSpec((1, d), lambda i, j: (0, 0)),
            w_spec,
            pl.BlockSpec((tm, dk), lambda i, j: (i, 0)),
            pl.BlockSpec((tm, dk), lambda i, j: (i, 0)),
        ] + side_in,
        out_specs=out_specs + side_out,
        out_shape=out_shape + side_shapes,
        scratch_shapes=[pltpu.VMEM((tm, d), BF16)],
        compiler_params=_params(),
        name="even_in_proj",
    )(x, g.reshape(1, d), w_arg, cosb, sinb, *side_args)
    n_main = 2 if emit else 1
    return out[0], (out[1] if emit else w), list(out[n_main:])


def _even_mixer_kernel(p_ref, chist_ref, rstate_ref, wconv_ref, gn_ref,
                       mix_ref, cstate_ref, rout_ref, carry_ref, s_ref, dmat_ref, qdec_ref, kdec_ref,
                       *, n_heads, dk, dv, d_a):
    q0 = 3 * d_a
    k0 = q0 + n_heads * dk
    v0 = k0 + n_heads * dk
    g0 = v0 + n_heads * dv
    c = pl.program_id(1)
    last = pl.num_programs(1) - 1
    blk = p_ref.shape[0]

    @pl.when(c == 0)
    def _():
        carry_ref[...] = chist_ref[...]
        s_ref[...] = rstate_ref[...]

    u = p_ref[:, 2 * d_a:3 * d_a].astype(F32) * p_ref[:, 0:d_a].astype(F32)
    row = lax.broadcasted_iota(jnp.int32, u.shape, 0)
    h0 = carry_ref[0:1, :]
    h1 = carry_ref[1:2, :]
    prev1 = jnp.where(row == 0, h1, pltpu.roll(u, 1, axis=0))
    prev2 = jnp.where(row == 0, h0, jnp.where(row == 1, h1, pltpu.roll(u, 2, axis=0)))
    w = wconv_ref[...]
    conv = prev2 * w[0:1] + prev1 * w[1:2] + u * w[2:3]
    mix_ref[:, 0:d_a] = (p_ref[:, d_a:2 * d_a].astype(F32) * conv).astype(BF16)
    tail = u[blk - 2:blk, :]
    carry_ref[...] = tail

    @pl.when(c == last)
    def _():
        cstate_ref[...] = tail

    log_gamma = [math.log1p(-(2.0 ** (-5.0 - h))) for h in range(n_heads)]

    @pl.when((pl.program_id(0) == 0) & (c == 0))
    def _():
        rel = (lax.broadcasted_iota(jnp.int32, (blk, blk), 0)
               - lax.broadcasted_iota(jnp.int32, (blk, blk), 1)).astype(F32)
        ii = lax.broadcasted_iota(jnp.int32, (blk, dk), 0).astype(F32)
        for h, lg in enumerate(log_gamma):
            dmat_ref[h] = jnp.where(rel >= 0, jnp.exp(jnp.maximum(rel, 0.0) * lg), 0.0)
            qdec_ref[h] = jnp.exp((ii + 1.0) * lg)
            kdec_ref[h] = jnp.exp((blk - 1.0 - ii) * lg)

    stage = []
    for h in range(n_heads):
        qr = p_ref[:, q0 + h * dk:q0 + (h + 1) * dk]
        kr = p_ref[:, k0 + h * dk:k0 + (h + 1) * dk]
        vb = p_ref[:, v0 + h * dv:v0 + (h + 1) * dv]
        sc = lax.dot_general(qr, kr, (((1,), (1,)), ((), ())), preferred_element_type=F32)
        st = s_ref[h]
        inter = jnp.dot((qr.astype(F32) * qdec_ref[h]).astype(BF16), st.astype(BF16),
                        preferred_element_type=F32)
        kv = lax.dot_general((kr.astype(F32) * kdec_ref[h]).astype(BF16), vb, (((0,), (0,)), ((), ())),
                             preferred_element_type=F32)
        s_ref[h] = math.exp(blk * log_gamma[h]) * st + kv
        stage.append((sc, inter, vb))
    for h, (sc, inter, vb) in enumerate(stage):
        o = jnp.dot((sc * dmat_ref[h]).astype(BF16), vb, preferred_element_type=F32) + inter
        on = o * lax.rsqrt(jnp.mean(o * o, axis=-1, keepdims=True) + EPS) * gn_ref[h:h + 1, :]
        gate = p_ref[:, g0 + h * dv:g0 + (h + 1) * dv].astype(F32)
        mix_ref[:, d_a + h * dv:d_a + (h + 1) * dv] = (on * gate).astype(BF16)

    @pl.when(c == last)
    def _():
        rout_ref[...] = s_ref[...]


def even_mixer(p, conv_hist, ret_state, w_conv, gn, *, blk):
    b, t, width = p.shape
    n_heads, dk, dv = ret_state.shape[1:]
    d_a = w_conv.shape[1]
    hv = n_heads * dv
    assert width == 3 * d_a + 2 * n_heads * dk + 2 * hv
    blk = _row_tile(t, blk)
    nc = t // blk
    kern = functools.partial(_even_mixer_kernel, n_heads=n_heads, dk=dk, dv=dv, d_a=d_a)
    return pl.pallas_call(
        kern,
        grid=(b, nc),
        in_specs=[
            pl.BlockSpec((None, blk, width), lambda bi, ci: (bi, ci, 0)),
            pl.BlockSpec((None, 2, d_a), lambda bi, ci: (bi, 0, 0)),
            pl.BlockSpec((None, n_heads, dk, dv), lambda bi, ci: (bi, 0, 0, 0)),
            pl.BlockSpec((3, d_a), lambda bi, ci: (0, 0)),
            pl.BlockSpec((n_heads, dv), lambda bi, ci: (0, 0)),
        ],
        out_specs=[
            pl.BlockSpec((None, blk, d_a + hv), lambda bi, ci: (bi, ci, 0)),
            pl.BlockSpec((None, 2, d_a), lambda bi, ci: (bi, 0, 0)),
            pl.BlockSpec((None, n_heads, dk, dv), lambda bi, ci: (bi, 0, 0, 0)),
        ],
        out_shape=[
            jax.ShapeDtypeStruct((b, t, d_a + hv), BF16),
            jax.ShapeDtypeStruct((b, 2, d_a), F32),
            jax.ShapeDtypeStruct((b, n_heads, dk, dv), F32),
        ],
        scratch_shapes=[pltpu.VMEM((2, d_a), F32), pltpu.VMEM((n_heads, dk, dv), F32),
                        pltpu.VMEM((n_heads, blk, blk), F32), pltpu.VMEM((n_heads, blk, dk), F32),
                        pltpu.VMEM((n_heads, blk, dk), F32)],
        compiler_params=_params(),
        name="even_mixer",
    )(p, conv_hist, ret_state, w_conv, gn)


def _odd_mixer_kernel(sinks_ref, p_ref, cos_ref, sin_ref,
                      phist_ref, ck_ref, cv_ref, wpool_ref, pscale_ref,
                      mix_ref, pstate_ref, kstate_ref, vstate_ref,
                      ext_ref, kbuf_ref, vbuf_ref,
                      *, d_c, n_q_heads, n_kv_heads, hd, pos0, has_cache, n_state_rows):
    c = pl.program_id(1)
    last = pl.num_programs(1) - 1
    blk = p_ref.shape[0]
    hist_rows = POOL_HIST + 1
    d_cg = d_c // len(POOL_WINDOWS)
    lanes = 2 * hd
    q0 = d_c
    k0 = q0 + n_q_heads * hd
    v0 = k0 + lanes

    @pl.when(c == 0)
    def _():
        ext_ref[0:1, :] = jnp.zeros((1, d_c), F32)
        ext_ref[1:hist_rows, :] = phist_ref[...]
        kbuf_ref[0:WINDOW, :] = ck_ref[...]
        vbuf_ref[0:WINDOW, :] = cv_ref[...]

    u = p_ref[:, 0:d_c]
    ext_ref[hist_rows:hist_rows + blk, :] = u
    pos = pos0 + c * blk + lax.broadcasted_iota(jnp.int32, (blk, 1), 0)
    for gi, win in enumerate(POOL_WINDOWS):
        cols = slice(gi * d_cg, (gi + 1) * d_cg)
        s = ext_ref[:, cols]
        span = 1
        while span < win:
            s = s + pltpu.roll(s, span, axis=0)
            span *= 2
        cnt = jnp.minimum(pos + 1, win).astype(F32)
        d = s[hist_rows:, :] / cnt - u[:, cols]
        y = jnp.dot(d.astype(BF16), wpool_ref[gi], preferred_element_type=F32)
        mix_ref[:, cols] = (y * pscale_ref[:, cols]).astype(BF16)
    tail = ext_ref[blk:blk + hist_rows, :]
    ext_ref[0:hist_rows, :] = tail

    cos4 = cos_ref[...]
    sin4 = sin_ref[...]
    lane = lax.broadcasted_iota(jnp.int32, (1, lanes), 1)
    first_half = jnp.bitwise_and(lane, hd - 1) < (hd // 2)
    left = lane < hd

    def rope(x):
        rot = jnp.where(first_half, pltpu.roll(x, lanes - hd // 2, axis=1), pltpu.roll(x, hd // 2, axis=1))
        return x * cos4 + rot * sin4

    kr = rope(p_ref[:, k0:k0 + lanes])
    vv = p_ref[:, v0:v0 + lanes]
    kbuf_ref[WINDOW:WINDOW + blk, :] = kr
    vbuf_ref[WINDOW:WINDOW + blk, :] = vv
    kall = kbuf_ref[...]
    vall = vbuf_ref[...]
    kswap = pltpu.roll(kall, hd, axis=1)
    vswap = pltpu.roll(vall, hd, axis=1)
    nk = WINDOW + blk
    chunk_shift = CHUNK.bit_length() - 1
    qi = jnp.right_shift(lax.broadcasted_iota(jnp.int32, (blk, 1), 0), chunk_shift)
    kj = lax.broadcasted_iota(jnp.int32, (1, nk), 1)
    kc = jnp.right_shift(kj, chunk_shift) - WINDOW // CHUNK
    ok = (kc <= qi) & (kc >= qi - WINDOW // CHUNK)
    if not has_cache:
        ok = ok & ((kj >= WINDOW) | (c > 0))
    scale = hd ** -0.5
    group = n_q_heads // n_kv_heads
    zeros = jnp.zeros_like(vall)
    kv_ops = [
        (jnp.where(left, kall, kswap).astype(BF16), jnp.where(left, vall, zeros).astype(BF16),
         jnp.where(left, zeros, vswap).astype(BF16)),
        (jnp.where(left, kswap, kall).astype(BF16), jnp.where(left, vswap, zeros).astype(BF16),
         jnp.where(left, zeros, vall).astype(BF16)),
    ]
    scores = []
    for head in range(n_q_heads):
        kvh = head // group
        if head % 2 == 0:
            qr = rope(p_ref[:, q0 + (head // 2) * lanes:q0 + (head // 2 + 1) * lanes])
        qh = jnp.where(left if head % 2 == 0 else jnp.logical_not(left), qr, 0.0).astype(BF16)
        scores.append(lax.dot_general(qh, kv_ops[kvh][0], (((1,), (1,)), ((), ())),
                                      preferred_element_type=F32))
    probs = []
    for head, sc in enumerate(scores):
        sc = jnp.where(ok, sc * scale, NEG_INF)
        sink = sinks_ref[head]
        m = jnp.maximum(jnp.max(sc, axis=-1, keepdims=True), sink)
        e = jnp.exp(sc - m)
        den = jnp.sum(e, axis=-1, keepdims=True) + jnp.exp(sink - m)
        probs.append((e / den).astype(BF16))
    for pair in range(n_q_heads // 2):
        kvh = (2 * pair) // group
        assert (2 * pair + 1) // group == kvh
        out = (jnp.dot(probs[2 * pair], kv_ops[kvh][1], preferred_element_type=F32)
               + jnp.dot(probs[2 * pair + 1], kv_ops[kvh][2], preferred_element_type=F32))
        mix_ref[:, d_c + pair * lanes:d_c + (pair + 1) * lanes] = out.astype(BF16)
    ktail = kbuf_ref[blk:blk + WINDOW, :]
    vtail = vbuf_ref[blk:blk + WINDOW, :]
    kbuf_ref[0:WINDOW, :] = ktail
    vbuf_ref[0:WINDOW, :] = vtail

    @pl.when(c == last)
    def _():
        pstate_ref[...] = ext_ref[1:hist_rows, :]
        kstate_ref[...] = kbuf_ref[nk - n_state_rows:nk, :]
        vstate_ref[...] = vbuf_ref[nk - n_state_rows:nk, :]


def odd_mixer(p, cos4, sin4, pool_hist, cache_k, cache_v, w_pool, pool_scale, sinks, *,
              blk, pos0, has_cache, n_state_rows):
    b, t, width = p.shape
    d_c = pool_scale.shape[-1]
    n_q_heads = sinks.shape[-1]
    lanes = cache_k.shape[-1]
    n_kv_heads = 2
    hd = lanes // n_kv_heads
    d_q = n_q_heads * hd
    assert d_q == d_c and d_c % lanes == 0 and width == d_c + d_q + 2 * lanes
    blk = _row_tile(t, blk)
    nc = t // blk
    assert blk % CHUNK == 0 or nc == 1
    kern = functools.partial(_odd_mixer_kernel, d_c=d_c, n_q_heads=n_q_heads, n_kv_heads=n_kv_heads,
                             hd=hd, pos0=pos0, has_cache=has_cache, n_state_rows=n_state_rows)
    return pl.pallas_call(
        kern,
        grid=(b, nc),
        in_specs=[
            pl.BlockSpec(memory_space=pltpu.SMEM),
            pl.BlockSpec((None, blk, width), lambda bi, ci: (bi, ci, 0)),
            pl.BlockSpec((blk, lanes), lambda bi, ci: (ci, 0)),
            pl.BlockSpec((blk, lanes), lambda bi, ci: (ci, 0)),
            pl.BlockSpec((None, POOL_HIST, d_c), lambda bi, ci: (bi, 0, 0)),
            pl.BlockSpec((None, WINDOW, lanes), lambda bi, ci: (bi, 0, 0)),
            pl.BlockSpec((None, WINDOW, lanes), lambda bi, ci: (bi, 0, 0)),
            pl.BlockSpec(w_pool.shape, lambda bi, ci: (0, 0, 0)),
            pl.BlockSpec((1, d_c), lambda bi, ci: (0, 0)),
        ],
        out_specs=[
            pl.BlockSpec((None, blk, d_c + d_q), lambda bi, ci: (bi, ci, 0)),
            pl.BlockSpec((None, POOL_HIST, d_c), lambda bi, ci: (bi, 0, 0)),
            pl.BlockSpec((None, n_state_rows, lanes), lambda bi, ci: (bi, 0, 0)),
            pl.BlockSpec((None, n_state_rows, lanes), lambda bi, ci: (bi, 0, 0)),
        ],
        out_shape=[
            jax.ShapeDtypeStruct((b, t, d_c + d_q), BF16),
            jax.ShapeDtypeStruct((b, POOL_HIST, d_c), F32),
            jax.ShapeDtypeStruct((b, n_state_rows, lanes), F32),
            jax.ShapeDtypeStruct((b, n_state_rows, lanes), F32),
        ],
        scratch_shapes=[
            pltpu.VMEM((POOL_HIST + 1 + blk, d_c), F32),
            pltpu.VMEM((WINDOW + blk, lanes), F32),
            pltpu.VMEM((WINDOW + blk, lanes), F32),
        ],
        compiler_params=_params(),
        name="odd_mixer",
    )(sinks, p, cos4, sin4, pool_hist, cache_k, cache_v, w_pool, pool_scale.reshape(1, d_c))


def _accumulate(o_ref, part, k):
    @pl.when(k == 0)
    def _():
        o_ref[...] = part

    @pl.when(k > 0)
    def _():
        o_ref[...] += part


def _out_proj_kernel(mix_ref, w_ref, x_ref, g_ref, *rest, n_k, emit, n_side):
    side_in, rest = rest[:n_side], rest[n_side:]
    o_ref, rest = rest[0], rest[1:]
    copy_ref, side_out = (rest[0], rest[1:]) if emit else (None, rest)
    _run_side_casts(side_in + side_out)
    k = pl.program_id(1)
    part = jnp.dot(mix_ref[...], _load_weight(w_ref, copy_ref), preferred_element_type=F32)
    if n_k == 1:
        o_ref[...] = x_ref[...] + _rms(part, g_ref[...])
    else:
        _accumulate(o_ref, part, k)

        @pl.when(k == n_k - 1)
        def _():
            o_ref[...] = x_ref[...] + _rms(o_ref[...], g_ref[...])


def out_proj(mix, w, x, g, *, tm, tk=None, sides=()):
    rows, kdim = mix.shape
    d = x.shape[1]
    tm = _row_tile(rows, tm)
    tk = kdim if tk is None else tk
    assert kdim % tk == 0
    w_arg, w_spec, copy_spec, copy_shape = _weight_in(w, (tk, d), lambda i, k: (k, 0))
    emit = copy_spec is not None
    assert not emit or rows == tm, "a bf16 copy needs every weight tile visited exactly once"
    if not emit and tk == kdim:
        w_spec = _resident_spec((tk, d), lambda i, k: (0, 0))
    out_specs = [pl.BlockSpec((tm, d), lambda i, k: (i, 0))]
    out_shape = [jax.ShapeDtypeStruct((rows, d), F32)]
    if emit:
        out_specs.append(copy_spec)
        out_shape.append(copy_shape)
    n_k = kdim // tk
    side_args, side_in, side_out, side_shapes = _side_casts(
        sides, (rows // tm) * n_k, lambda i, k: i * n_k + k)
    out = pl.pallas_call(
        functools.partial(_out_proj_kernel, n_k=n_k, emit=emit, n_side=len(sides)),
        grid=(rows // tm, n_k),
        in_specs=[
            pl.BlockSpec((tm, tk), lambda i, k: (i, k)),
            w_spec,
            pl.BlockSpec((tm, d), lambda i, k: (i, 0)),
            pl.BlockSpec((1, d), lambda i, k: (0, 0)),
        ] + side_in,
        out_specs=out_specs + side_out,
        out_shape=out_shape + side_shapes,
        compiler_params=_params(),
        name="out_proj",
    )(mix, w_arg, x, g.reshape(1, d), *side_args)
    n_main = 2 if emit else 1
    return out[0], (out[1] if emit else w), list(out[n_main:])


def _ffn_kernel(x_ref, gpre_ref, wa_ref, wg_ref, wconv_ref, wd_ref, gpost_ref, hist_ref,
                o_ref, state_ref, *rest, n_seq, blocks_per_seq, sub, emit_up, emit_down):
    (wa_copy_ref, wg_copy_ref), rest = (rest[:2], rest[2:]) if emit_up else ((None, None), rest)
    wd_copy_ref, rest = (rest[0], rest[1:]) if emit_down else (None, rest)
    h_ref, carry_ref = rest
    i = pl.program_id(0)
    j = pl.program_id(1)
    last_j = pl.num_programs(1) - 1
    tm = x_ref.shape[0]
    tf = wa_ref.shape[1]
    rows_per_seq = tm // n_seq

    @pl.when(j == 0)
    def _():
        def start(rows):
            x = x_ref[rows, :]
            h_ref[rows, :] = _rms(x, gpre_ref[...]).astype(BF16)
            o_ref[rows, :] = jnp.zeros_like(x)

        _for_row_chunks(tm, start)

    if blocks_per_seq > 1:
        @pl.when(i % blocks_per_seq == 0)
        def _():
            carry_ref[j] = hist_ref[0]

    h = h_ref[...]
    assert rows_per_seq & (rows_per_seq - 1) == 0
    t = jnp.bitwise_and(lax.broadcasted_iota(jnp.int32, (tm, 1), 0), rows_per_seq - 1)
    groups = [slice(s * sub, (s + 1) * sub) for s in range(tf // sub)]
    wa, wg = ((_load_weight(wa_ref, wa_copy_ref), _load_weight(wg_ref, wg_copy_ref)) if emit_up
              else (wa_ref, wg_ref))
    wd = _load_weight(wd_ref, wd_copy_ref) if emit_down else wd_ref
    pending = None
    for cols in groups:
        a = jnp.dot(h, wa[:, cols], preferred_element_type=F32)
        val = jnp.dot(h, wg[:, cols], preferred_element_type=F32)
        if pending is not None:
            o_ref[...] += jnp.dot(pending[0], wd[pending[1], :], preferred_element_type=F32)
        if blocks_per_seq > 1:
            hist0 = carry_ref[j, 0:1, cols]
            hist1 = carry_ref[j, 1:2, cols]
        elif n_seq == 1:
            hist0 = hist_ref[0, 0:1, cols]
            hist1 = hist_ref[0, 1:2, cols]
        else:
            hist = hist_ref[:, :, cols]
            hist0 = jnp.broadcast_to(hist[:, 0:1, :], (n_seq, rows_per_seq, sub)).reshape(tm, sub)
            hist1 = jnp.broadcast_to(hist[:, 1:2, :], (n_seq, rows_per_seq, sub)).reshape(tm, sub)
        prev1, prev2 = pltpu.roll(a, 1, axis=0), pltpu.roll(a, 2, axis=0)
        if n_seq == 1:
            t8 = t[0:SUBLANES]
            top1 = jnp.where(t8 == 0, hist1, prev1[0:SUBLANES])
            top2 = jnp.where(t8 == 0, hist0, jnp.where(t8 == 1, hist1, prev2[0:SUBLANES]))
            prev1 = jnp.concatenate([top1, prev1[SUBLANES:]], axis=0)
            prev2 = jnp.concatenate([top2, prev2[SUBLANES:]], axis=0)
        else:
            prev1 = jnp.where(t == 0, hist1, prev1)
            prev2 = jnp.where(t == 0, hist0, jnp.where(t == 1, hist1, prev2))
        w = wconv_ref[:, cols]
        conv = prev2 * w[0:1] + prev1 * w[1:2] + a * w[2:3]
        pending = ((_gelu_tanh(conv) * val).astype(BF16), cols)
        tail = a.reshape(n_seq, rows_per_seq, sub)[:, rows_per_seq - 2:, :]
        state_ref[j, :, :, cols] = tail
        if blocks_per_seq > 1:
            carry_ref[j, :, cols] = tail[0]
    o_ref[...] += jnp.dot(pending[0], wd[pending[1], :], preferred_element_type=F32)

    @pl.when(j == last_j)
    def _():
        def finish(rows):
            o_ref[rows, :] = x_ref[rows, :] + _rms(o_ref[rows, :], gpost_ref[...])

        _for_row_chunks(tm, finish)


def conv_ffn(x, g_pre, w_up, w_conv, w_down, g_post, hist, *, seq_len, tm, tf):
    rows, d = x.shape
    d_ff = w_down.shape[0]
    tm = _row_tile(rows, tm)
    assert d_ff % tf == 0 and tf % MXU_COLS == 0
    n_ff = d_ff // tf
    emit_up = isinstance(w_up, LayerWeight)
    emit_down = isinstance(w_down, LayerWeight)
    assert not (emit_up or emit_down) or rows == tm, "a bf16 copy needs every weight tile visited exactly once"
    gate_tile, value_tile = (lambda i, j: (0, j)), (lambda i, j: (0, j + n_ff))
    copy_specs, copy_shapes = [], []
    if emit_up:
        up_args = (w_up.stack, w_up.stack)
        up_specs = [w_up.spec((d, tf), gate_tile), w_up.spec((d, tf), value_tile)]
        copy_specs += [pl.BlockSpec((d, tf), gate_tile)] * 2
        copy_shapes += [jax.ShapeDtypeStruct((d, d_ff), BF16)] * 2
    elif isinstance(w_up, tuple):
        up_args = w_up
        up_specs = [pl.BlockSpec((d, tf), gate_tile)] * 2
    else:
        up_args = (w_up, w_up)
        up_specs = [pl.BlockSpec((d, tf), gate_tile), pl.BlockSpec((d, tf), value_tile)]
    wd_arg, wd_spec, wd_copy_spec, wd_copy_shape = _weight_in(w_down, (tf, d), lambda i, j: (j, 0))
    if emit_down:
        copy_specs.append(wd_copy_spec)
        copy_shapes.append(wd_copy_shape)
    if seq_len >= tm:
        assert seq_len % tm == 0
        n_seq, blocks_per_seq = 1, seq_len // tm
        seq_of = lambda i: i // blocks_per_seq
    else:
        assert tm % seq_len == 0
        n_seq, blocks_per_seq = tm // seq_len, 1
        seq_of = lambda i: i
    kern = functools.partial(_ffn_kernel, n_seq=n_seq, blocks_per_seq=blocks_per_seq, sub=MXU_COLS,
                             emit_up=emit_up, emit_down=emit_down)
    out = pl.pallas_call(
        kern,
        grid=(rows // tm, n_ff),
        in_specs=[
            (_resident_spec if rows == tm else pl.BlockSpec)((tm, d), lambda i, j: (i, 0)),
            pl.BlockSpec((1, d), lambda i, j: (0, 0)),
            up_specs[0],
            up_specs[1],
            pl.BlockSpec((3, tf), lambda i, j: (0, j)),
            wd_spec,
            pl.BlockSpec((1, d), lambda i, j: (0, 0)),
            pl.BlockSpec((n_seq, 2, tf), lambda i, j: (seq_of(i), 0, j)),
        ],
        out_specs=[
            pl.BlockSpec((tm, d), lambda i, j: (i, 0)),
            pl.BlockSpec((n_ff, n_seq, 2, tf), lambda i, j: (0, seq_of(i), 0, 0)),
        ] + copy_specs,
        out_shape=[
            jax.ShapeDtypeStruct((rows, d), F32),
            jax.ShapeDtypeStruct((n_ff, rows // seq_len, 2, tf), F32),
        ] + copy_shapes,
        scratch_shapes=[pltpu.VMEM((tm, d), BF16), pltpu.VMEM((n_ff, 2, tf), F32)],
        compiler_params=_params(),
        name="conv_ffn",
    )(x, g_pre.reshape(1, d), *up_args, w_conv, wd_arg, g_post.reshape(1, d), hist)
    state = jnp.moveaxis(out[1], 0, 2).reshape(rows // seq_len, 2, d_ff)
    copies = list(out[2:])
    w_up_q = (copies.pop(0), copies.pop(0)) if emit_up else w_up
    w_down_q = copies.pop(0) if emit_down else w_down
    return out[0], state, (w_up_q, w_down_q)


def _ple_kernel(x_ref, p_ref, gpre_ref, wgate_ref, wproj_ref, gpost_ref, o_ref, *rest, n_k, emit):
    wgate_copy_ref, wproj_copy_ref = rest[:2] if emit else (None, None)
    h_ref = rest[-1] if n_k > 1 else None
    k = pl.program_id(1)
    tk = wgate_ref.shape[0]

    def finish(pre):
        emb = jnp.dot(p_ref[...].astype(BF16), _load_weight(wproj_ref, wproj_copy_ref),
                      preferred_element_type=F32)
        o_ref[...] = x_ref[...] + _rms(jax.nn.sigmoid(pre) * emb, gpost_ref[...])

    if n_k == 1:
        h = _rms(x_ref[...], gpre_ref[...]).astype(BF16)
        finish(jnp.dot(h, _load_weight(wgate_ref, wgate_copy_ref), preferred_element_type=F32))
    else:
        @pl.when(k == 0)
        def _():
            h = _rms(x_ref[...], gpre_ref[...]).astype(BF16)
            for kk in range(n_k):
                h_ref[kk] = h[:, kk * tk:(kk + 1) * tk]

        _accumulate(o_ref, jnp.dot(h_ref[k], _load_weight(wgate_ref, wgate_copy_ref),
                                   preferred_element_type=F32), k)

        @pl.when(k == n_k - 1)
        def _():
            finish(o_ref[...])


def per_layer_embedding(x, p, layer, g_pre, w_gate, w_proj, g_post, *, tm, tk=None):
    rows, d = x.shape
    pd = p.shape[-1]
    tm = _row_tile(rows, tm)
    tk = d if tk is None else tk
    assert d % tk == 0
    n_k = d // tk
    wg_arg, wg_spec, wg_copy_spec, wg_copy_shape = _weight_in(w_gate, (tk, d), lambda i, k: (k, 0))
    wp_arg, wp_spec, wp_copy_spec, wp_copy_shape = _weight_in(w_proj, (pd, d), lambda i, k: (0, 0))
    emit = wg_copy_spec is not None
    assert emit == (wp_copy_spec is not None)
    assert not emit or rows == tm, "a bf16 copy needs every weight tile visited exactly once"
    if not emit:
        wp_spec = _resident_spec((pd, d), lambda i, k: (0, 0))
        if n_k == 1:
            wg_spec = _resident_spec((tk, d), lambda i, k: (0, 0))
    out_specs = [pl.BlockSpec((tm, d), lambda i, k: (i, 0))]
    out_shape = [jax.ShapeDtypeStruct((rows, d), F32)]
    if emit:
        out_specs += [wg_copy_spec, wp_copy_spec]
        out_shape += [wg_copy_shape, wp_copy_shape]
    out = pl.pallas_call(
        functools.partial(_ple_kernel, n_k=n_k, emit=emit),
        grid=(rows // tm, n_k),
        in_specs=[
            pl.BlockSpec((tm, d), lambda i, k: (i, 0)),
            pl.BlockSpec((None, tm, pd), lambda i, k: (layer, i, 0)),
            pl.BlockSpec((1, d), lambda i, k: (0, 0)),
            wg_spec,
            wp_spec,
            pl.BlockSpec((1, d), lambda i, k: (0, 0)),
        ],
        out_specs=out_specs,
        out_shape=out_shape,
        scratch_shapes=[pltpu.VMEM((n_k, tm, tk), BF16)] if n_k > 1 else [],
        compiler_params=_params(),
        name="per_layer_embedding",
    )(x, p, g_pre.reshape(1, d), wg_arg, wp_arg, g_post.reshape(1, d))
    return (out[0], (out[1], out[2])) if emit else (out[0], (w_gate, w_proj))


def _rope_cos_sin(pos, dim):
    inv = 1.0 / (ROPE_THETA ** (jnp.arange(0, dim, 2, dtype=F32) / dim))
    ang = pos.astype(F32)[:, None] * inv[None, :]
    return jnp.cos(ang), jnp.sin(ang)


def _rope_layout(cos, sin, reps):
    return (jnp.tile(jnp.concatenate([cos, cos], axis=-1), (1, reps)),
            jnp.tile(jnp.concatenate([-sin, sin], axis=-1), (1, reps)))


def _mix_stage(i, x, grp, states, wts, prm, cfg, sides=None):
    b, t = grp['b'], grp['t']
    rows = b * t
    j = i // 2
    wq = {}
    sides = dict(sides or {})
    cast = {}
    if i % 2 == 0:
        n_heads, dk, dv = states['ret'].shape[2:]
        names = [n for n in sides if n == 'w_up']
        proj, wq['w_in'], done = even_in_proj(
            x, prm['norm_mix_pre'][i], wts['w_in'], *grp['even_tabs'], tm=cfg['tm_in'], tn=cfg['tn_even'],
            d_a=prm['w_conv_a'].shape[-1], n_heads=n_heads, dk=dk, dv=dv, sides=[sides.pop(n) for n in names])
        cast.update(zip(names, done))
        mix, c_s, r_s = even_mixer(proj.reshape(b, t, -1), states['conv'][j], states['ret'][j],
                                   prm['w_conv_a'][j], prm['ret_gn'][j], blk=cfg['blk_even'])
        new = dict(conv=c_s, ret=r_s)
    else:
        proj, wq['w_in'] = norm_matmul(x, prm['norm_mix_pre'][i], wts['w_in'], tm=cfg['tm_in'], tn=cfg['tn_odd'])
        lanes = grp['kv_lanes']
        has_cache = states['cache_k'] is not None
        if has_cache:
            ck = states['cache_k'][j].reshape(b, WINDOW, lanes)
            cv = states['cache_v'][j].reshape(b, WINDOW, lanes)
            n_state_rows = t
        else:
            ck = cv = jnp.zeros((b, WINDOW, lanes), F32)
            n_state_rows = cfg['kv_rows']
        mix, p_s, k_s, v_s = odd_mixer(proj.reshape(b, t, -1), *grp['odd_tabs'], states['pool'][j], ck, cv,
                                       prm['w_pool'][j], prm['pool_scale'][j], prm['sinks'][j],
                                       blk=cfg['blk_odd'], pos0=grp['pos0'], has_cache=has_cache,
                                       n_state_rows=n_state_rows)
        kv_shape = (b, n_state_rows, 2, lanes // 2)
        new = dict(pool=p_s, k=k_s.reshape(kv_shape), v=v_s.reshape(kv_shape))
    names = list(sides)
    x, wq['w_out'], done = out_proj(mix.reshape(rows, -1), wts['w_out'], x, prm['norm_mix_post'][i],
                                    tm=cfg['tm_out'], tk=cfg['tk_out'], sides=[sides[n] for n in names])
    cast.update(zip(names, done))
    return x, new, wq, cast


def _ffn_stage(i, x, p, grp, states, wts, prm, cfg):
    wq = {}
    x, ffn_state, (wq['w_up'], wq['w_down']) = conv_ffn(
        x, prm['norm_ffn_pre'][i], wts['w_up'], prm['w_conv_ffn'][i], wts['w_down'], prm['norm_ffn_post'][i],
        states['ffn'][i], seq_len=grp['t'], tm=cfg['tm_ffn'], tf=cfg['tf'])
    x, (wq['w_gate'], wq['w_proj']) = per_layer_embedding(
        x, p, i, prm['norm_ple_pre'][i], wts['w_gate'], wts['w_proj'], prm['norm_ple_post'][i],
        tm=cfg['tm_ple'], tk=cfg['tk_ple'])
    return x, ffn_state, wq


def kernel(x_prompt, x_sample, state_conv, state_ret, state_pool, cache_k, cache_v, state_ffn, p_prompt, p_sample, norm_mix_pre, norm_mix_post, norm_ffn_pre, norm_ffn_post, norm_ple_pre, norm_ple_post, w_in_even, w_conv_a, ret_gn, w_out_even, w_in_odd, w_pool, pool_scale, sinks, w_out_odd, w_up, w_conv_ffn, w_down, w_ple_gate, w_ple_proj):
    depth = norm_mix_pre.shape[0]
    n_even, n_odd = w_in_even.shape[0], w_in_odd.shape[0]
    d = x_prompt.shape[-1]
    d_a = w_conv_a.shape[-1]
    n_heads, dk, dv = state_ret.shape[2:]
    d_c = pool_scale.shape[-1]
    kv_rows, n_kv, hd = cache_k.shape[2:]
    d_ff = w_conv_ffn.shape[-1]
    lanes = n_kv * hd
    dt = x_prompt.dtype

    prm = dict(norm_mix_pre=norm_mix_pre, norm_mix_post=norm_mix_post, norm_ffn_pre=norm_ffn_pre,
               norm_ffn_post=norm_ffn_post, norm_ple_pre=norm_ple_pre, norm_ple_post=norm_ple_post,
               w_conv_a=w_conv_a, ret_gn=ret_gn, w_pool=w_pool.astype(BF16), pool_scale=pool_scale, sinks=sinks,
               w_conv_ffn=w_conv_ffn)

    def group(x, pos0):
        b, t = x.shape[:2]
        pos = pos0 + jnp.arange(t, dtype=jnp.int32)
        return dict(b=b, t=t, pos0=pos0, kv_lanes=lanes,
                    even_tabs=tuple(jnp.tile(tab, (b, 1)) for tab in _rope_layout(*_rope_cos_sin(pos, dk), 1)),
                    odd_tabs=_rope_layout(*_rope_cos_sin(pos, hd), lanes // hd))

    bp, sp = x_prompt.shape[:2]
    bs, ts = x_sample.shape[:2]
    grp_p, grp_s = group(x_prompt, 0), group(x_sample, PAST_LEN)
    st_p = dict(conv=jnp.zeros((n_even, bp, 2, d_a), dt), ret=jnp.zeros((n_even, bp, n_heads, dk, dv), dt),
                pool=jnp.zeros((n_odd, bp, POOL_HIST, d_c), dt), cache_k=None, cache_v=None,
                ffn=jnp.zeros((depth, bp, 2, d_ff), dt))
    st_s = dict(conv=state_conv, ret=state_ret, pool=state_pool, cache_k=cache_k, cache_v=cache_v, ffn=state_ffn)
    cfg_s = dict(tm_in=bs * ts, tn_even=1024, tn_odd=w_in_odd.shape[-1] // 2, blk_even=ts, blk_odd=ts,
                 tm_out=bs * ts, tk_out=512, tm_ffn=bs * ts, tf=512, tm_ple=bs * ts, tk_ple=512, kv_rows=kv_rows)
    cfg_p = dict(tm_in=1024, tn_even=1024, tn_odd=w_in_odd.shape[-1] // 2, blk_even=256, blk_odd=128,
                 tm_out=512, tk_out=None, tm_ffn=1024, tf=512, tm_ple=1024, tk_ple=None, kv_rows=kv_rows)
    xp = x_prompt.reshape(bp * sp, d)
    xs = x_sample.reshape(bs * ts, d)
    pp = p_prompt.reshape(depth, bp * sp, -1)
    ps = p_sample.reshape(depth, bs * ts, -1)
    new_p, new_s = [], []
    for i in range(depth):
        j = i // 2
        raw = dict(w_in=LayerWeight(w_in_even if i % 2 == 0 else w_in_odd, j),
                   w_out=LayerWeight(w_out_even if i % 2 == 0 else w_out_odd, j),
                   w_up=LayerWeight(w_up, i), w_down=LayerWeight(w_down, i),
                   w_gate=LayerWeight(w_ple_gate, i), w_proj=LayerWeight(w_ple_proj, i))
        sides = {name: raw[name] for name in (('w_up', 'w_down') if i % 2 == 0 else ('w_down',))}
        xs, st_mix_s, wq, _ = _mix_stage(i, xs, grp_s, st_s, raw, prm, cfg_s)
        xp, st_mix_p, _, cast = _mix_stage(i, xp, grp_p, st_p, wq, prm, cfg_p, sides)
        xs, ffn_s, wq = _ffn_stage(i, xs, ps, grp_s, st_s, {**raw, **cast}, prm, cfg_s)
        xp, ffn_p, _ = _ffn_stage(i, xp, pp, grp_p, st_p, wq, prm, cfg_p)
        new_s.append(dict(st_mix_s, ffn=ffn_s))
        new_p.append(dict(st_mix_p, ffn=ffn_p))

    def stacked(new, key):
        parts = [st[key] for st in new if key in st]
        return parts[0][None] if len(parts) == 1 else jnp.stack(parts)

    outs = [xp.reshape(bp, sp, d), xs.reshape(bs, ts, d)]
    for key in ('conv', 'ret', 'pool', 'k', 'v', 'ffn'):
        outs += [stacked(new_p, key), stacked(new_s, key)]
    return tuple(outs)
```

```python
import functools
import math

import jax
import jax.numpy as jnp
from jax import lax
from jax.experimental import pallas as pl
from jax.experimental.pallas import tpu as pltpu

CHUNK = 64
WINDOW = 128
PAST_LEN = 4096
EPS = 1e-6
ROPE_THETA = 10000.0
NEG_INF = -1e30
POOL_WINDOWS = (2, 4, 8, 16)
POOL_HIST = max(POOL_WINDOWS) - 1

V7X_VMEM_BYTES = 64 * 1024 * 1024
VMEM_LIMIT_BYTES = V7X_VMEM_BYTES - 6 * 1024 * 1024
MXU_COLS = 256
SUBLANES = 8

F32 = jnp.float32
BF16 = jnp.bfloat16


def _params():
    return pltpu.CompilerParams(vmem_limit_bytes=VMEM_LIMIT_BYTES)


def _rms(x, g):
    return x * lax.rsqrt(jnp.mean(x * x, axis=-1, keepdims=True) + EPS) * g


def _gelu_tanh(x):
    c1 = math.sqrt(2.0 / math.pi)
    half = 0.5 * x
    return half + half * jnp.tanh(x * (c1 + (c1 * 0.044715) * (x * x)))


ROW_CHUNK = 256


def _for_row_chunks(n_rows, fn):
    if n_rows <= ROW_CHUNK or n_rows % ROW_CHUNK:
        fn(slice(0, n_rows))
        return

    def body(r, carry):
        fn(pl.ds(pl.multiple_of(r * ROW_CHUNK, ROW_CHUNK), ROW_CHUNK))
        return carry

    lax.fori_loop(0, n_rows // ROW_CHUNK, body, 0)


def _resident_spec(block, index_map):
    return pl.BlockSpec(block, index_map, pipeline_mode=pl.Buffered(1))


def _row_tile(rows, want):
    t = min(rows, want)
    assert rows % t == 0, (rows, t)
    return t


class LayerWeight:
    def __init__(self, stack, layer):
        self.stack, self.layer = stack, layer
        self.shape = stack.shape[1:]

    def spec(self, block, index_map):
        layer = self.layer
        return pl.BlockSpec((None,) + block, lambda *g: (layer,) + index_map(*g))


def _weight_in(w, block, index_map):
    if isinstance(w, LayerWeight):
        return (w.stack, w.spec(block, index_map), pl.BlockSpec(block, index_map),
                jax.ShapeDtypeStruct(w.shape, BF16))
    return w, pl.BlockSpec(block, index_map), None, None


def _load_weight(w_ref, copy_ref):
    w = w_ref[...]
    if copy_ref is not None:
        w = w.astype(BF16)
        copy_ref[...] = w
    return w


BF16_SUBLANES = 16


def _side_casts(sides, n_steps, step_of):
    args, in_specs, out_specs, out_shapes = [], [], [], []
    for side in sides:
        rows, cols = side.shape
        n_slabs = max(n for n in range(1, n_steps + 1) if rows % n == 0 and (rows // n) % BF16_SUBLANES == 0)
        slab = rows // n_slabs
        index_map = lambda *g, last=n_slabs - 1: (jnp.minimum(step_of(*g), last), 0)
        args.append(side.stack)
        in_specs.append(side.spec((slab, cols), index_map))
        out_specs.append(pl.BlockSpec((slab, cols), index_map))
        out_shapes.append(jax.ShapeDtypeStruct((rows, cols), BF16))
    return args, in_specs, out_specs, out_shapes


def _run_side_casts(side_refs):
    n = len(side_refs) // 2
    for src, dst in zip(side_refs[:n], side_refs[n:]):
        dst[...] = src[...].astype(BF16)


def _norm_matmul_kernel(x_ref, g_ref, w_ref, *rest, emit):
    o_ref, copy_ref, h_ref = rest if emit else (rest[0], None, rest[1])

    @pl.when(pl.program_id(1) == 0)
    def _():
        def start(rows):
            h_ref[rows, :] = _rms(x_ref[rows, :], g_ref[...]).astype(BF16)

        _for_row_chunks(x_ref.shape[0], start)

    w = _load_weight(w_ref, copy_ref)
    o_ref[...] = jnp.dot(h_ref[...], w, preferred_element_type=F32)


def norm_matmul(x, g, w, *, tm, tn):
    rows, d = x.shape
    n = w.shape[1]
    tm = _row_tile(rows, tm)
    assert n % tn == 0
    w_arg, w_spec, copy_spec, copy_shape = _weight_in(w, (d, tn), lambda i, j: (0, j))
    emit = copy_spec is not None
    assert not emit or rows == tm, "a bf16 copy needs every weight tile visited exactly once"
    out_specs = [pl.BlockSpec((tm, tn), lambda i, j: (i, j))]
    out_shape = [jax.ShapeDtypeStruct((rows, n), F32)]
    if emit:
        out_specs.append(copy_spec)
        out_shape.append(copy_shape)
    out = pl.pallas_call(
        functools.partial(_norm_matmul_kernel, emit=emit),
        grid=(rows // tm, n // tn),
        in_specs=[
            pl.BlockSpec((tm, d), lambda i, j: (i, 0)),
            pl.BlockSpec((1, d), lambda i, j: (0, 0)),
            w_spec,
        ],
        out_specs=out_specs,
        out_shape=out_shape,
        scratch_shapes=[pltpu.VMEM((tm, d), BF16)],
        compiler_params=_params(),
        name="norm_matmul",
    )(x, g.reshape(1, d), w_arg)
    return (out[0], out[1]) if emit else (out[0], w)


def _even_in_proj_kernel(x_ref, g_ref, w_ref, cos_ref, sin_ref, *rest, emit, n_side, kinds, dk, k_scale):
    side_in, rest = rest[:n_side], rest[n_side:]
    o_ref, rest = rest[0], rest[1:]
    copy_ref, rest = (rest[0], rest[1:]) if emit else (None, rest)
    side_out, (h_ref,) = rest[:n_side], rest[n_side:]
    _run_side_casts(side_in + side_out)
    j = pl.program_id(1)
    tn = w_ref.shape[1]

    @pl.when(j == 0)
    def _():
        def start(rows):
            h_ref[rows, :] = _rms(x_ref[rows, :], g_ref[...]).astype(BF16)

        _for_row_chunks(x_ref.shape[0], start)

    def project():
        return jnp.dot(h_ref[...], _load_weight(w_ref, copy_ref), preferred_element_type=F32)

    def rope(acc, scale):
        cosb, sinb = cos_ref[...], sin_ref[...]
        for hh in range(tn // dk):
            seg = acc[:, hh * dk:(hh + 1) * dk]
            out = seg * cosb + pltpu.roll(seg, dk // 2, axis=1) * sinb
            if scale != 1.0:
                out = out * scale
            o_ref[:, hh * dk:(hh + 1) * dk] = out.astype(BF16)

    def finish(kind):
        acc = project()
        if kind == 'rope':
            rope(acc, 1.0)
        elif kind == 'rope_scaled':
            rope(acc, k_scale)
        elif kind == 'plain':
            o_ref[...] = acc.astype(BF16)
        else:
            assert kind == 'silu'
            o_ref[...] = (acc * jax.nn.sigmoid(acc)).astype(BF16)

    for kind in dict.fromkeys(kinds):
        tiles = [t for t, k in enumerate(kinds) if k == kind]
        here = functools.reduce(jnp.logical_or, [j == t for t in tiles])
        pl.when(here)(functools.partial(finish, kind))


def even_in_proj(x, g, w, cosb, sinb, *, tm, tn, d_a, n_heads, dk, dv, sides=()):
    rows, d = x.shape
    n = w.shape[1]
    tm = _row_tile(rows, tm)
    hk, hv = n_heads * dk, n_heads * dv
    assert n == 3 * d_a + 2 * hk + 2 * hv and tn % dk == 0
    assert all(part % tn == 0 for part in (d_a, hk, hv))
    kinds = (('plain',) * (3 * d_a // tn) + ('rope',) * (hk // tn) + ('rope_scaled',) * (hk // tn)
             + ('plain',) * (hv // tn) + ('silu',) * (hv // tn))
    w_arg, w_spec, copy_spec, copy_shape = _weight_in(w, (d, tn), lambda i, j: (0, j))
    emit = copy_spec is not None
    assert not emit or rows == tm, "a bf16 copy needs every weight tile visited exactly once"
    out_specs = [pl.BlockSpec((tm, tn), lambda i, j: (i, j))]
    out_shape = [jax.ShapeDtypeStruct((rows, n), BF16)]
    if emit:
        out_specs.append(copy_spec)
        out_shape.append(copy_shape)
    n_col = n // tn
    side_args, side_in, side_out, side_shapes = _side_casts(
        sides, (rows // tm) * n_col, lambda i, j: i * n_col + j)
    kern = functools.partial(_even_in_proj_kernel, emit=emit, n_side=len(sides), kinds=kinds, dk=dk,
                             k_scale=dk ** -0.5)
    out = pl.pallas_call(
        kern,
        grid=(rows // tm, n_col),
        in_specs=[
            pl.BlockSpec((tm, d), lambda i, j: (i, 0)),
            pl.BlockSpec((1, d), lambda i, j: (0, 0)),
            w_spec,
            pl.BlockSpec((tm, dk), lambda i, j: (i, 0)),
            pl.BlockSpec((tm, dk), lambda i, j: (i, 0)),
        ] + side_in,
        out_specs=out_specs + side_out,
        out_shape=out_shape + side_shapes,
        scratch_shapes=[pltpu.VMEM((tm, d), BF16)],
        compiler_params=_params(),
        name="even_in_proj",
    )(x, g.reshape(1, d), w_arg, cosb, sinb, *side_args)
    n_main = 2 if emit else 1
    return out[0], (out[1] if emit else w), list(out[n_main:])


def _even_mixer_kernel(p_ref, chist_ref, rstate_ref, wconv_ref, gn_ref,
                       mix_ref, cstate_ref, rout_ref, carry_ref, s_ref, dmat_ref, qdec_ref, kdec_ref,
                       *, n_heads, dk, dv, d_a):
    q0 = 3 * d_a
    k0 = q0 + n_heads * dk
    v0 = k0 + n_heads * dk
    g0 = v0 + n_heads * dv
    c = pl.program_id(1)
    last = pl.num_programs(1) - 1
    blk = p_ref.shape[0]

    @pl.when(c == 0)
    def _():
        carry_ref[...] = chist_ref[...]
        s_ref[...] = rstate_ref[...]

    u = p_ref[:, 2 * d_a:3 * d_a].astype(F32) * p_ref[:, 0:d_a].astype(F32)
    row = lax.broadcasted_iota(jnp.int32, u.shape, 0)
    h0 = carry_ref[0:1, :]
    h1 = carry_ref[1:2, :]
    prev1 = jnp.where(row == 0, h1, pltpu.roll(u, 1, axis=0))
    prev2 = jnp.where(row == 0, h0, jnp.where(row == 1, h1, pltpu.roll(u, 2, axis=0)))
    w = wconv_ref[...]
    conv = prev2 * w[0:1] + prev1 * w[1:2] + u * w[2:3]
    mix_ref[:, 0:d_a] = (p_ref[:, d_a:2 * d_a].astype(F32) * conv).astype(BF16)
    tail = u[blk - 2:blk, :]
    carry_ref[...] = tail

    @pl.when(c == last)
    def _():
        cstate_ref[...] = tail

    log_gamma = [math.log1p(-(2.0 ** (-5.0 - h))) for h in range(n_heads)]

    @pl.when((pl.program_id(0) == 0) & (c == 0))
    def _():
        rel = (lax.broadcasted_iota(jnp.int32, (blk, blk), 0)
               - lax.broadcasted_iota(jnp.int32, (blk, blk), 1)).astype(F32)
        ii = lax.broadcasted_iota(jnp.int32, (blk, dk), 0).astype(F32)
        for h, lg in enumerate(log_gamma):
            dmat_ref[h] = jnp.where(rel >= 0, jnp.exp(jnp.maximum(rel, 0.0) * lg), 0.0)
            qdec_ref[h] = jnp.exp((ii + 1.0) * lg)
            kdec_ref[h] = jnp.exp((blk - 1.0 - ii) * lg)

    stage = []
    for h in range(n_heads):
        qr = p_ref[:, q0 + h * dk:q0 + (h + 1) * dk]
        kr = p_ref[:, k0 + h * dk:k0 + (h + 1) * dk]
        vb = p_ref[:, v0 + h * dv:v0 + (h + 1) * dv]
        sc = lax.dot_general(qr, kr, (((1,), (1,)), ((), ())), preferred_element_type=F32)
        st = s_ref[h]
        inter = jnp.dot((qr.astype(F32) * qdec_ref[h]).astype(BF16), st.astype(BF16),
                        preferred_element_type=F32)
        kv = lax.dot_general((kr.astype(F32) * kdec_ref[h]).astype(BF16), vb, (((0,), (0,)), ((), ())),
                             preferred_element_type=F32)
        s_ref[h] = math.exp(blk * log_gamma[h]) * st + kv
        stage.append((sc, inter, vb))
    for h, (sc, inter, vb) in enumerate(stage):
        o = jnp.dot((sc * dmat_ref[h]).astype(BF16), vb, preferred_element_type=F32) + inter
        on = o * lax.rsqrt(jnp.mean(o * o, axis=-1, keepdims=True) + EPS) * gn_ref[h:h + 1, :]
        gate = p_ref[:, g0 + h * dv:g0 + (h + 1) * dv].astype(F32)
        mix_ref[:, d_a + h * dv:d_a + (h + 1) * dv] = (on * gate).astype(BF16)

    @pl.when(c == last)
    def _():
        rout_ref[...] = s_ref[...]


def even_mixer(p, conv_hist, ret_state, w_conv, gn, *, blk):
    b, t, width = p.shape
    n_heads, dk, dv = ret_state.shape[1:]
    d_a = w_conv.shape[1]
    hv = n_heads * dv
    assert width == 3 * d_a + 2 * n_heads * dk + 2 * hv
    blk = _row_tile(t, blk)
    nc = t // blk
    kern = functools.partial(_even_mixer_kernel, n_heads=n_heads, dk=dk, dv=dv, d_a=d_a)
    return pl.pallas_call(
        kern,
        grid=(b, nc),
        in_specs=[
            pl.BlockSpec((None, blk, width), lambda bi, ci: (bi, ci, 0)),
            pl.BlockSpec((None, 2, d_a), lambda bi, ci: (bi, 0, 0)),
            pl.BlockSpec((None, n_heads, dk, dv), lambda bi, ci: (bi, 0, 0, 0)),
            pl.BlockSpec((3, d_a), lambda bi, ci: (0, 0)),
            pl.BlockSpec((n_heads, dv), lambda bi, ci: (0, 0)),
        ],
        out_specs=[
            pl.BlockSpec((None, blk, d_a + hv), lambda bi, ci: (bi, ci, 0)),
            pl.BlockSpec((None, 2, d_a), lambda bi, ci: (bi, 0, 0)),
            pl.BlockSpec((None, n_heads, dk, dv), lambda bi, ci: (bi, 0, 0, 0)),
        ],
        out_shape=[
            jax.ShapeDtypeStruct((b, t, d_a + hv), BF16),
            jax.ShapeDtypeStruct((b, 2, d_a), F32),
            jax.ShapeDtypeStruct((b, n_heads, dk, dv), F32),
        ],
        scratch_shapes=[pltpu.VMEM((2, d_a), F32), pltpu.VMEM((n_heads, dk, dv), F32),
                        pltpu.VMEM((n_heads, blk, blk), F32), pltpu.VMEM((n_heads, blk, dk), F32),
                        pltpu.VMEM((n_heads, blk, dk), F32)],
        compiler_params=_params(),
        name="even_mixer",
    )(p, conv_hist, ret_state, w_conv, gn)


def _odd_mixer_kernel(sinks_ref, p_ref, cos_ref, sin_ref,
                      phist_ref, ck_ref, cv_ref, wpool_ref, pscale_ref,
                      mix_ref, pstate_ref, kstate_ref, vstate_ref,
                      ext_ref, kbuf_ref, vbuf_ref,
                      *, d_c, n_q_heads, n_kv_heads, hd, pos0, has_cache, n_state_rows):
    c = pl.program_id(1)
    last = pl.num_programs(1) - 1
    blk = p_ref.shape[0]
    hist_rows = POOL_HIST + 1
    d_cg = d_c // len(POOL_WINDOWS)
    lanes = 2 * hd
    q0 = d_c
    k0 = q0 + n_q_heads * hd
    v0 = k0 + lanes

    @pl.when(c == 0)
    def _():
        ext_ref[0:1, :] = jnp.zeros((1, d_c), F32)
        ext_ref[1:hist_rows, :] = phist_ref[...]
        kbuf_ref[0:WINDOW, :] = ck_ref[...]
        vbuf_ref[0:WINDOW, :] = cv_ref[...]

    u = p_ref[:, 0:d_c]
    ext_ref[hist_rows:hist_rows + blk, :] = u
    pos = pos0 + c * blk + lax.broadcasted_iota(jnp.int32, (blk, 1), 0)
    for gi, win in enumerate(POOL_WINDOWS):
        cols = slice(gi * d_cg, (gi + 1) * d_cg)
        s = ext_ref[:, cols]
        span = 1
        while span < win:
            s = s + pltpu.roll(s, span, axis=0)
            span *= 2
        cnt = jnp.minimum(pos + 1, win).astype(F32)
        d = s[hist_rows:, :] / cnt - u[:, cols]
        y = jnp.dot(d.astype(BF16), wpool_ref[gi], preferred_element_type=F32)
        mix_ref[:, cols] = (y * pscale_ref[:, cols]).astype(BF16)
    tail = ext_ref[blk:blk + hist_rows, :]
    ext_ref[0:hist_rows, :] = tail

    cos4 = cos_ref[...]
    sin4 = sin_ref[...]
    lane = lax.broadcasted_iota(jnp.int32, (1, lanes), 1)
    first_half = jnp.bitwise_and(lane, hd - 1) < (hd // 2)
    left = lane < hd

    def rope(x):
        rot = jnp.where(first_half, pltpu.roll(x, lanes - hd // 2, axis=1), pltpu.roll(x, hd // 2, axis=1))
        return x * cos4 + rot * sin4

    kr = rope(p_ref[:, k0:k0 + lanes])
    vv = p_ref[:, v0:v0 + lanes]
    kbuf_ref[WINDOW:WINDOW + blk, :] = kr
    vbuf_ref[WINDOW:WINDOW + blk, :] = vv
    kall = kbuf_ref[...]
    vall = vbuf_ref[...]
    kswap = pltpu.roll(kall, hd, axis=1)
    vswap = pltpu.roll(vall, hd, axis=1)
    nk = WINDOW + blk
    chunk_shift = CHUNK.bit_length() - 1
    qi = jnp.right_shift(lax.broadcasted_iota(jnp.int32, (blk, 1), 0), chunk_shift)
    kj = lax.broadcasted_iota(jnp.int32, (1, nk), 1)
    kc = jnp.right_shift(kj, chunk_shift) - WINDOW // CHUNK
    ok = (kc <= qi) & (kc >= qi - WINDOW // CHUNK)
    if not has_cache:
        ok = ok & ((kj >= WINDOW) | (c > 0))
    scale = hd ** -0.5
    group = n_q_heads // n_kv_heads
    zeros = jnp.zeros_like(vall)
    kv_ops = [
        (jnp.where(left, kall, kswap).astype(BF16), jnp.where(left, vall, zeros).astype(BF16),
         jnp.where(left, zeros, vswap).astype(BF16)),
        (jnp.where(left, kswap, kall).astype(BF16), jnp.where(left, vswap, zeros).astype(BF16),
         jnp.where(left, zeros, vall).astype(BF16)),
    ]
    scores = []
    for head in range(n_q_heads):
        kvh = head // group
        if head % 2 == 0:
            qr = rope(p_ref[:, q0 + (head // 2) * lanes:q0 + (head // 2 + 1) * lanes])
        qh = jnp.where(left if head % 2 == 0 else jnp.logical_not(left), qr, 0.0).astype(BF16)
        scores.append(lax.dot_general(qh, kv_ops[kvh][0], (((1,), (1,)), ((), ())),
                                      preferred_element_type=F32))
    probs = []
    for head, sc in enumerate(scores):
        sc = jnp.where(ok, sc * scale, NEG_INF)
        sink = sinks_ref[head]
        m = jnp.maximum(jnp.max(sc, axis=-1, keepdims=True), sink)
        e = jnp.exp(sc - m)
        den = jnp.sum(e, axis=-1, keepdims=True) + jnp.exp(sink - m)
        probs.append((e / den).astype(BF16))
    for pair in range(n_q_heads // 2):
        kvh = (2 * pair) // group
        assert (2 * pair + 1) // group == kvh
        out = (jnp.dot(probs[2 * pair], kv_ops[kvh][1], preferred_element_type=F32)
               + jnp.dot(probs[2 * pair + 1], kv_ops[kvh][2], preferred_element_type=F32))
        mix_ref[:, d_c + pair * lanes:d_c + (pair + 1) * lanes] = out.astype(BF16)
    ktail = kbuf_ref[blk:blk + WINDOW, :]
    vtail = vbuf_ref[blk:blk + WINDOW, :]
    kbuf_ref[0:WINDOW, :] = ktail
    vbuf_ref[0:WINDOW, :] = vtail

    @pl.when(c == last)
    def _():
        pstate_ref[...] = ext_ref[1:hist_rows, :]
        kstate_ref[...] = kbuf_ref[nk - n_state_rows:nk, :]
        vstate_ref[...] = vbuf_ref[nk - n_state_rows:nk, :]


def odd_mixer(p, cos4, sin4, pool_hist, cache_k, cache_v, w_pool, pool_scale, sinks, *,
              blk, pos0, has_cache, n_state_rows):
    b, t, width = p.shape
    d_c = pool_scale.shape[-1]
    n_q_heads = sinks.shape[-1]
    lanes = cache_k.shape[-1]
    n_kv_heads = 2
    hd = lanes // n_kv_heads
    d_q = n_q_heads * hd
    assert d_q == d_c and d_c % lanes == 0 and width == d_c + d_q + 2 * lanes
    blk = _row_tile(t, blk)
    nc = t // blk
    assert blk % CHUNK == 0 or nc == 1
    kern = functools.partial(_odd_mixer_kernel, d_c=d_c, n_q_heads=n_q_heads, n_kv_heads=n_kv_heads,
                             hd=hd, pos0=pos0, has_cache=has_cache, n_state_rows=n_state_rows)
    return pl.pallas_call(
        kern,
        grid=(b, nc),
        in_specs=[
            pl.BlockSpec(memory_space=pltpu.SMEM),
            pl.BlockSpec((None, blk, width), lambda bi, ci: (bi, ci, 0)),
            pl.BlockSpec((blk, lanes), lambda bi, ci: (ci, 0)),
            pl.BlockSpec((blk, lanes), lambda bi, ci: (ci, 0)),
            pl.BlockSpec((None, POOL_HIST, d_c), lambda bi, ci: (bi, 0, 0)),
            pl.BlockSpec((None, WINDOW, lanes), lambda bi, ci: (bi, 0, 0)),
            pl.BlockSpec((None, WINDOW, lanes), lambda bi, ci: (bi, 0, 0)),
            pl.BlockSpec(w_pool.shape, lambda bi, ci: (0, 0, 0)),
            pl.BlockSpec((1, d_c), lambda bi, ci: (0, 0)),
        ],
        out_specs=[
            pl.BlockSpec((None, blk, d_c + d_q), lambda bi, ci: (bi, ci, 0)),
            pl.BlockSpec((None, POOL_HIST, d_c), lambda bi, ci: (bi, 0, 0)),
            pl.BlockSpec((None, n_state_rows, lanes), lambda bi, ci: (bi, 0, 0)),
            pl.BlockSpec((None, n_state_rows, lanes), lambda bi, ci: (bi, 0, 0)),
        ],
        out_shape=[
            jax.ShapeDtypeStruct((b, t, d_c + d_q), BF16),
            jax.ShapeDtypeStruct((b, POOL_HIST, d_c), F32),
            jax.ShapeDtypeStruct((b, n_state_rows, lanes), F32),
            jax.ShapeDtypeStruct((b, n_state_rows, lanes), F32),
        ],
        scratch_shapes=[
            pltpu.VMEM((POOL_HIST + 1 + blk, d_c), F32),
            pltpu.VMEM((WINDOW + blk, lanes), F32),
            pltpu.VMEM((WINDOW + blk, lanes), F32),
        ],
        compiler_params=_params(),
        name="odd_mixer",
    )(sinks, p, cos4, sin4, pool_hist, cache_k, cache_v, w_pool, pool_scale.reshape(1, d_c))


def _accumulate(o_ref, part, k):
    @pl.when(k == 0)
    def _():
        o_ref[...] = part

    @pl.when(k > 0)
    def _():
        o_ref[...] += part


def _out_proj_kernel(mix_ref, w_ref, x_ref, g_ref, *rest, n_k, emit, n_side):
    side_in, rest = rest[:n_side], rest[n_side:]
    o_ref, rest = rest[0], rest[1:]
    copy_ref, side_out = (rest[0], rest[1:]) if emit else (None, rest)
    _run_side_casts(side_in + side_out)
    k = pl.program_id(1)
    part = jnp.dot(mix_ref[...], _load_weight(w_ref, copy_ref), preferred_element_type=F32)
    if n_k == 1:
        o_ref[...] = x_ref[...] + _rms(part, g_ref[...])
    else:
        _accumulate(o_ref, part, k)

        @pl.when(k == n_k - 1)
        def _():
            o_ref[...] = x_ref[...] + _rms(o_ref[...], g_ref[...])


def out_proj(mix, w, x, g, *, tm, tk=None, sides=()):
    rows, kdim = mix.shape
    d = x.shape[1]
    tm = _row_tile(rows, tm)
    tk = kdim if tk is None else tk
    assert kdim % tk == 0
    w_arg, w_spec, copy_spec, copy_shape = _weight_in(w, (tk, d), lambda i, k: (k, 0))
    emit = copy_spec is not None
    assert not emit or rows == tm, "a bf16 copy needs every weight tile visited exactly once"
    if not emit and tk == kdim:
        w_spec = _resident_spec((tk, d), lambda i, k: (0, 0))
    out_specs = [pl.BlockSpec((tm, d), lambda i, k: (i, 0))]
    out_shape = [jax.ShapeDtypeStruct((rows, d), F32)]
    if emit:
        out_specs.append(copy_spec)
        out_shape.append(copy_shape)
    n_k = kdim // tk
    side_args, side_in, side_out, side_shapes = _side_casts(
        sides, (rows // tm) * n_k, lambda i, k: i * n_k + k)
    out = pl.pallas_call(
        functools.partial(_out_proj_kernel, n_k=n_k, emit=emit, n_side=len(sides)),
        grid=(rows // tm, n_k),
        in_specs=[
            pl.BlockSpec((tm, tk), lambda i, k: (i, k)),
            w_spec,
            pl.BlockSpec((tm, d), lambda i, k: (i, 0)),
            pl.BlockSpec((1, d), lambda i, k: (0, 0)),
        ] + side_in,
        out_specs=out_specs + side_out,
        out_shape=out_shape + side_shapes,
        compiler_params=_params(),
        name="out_proj",
    )(mix, w_arg, x, g.reshape(1, d), *side_args)
    n_main = 2 if emit else 1
    return out[0], (out[1] if emit else w), list(out[n_main:])


def _ffn_kernel(x_ref, gpre_ref, wa_ref, wg_ref, wconv_ref, wd_ref, gpost_ref, hist_ref,
                o_ref, state_ref, *rest, n_seq, blocks_per_seq, sub, emit_up, emit_down):
    (wa_copy_ref, wg_copy_ref), rest = (rest[:2], rest[2:]) if emit_up else ((None, None), rest)
    wd_copy_ref, rest = (rest[0], rest[1:]) if emit_down else (None, rest)
    h_ref, carry_ref = rest
    i = pl.program_id(0)
    j = pl.program_id(1)
    last_j = pl.num_programs(1) - 1
    tm = x_ref.shape[0]
    tf = wa_ref.shape[1]
    rows_per_seq = tm // n_seq

    @pl.when(j == 0)
    def _():
        def start(rows):
            x = x_ref[rows, :]
            h_ref[rows, :] = _rms(x, gpre_ref[...]).astype(BF16)
            o_ref[rows, :] = jnp.zeros_like(x)

        _for_row_chunks(tm, start)

    if blocks_per_seq > 1:
        @pl.when(i % blocks_per_seq == 0)
        def _():
            carry_ref[j] = hist_ref[0]

    h = h_ref[...]
    assert rows_per_seq & (rows_per_seq - 1) == 0
    t = jnp.bitwise_and(lax.broadcasted_iota(jnp.int32, (tm, 1), 0), rows_per_seq - 1)
    groups = [slice(s * sub, (s + 1) * sub) for s in range(tf // sub)]
    wa, wg = ((_load_weight(wa_ref, wa_copy_ref), _load_weight(wg_ref, wg_copy_ref)) if emit_up
              else (wa_ref, wg_ref))
    wd = _load_weight(wd_ref, wd_copy_ref) if emit_down else wd_ref
    pending = None
    for cols in groups:
        a = jnp.dot(h, wa[:, cols], preferred_element_type=F32)
        val = jnp.dot(h, wg[:, cols], preferred_element_type=F32)
        if pending is not None:
            o_ref[...] += jnp.dot(pending[0], wd[pending[1], :], preferred_element_type=F32)
        if blocks_per_seq > 1:
            hist0 = carry_ref[j, 0:1, cols]
            hist1 = carry_ref[j, 1:2, cols]
        elif n_seq == 1:
            hist0 = hist_ref[0, 0:1, cols]
            hist1 = hist_ref[0, 1:2, cols]
        else:
            hist = hist_ref[:, :, cols]
            hist0 = jnp.broadcast_to(hist[:, 0:1, :], (n_seq, rows_per_seq, sub)).reshape(tm, sub)
            hist1 = jnp.broadcast_to(hist[:, 1:2, :], (n_seq, rows_per_seq, sub)).reshape(tm, sub)
        prev1, prev2 = pltpu.roll(a, 1, axis=0), pltpu.roll(a, 2, axis=0)
        if n_seq == 1:
            t8 = t[0:SUBLANES]
            top1 = jnp.where(t8 == 0, hist1, prev1[0:SUBLANES])
            top2 = jnp.where(t8 == 0, hist0, jnp.where(t8 == 1, hist1, prev2[0:SUBLANES]))
            prev1 = jnp.concatenate([top1, prev1[SUBLANES:]], axis=0)
            prev2 = jnp.concatenate([top2, prev2[SUBLANES:]], axis=0)
        else:
            prev1 = jnp.where(t == 0, hist1, prev1)
            prev2 = jnp.where(t == 0, hist0, jnp.where(t == 1, hist1, prev2))
        w = wconv_ref[:, cols]
        conv = prev2 * w[0:1] + prev1 * w[1:2] + a * w[2:3]
        pending = ((_gelu_tanh(conv) * val).astype(BF16), cols)
        tail = a.reshape(n_seq, rows_per_seq, sub)[:, rows_per_seq - 2:, :]
        state_ref[j, :, :, cols] = tail
        if blocks_per_seq > 1:
            carry_ref[j, :, cols] = tail[0]
    o_ref[...] += jnp.dot(pending[0], wd[pending[1], :], preferred_element_type=F32)

    @pl.when(j == last_j)
    def _():
        def finish(rows):
            o_ref[rows, :] = x_ref[rows, :] + _rms(o_ref[rows, :], gpost_ref[...])

        _for_row_chunks(tm, finish)


def conv_ffn(x, g_pre, w_up, w_conv, w_down, g_post, hist, *, seq_len, tm, tf):
    rows, d = x.shape
    d_ff = w_down.shape[0]
    tm = _row_tile(rows, tm)
    assert d_ff % tf == 0 and tf % MXU_COLS == 0
    n_ff = d_ff // tf
    emit_up = isinstance(w_up, LayerWeight)
    emit_down = isinstance(w_down, LayerWeight)
    assert not (emit_up or emit_down) or rows == tm, "a bf16 copy needs every weight tile visited exactly once"
    gate_tile, value_tile = (lambda i, j: (0, j)), (lambda i, j: (0, j + n_ff))
    copy_specs, copy_shapes = [], []
    if emit_up:
        up_args = (w_up.stack, w_up.stack)
        up_specs = [w_up.spec((d, tf), gate_tile), w_up.spec((d, tf), value_tile)]
        copy_specs += [pl.BlockSpec((d, tf), gate_tile)] * 2
        copy_shapes += [jax.ShapeDtypeStruct((d, d_ff), BF16)] * 2
    elif isinstance(w_up, tuple):
        up_args = w_up
        up_specs = [pl.BlockSpec((d, tf), gate_tile)] * 2
    else:
        up_args = (w_up, w_up)
        up_specs = [pl.BlockSpec((d, tf), gate_tile), pl.BlockSpec((d, tf), value_tile)]
    wd_arg, wd_spec, wd_copy_spec, wd_copy_shape = _weight_in(w_down, (tf, d), lambda i, j: (j, 0))
    if emit_down:
        copy_specs.append(wd_copy_spec)
        copy_shapes.append(wd_copy_shape)
    if seq_len >= tm:
        assert seq_len % tm == 0
        n_seq, blocks_per_seq = 1, seq_len // tm
        seq_of = lambda i: i // blocks_per_seq
    else:
        assert tm % seq_len == 0
        n_seq, blocks_per_seq = tm // seq_len, 1
        seq_of = lambda i: i
    kern = functools.partial(_ffn_kernel, n_seq=n_seq, blocks_per_seq=blocks_per_seq, sub=MXU_COLS,
                             emit_up=emit_up, emit_down=emit_down)
    out = pl.pallas_call(
        kern,
        grid=(rows // tm, n_ff),
        in_specs=[
            (_resident_spec if rows == tm else pl.BlockSpec)((tm, d), lambda i, j: (i, 0)),
            pl.BlockSpec((1, d), lambda i, j: (0, 0)),
            up_specs[0],
            up_specs[1],
            pl.BlockSpec((3, tf), lambda i, j: (0, j)),
            wd_spec,
            pl.BlockSpec((1, d), lambda i, j: (0, 0)),
            pl.BlockSpec((n_seq, 2, tf), lambda i, j: (seq_of(i), 0, j)),
        ],
        out_specs=[
            pl.BlockSpec((tm, d), lambda i, j: (i, 0)),
            pl.BlockSpec((n_ff, n_seq, 2, tf), lambda i, j: (0, seq_of(i), 0, 0)),
        ] + copy_specs,
        out_shape=[
            jax.ShapeDtypeStruct((rows, d), F32),
            jax.ShapeDtypeStruct((n_ff, rows // seq_len, 2, tf), F32),
        ] + copy_shapes,
        scratch_shapes=[pltpu.VMEM((tm, d), BF16), pltpu.VMEM((n_ff, 2, tf), F32)],
        compiler_params=_params(),
        name="conv_ffn",
    )(x, g_pre.reshape(1, d), *up_args, w_conv, wd_arg, g_post.reshape(1, d), hist)
    state = jnp.moveaxis(out[1], 0, 2).reshape(rows // seq_len, 2, d_ff)
    copies = list(out[2:])
    w_up_q = (copies.pop(0), copies.pop(0)) if emit_up else w_up
    w_down_q = copies.pop(0) if emit_down else w_down
    return out[0], state, (w_up_q, w_down_q)


def _ple_kernel(x_ref, p_ref, gpre_ref, wgate_ref, wproj_ref, gpost_ref, o_ref, *rest, n_k, emit):
    wgate_copy_ref, wproj_copy_ref = rest[:2] if emit else (None, None)
    h_ref = rest[-1] if n_k > 1 else None
    k = pl.program_id(1)
    tk = wgate_ref.shape[0]

    def finish(pre):
        emb = jnp.dot(p_ref[...].astype(BF16), _load_weight(wproj_ref, wproj_copy_ref),
                      preferred_element_type=F32)
        o_ref[...] = x_ref[...] + _rms(jax.nn.sigmoid(pre) * emb, gpost_ref[...])

    if n_k == 1:
        h = _rms(x_ref[...], gpre_ref[...]).astype(BF16)
        finish(jnp.dot(h, _load_weight(wgate_ref, wgate_copy_ref), preferred_element_type=F32))
    else:
        @pl.when(k == 0)
        def _():
            h = _rms(x_ref[...], gpre_ref[...]).astype(BF16)
            for kk in range(n_k):
                h_ref[kk] = h[:, kk * tk:(kk + 1) * tk]

        _accumulate(o_ref, jnp.dot(h_ref[k], _load_weight(wgate_ref, wgate_copy_ref),
                                   preferred_element_type=F32), k)

        @pl.when(k == n_k - 1)
        def _():
            finish(o_ref[...])


def per_layer_embedding(x, p, layer, g_pre, w_gate, w_proj, g_post, *, tm, tk=None):
    rows, d = x.shape
    pd = p.shape[-1]
    tm = _row_tile(rows, tm)
    tk = d if tk is None else tk
    assert d % tk == 0
    n_k = d // tk
    wg_arg, wg_spec, wg_copy_spec, wg_copy_shape = _weight_in(w_gate, (tk, d), lambda i, k: (k, 0))
    wp_arg, wp_spec, wp_copy_spec, wp_copy_shape = _weight_in(w_proj, (pd, d), lambda i, k: (0, 0))
    emit = wg_copy_spec is not None
    assert emit == (wp_copy_spec is not None)
    assert not emit or rows == tm, "a bf16 copy needs every weight tile visited exactly once"
    if not emit:
        wp_spec = _resident_spec((pd, d), lambda i, k: (0, 0))
        if n_k == 1:
            wg_spec = _resident_spec((tk, d), lambda i, k: (0, 0))
    out_specs = [pl.BlockSpec((tm, d), lambda i, k: (i, 0))]
    out_shape = [jax.ShapeDtypeStruct((rows, d), F32)]
    if emit:
        out_specs += [wg_copy_spec, wp_copy_spec]
        out_shape += [wg_copy_shape, wp_copy_shape]
    out = pl.pallas_call(
        functools.partial(_ple_kernel, n_k=n_k, emit=emit),
        grid=(rows // tm, n_k),
        in_specs=[
            pl.BlockSpec((tm, d), lambda i, k: (i, 0)),
            pl.BlockSpec((None, tm, pd), lambda i, k: (layer, i, 0)),
            pl.BlockSpec((1, d), lambda i, k: (0, 0)),
            wg_spec,
            wp_spec,
            pl.BlockSpec((1, d), lambda i, k: (0, 0)),
        ],
        out_specs=out_specs,
        out_shape=out_shape,
        scratch_shapes=[pltpu.VMEM((n_k, tm, tk), BF16)] if n_k > 1 else [],
        compiler_params=_params(),
        name="per_layer_embedding",
    )(x, p, g_pre.reshape(1, d), wg_arg, wp_arg, g_post.reshape(1, d))
    return (out[0], (out[1], out[2])) if emit else (out[0], (w_gate, w_proj))


def _rope_cos_sin(pos, dim):
    inv = 1.0 / (ROPE_THETA ** (jnp.arange(0, dim, 2, dtype=F32) / dim))
    ang = pos.astype(F32)[:, None] * inv[None, :]
    return jnp.cos(ang), jnp.sin(ang)


def _rope_layout(cos, sin, reps):
    return (jnp.tile(jnp.concatenate([cos, cos], axis=-1), (1, reps)),
            jnp.tile(jnp.concatenate([-sin, sin], axis=-1), (1, reps)))


def _mix_stage(i, x, grp, states, wts, prm, cfg, sides=None):
    b, t = grp['b'], grp['t']
    rows = b * t
    j = i // 2
    wq = {}
    sides = dict(sides or {})
    cast = {}
    if i % 2 == 0:
        n_heads, dk, dv = states['ret'].shape[2:]
        names = [n for n in sides if n == 'w_up']
        proj, wq['w_in'], done = even_in_proj(
            x, prm['norm_mix_pre'][i], wts['w_in'], *grp['even_tabs'], tm=cfg['tm_in'], tn=cfg['tn_even'],
            d_a=prm['w_conv_a'].shape[-1], n_heads=n_heads, dk=dk, dv=dv, sides=[sides.pop(n) for n in names])
        cast.update(zip(names, done))
        mix, c_s, r_s = even_mixer(proj.reshape(b, t, -1), states['conv'][j], states['ret'][j],
                                   prm['w_conv_a'][j], prm['ret_gn'][j], blk=cfg['blk_even'])
        new = dict(conv=c_s, ret=r_s)
    else:
        proj, wq['w_in'] = norm_matmul(x, prm['norm_mix_pre'][i], wts['w_in'], tm=cfg['tm_in'], tn=cfg['tn_odd'])
        lanes = grp['kv_lanes']
        has_cache = states['cache_k'] is not None
        if has_cache:
            ck = states['cache_k'][j].reshape(b, WINDOW, lanes)
            cv = states['cache_v'][j].reshape(b, WINDOW, lanes)
            n_state_rows = t
        else:
            ck = cv = jnp.zeros((b, WINDOW, lanes), F32)
            n_state_rows = cfg['kv_rows']
        mix, p_s, k_s, v_s = odd_mixer(proj.reshape(b, t, -1), *grp['odd_tabs'], states['pool'][j], ck, cv,
                                       prm['w_pool'][j], prm['pool_scale'][j], prm['sinks'][j],
                                       blk=cfg['blk_odd'], pos0=grp['pos0'], has_cache=has_cache,
                                       n_state_rows=n_state_rows)
        kv_shape = (b, n_state_rows, 2, lanes // 2)
        new = dict(pool=p_s, k=k_s.reshape(kv_shape), v=v_s.reshape(kv_shape))
    names = list(sides)
    x, wq['w_out'], done = out_proj(mix.reshape(rows, -1), wts['w_out'], x, prm['norm_mix_post'][i],
                                    tm=cfg['tm_out'], tk=cfg['tk_out'], sides=[sides[n] for n in names])
    cast.update(zip(names, done))
    return x, new, wq, cast


def _ffn_stage(i, x, p, grp, states, wts, prm, cfg):
    wq = {}
    x, ffn_state, (wq['w_up'], wq['w_down']) = conv_ffn(
        x, prm['norm_ffn_pre'][i], wts['w_up'], prm['w_conv_ffn'][i], wts['w_down'], prm['norm_ffn_post'][i],
        states['ffn'][i], seq_len=grp['t'], tm=cfg['tm_ffn'], tf=cfg['tf'])
    x, (wq['w_gate'], wq['w_proj']) = per_layer_embedding(
        x, p, i, prm['norm_ple_pre'][i], wts['w_gate'], wts['w_proj'], prm['norm_ple_post'][i],
        tm=cfg['tm_ple'], tk=cfg['tk_ple'])
    return x, ffn_state, wq


def kernel(x_prompt, x_sample, state_conv, state_ret, state_pool, cache_k, cache_v, state_ffn, p_prompt, p_sample, norm_mix_pre, norm_mix_post, norm_ffn_pre, norm_ffn_post, norm_ple_pre, norm_ple_post, w_in_even, w_conv_a, ret_gn, w_out_even, w_in_odd, w_pool, pool_scale, sinks, w_out_odd, w_up, w_conv_ffn, w_down, w_ple_gate, w_ple_proj):
    depth = norm_mix_pre.shape[0]
    n_even, n_odd = w_in_even.shape[0], w_in_odd.shape[0]
    d = x_prompt.shape[-1]
    d_a = w_conv_a.shape[-1]
    n_heads, dk, dv = state_ret.shape[2:]
    d_c = pool_scale.shape[-1]
    kv_rows, n_kv, hd = cache_k.shape[2:]
    d_ff = w_conv_ffn.shape[-1]
    lanes = n_kv * hd
    dt = x_prompt.dtype

    prm = dict(norm_mix_pre=norm_mix_pre, norm_mix_post=norm_mix_post, norm_ffn_pre=norm_ffn_pre,
               norm_ffn_post=norm_ffn_post, norm_ple_pre=norm_ple_pre, norm_ple_post=norm_ple_post,
               w_conv_a=w_conv_a, ret_gn=ret_gn, w_pool=w_pool.astype(BF16), pool_scale=pool_scale, sinks=sinks,
               w_conv_ffn=w_conv_ffn)

    def group(x, pos0):
        b, t = x.shape[:2]
        pos = pos0 + jnp.arange(t, dtype=jnp.int32)
        return dict(b=b, t=t, pos0=pos0, kv_lanes=lanes,
                    even_tabs=tuple(jnp.tile(tab, (b, 1)) for tab in _rope_layout(*_rope_cos_sin(pos, dk), 1)),
                    odd_tabs=_rope_layout(*_rope_cos_sin(pos, hd), lanes // hd))

    bp, sp = x_prompt.shape[:2]
    bs, ts = x_sample.shape[:2]
    grp_p, grp_s = group(x_prompt, 0), group(x_sample, PAST_LEN)
    st_p = dict(conv=jnp.zeros((n_even, bp, 2, d_a), dt), ret=jnp.zeros((n_even, bp, n_heads, dk, dv), dt),
                pool=jnp.zeros((n_odd, bp, POOL_HIST, d_c), dt), cache_k=None, cache_v=None,
                ffn=jnp.zeros((depth, bp, 2, d_ff), dt))
    st_s = dict(conv=state_conv, ret=state_ret, pool=state_pool, cache_k=cache_k, cache_v=cache_v, ffn=state_ffn)
    cfg_s = dict(tm_in=bs * ts, tn_even=1024, tn_odd=w_in_odd.shape[-1] // 2, blk_even=ts, blk_odd=ts,
                 tm_out=bs * ts, tk_out=512, tm_ffn=bs * ts, tf=512, tm_ple=bs * ts, tk_ple=512, kv_rows=kv_rows)
    cfg_p = dict(tm_in=1024, tn_even=1024, tn_odd=w_in_odd.shape[-1], blk_even=256, blk_odd=128,
                 tm_out=512, tk_out=None, tm_ffn=1024, tf=512, tm_ple=1024, tk_ple=None, kv_rows=kv_rows)
    xp = x_prompt.reshape(bp * sp, d)
    xs = x_sample.reshape(bs * ts, d)
    pp = p_prompt.reshape(depth, bp * sp, -1)
    ps = p_sample.reshape(depth, bs * ts, -1)
    new_p, new_s = [], []
    for i in range(depth):
        j = i // 2
        raw = dict(w_in=LayerWeight(w_in_even if i % 2 == 0 else w_in_odd, j),
                   w_out=LayerWeight(w_out_even if i % 2 == 0 else w_out_odd, j),
                   w_up=LayerWeight(w_up, i), w_down=LayerWeight(w_down, i),
                   w_gate=LayerWeight(w_ple_gate, i), w_proj=LayerWeight(w_ple_proj, i))
        sides = {name: raw[name] for name in (('w_up', 'w_down') if i % 2 == 0 else ('w_down',))}
        xs, st_mix_s, wq, _ = _mix_stage(i, xs, grp_s, st_s, raw, prm, cfg_s)
        xp, st_mix_p, _, cast = _mix_stage(i, xp, grp_p, st_p, wq, prm, cfg_p, sides)
        xs, ffn_s, wq = _ffn_stage(i, xs, ps, grp_s, st_s, {**raw, **cast}, prm, cfg_s)
        xp, ffn_p, _ = _ffn_stage(i, xp, pp, grp_p, st_p, wq, prm, cfg_p)
        new_s.append(dict(st_mix_s, ffn=ffn_s))
        new_p.append(dict(st_mix_p, ffn=ffn_p))

    def stacked(new, key):
        parts = [st[key] for st in new if key in st]
        return parts[0][None] if len(parts) == 1 else jnp.stack(parts)

    outs = [xp.reshape(bp, sp, d), xs.reshape(bs, ts, d)]
    for key in ('conv', 'ret', 'pool', 'k', 'v', 'ffn'):
        outs += [stacked(new_p, key), stacked(new_s, key)]
    return tuple(outs)
```
